```python
import math
import jax, jax.numpy as jnp
from jax import lax
import numpy as np

D_MODEL = 2048
BATCH = 8
SEQ = 4096
DEPTH = 1
DEC_BATCH = 8
DEC_SEQ = 16
PAST_LEN = 1024

CHUNK = 64
RET_HEADS = 8
RET_DK = D_MODEL // RET_HEADS
RET_DV = 2 * D_MODEL // RET_HEADS
RET_QK = RET_HEADS * RET_DK
RET_V = RET_HEADS * RET_DV
ROPE_BASE = 10000.0
SSM_INNER = 2 * D_MODEL
SSM_HEAD_DIM = 64
SSM_HEADS = SSM_INNER // SSM_HEAD_DIM
SSM_GROUPS = 8
SSM_HPG = SSM_HEADS // SSM_GROUPS
SSM_STATE = 128
SSM_CONV = 4
SSM_XBC = SSM_INNER + 2 * SSM_GROUPS * SSM_STATE
D_FF = 5632
N_BRANCH = 2
EPS = 1e-6
IN_SPLITS = (RET_QK, RET_QK, RET_V, RET_V, SSM_INNER, SSM_XBC, SSM_HEADS, N_BRANCH * D_MODEL)
IN_TOTAL = sum(IN_SPLITS)

kernel_name = "hybrid_retention_ssd_macaron_step"


def _rmsnorm(x, w):
    xf = x.astype(jnp.float32)
    y = xf * lax.rsqrt(jnp.mean(xf * xf, axis=-1, keepdims=True) + EPS)
    return (y * w.astype(jnp.float32)).astype(x.dtype)


def _swiglu(x, w_gate, w_up, w_down):
    return (jax.nn.silu(x @ w_gate) * (x @ w_up)) @ w_down


def _rope(x, pos):
    half = x.shape[-1] // 2
    inv = ROPE_BASE ** (-jnp.arange(half, dtype=jnp.float32) / half)
    ang = pos[:, None] * inv[None, :]
    cos = jnp.cos(ang)[None, :, None, :]
    sin = jnp.sin(ang)[None, :, None, :]
    xf = x.astype(jnp.float32)
    x1, x2 = xf[..., :half], xf[..., half:]
    return jnp.concatenate([x1 * cos - x2 * sin, x1 * sin + x2 * cos], axis=-1)


def _to_chunks(a, chunk):
    b, l = a.shape[:2]
    return jnp.moveaxis(a.reshape((b, l // chunk, chunk) + a.shape[2:]), 1, 0)


def _from_chunks(a):
    a = jnp.moveaxis(a, 0, 1)
    return a.reshape((a.shape[0], a.shape[1] * a.shape[2]) + a.shape[3:])


def _retention(q, k, v, s0, chunk):
    lg = jnp.log1p(-jnp.exp2(-5.0 - jnp.arange(RET_HEADS, dtype=jnp.float32)))
    idx = jnp.arange(chunk, dtype=jnp.float32)
    diff = idx[:, None] - idx[None, :]
    causal = diff >= 0
    decay_mask = jnp.where(causal[None], jnp.exp(jnp.where(causal, diff, 0.0)[None] * lg[:, None, None]), 0.0)
    inner_decay = jnp.exp((idx[:, None] + 1.0) * lg[None, :])
    state_decay = jnp.exp((chunk - 1.0 - idx[:, None]) * lg[None, :])
    chunk_decay = jnp.exp(chunk * lg)

    def step(s, inp):
        qc, kc, vc = inp
        scores = jnp.einsum('bihd,bjhd->bhij', qc, kc) * decay_mask[None]
        o = (jnp.einsum('bhij,bjhv->bihv', scores, vc)
             + jnp.einsum('bihd,bhdv->bihv', qc, s) * inner_decay[None, :, :, None])
        s_new = (s * chunk_decay[None, :, None, None]
                 + jnp.einsum('bjhd,bjhv->bhdv', kc * state_decay[None, :, :, None], vc))
        return s_new, o

    s, o = lax.scan(step, s0, (_to_chunks(q, chunk), _to_chunks(k, chunk), _to_chunks(v, chunk)))
    return _from_chunks(o), s


def _ssd(x, dt, a, bm, cm, h0, chunk):
    tri = jnp.tril(jnp.ones((chunk, chunk), dtype=bool))[None, :, :, None, None]

    def step(h, inp):
        xc, dtc, bc, cc = inp
        cum = jnp.cumsum(dtc * a[None, None], axis=1)
        seg = cum[:, :, None] - cum[:, None, :]
        lmat = jnp.where(tri, jnp.exp(jnp.where(tri, seg, 0.0)), 0.0)
        xdt = xc * dtc[..., None]
        cb = jnp.einsum('bign,bjgn->bijg', cc, bc)
        y = (jnp.einsum('bijg,bijgh,bjghp->bighp', cb, lmat, xdt)
             + jnp.einsum('bign,bghpn->bighp', cc, h) * jnp.exp(cum)[..., None])
        dend = jnp.exp(cum[:, -1:] - cum)
        h_new = (h * jnp.exp(cum[:, -1])[..., None, None]
                 + jnp.einsum('bjgn,bjgh,bjghp->bghpn', bc, dend, xdt))
        return h_new, y

    h, y = lax.scan(step, h0, (_to_chunks(x, chunk), _to_chunks(dt, chunk),
                               _to_chunks(bm, chunk), _to_chunks(cm, chunk)))
    return _from_chunks(y), h


def _layer(x, s_ret, s_ssm, s_conv, pos0, chunk, p):
    b, l, _ = x.shape
    f32 = jnp.float32
    h = x + 0.5 * _swiglu(_rmsnorm(x, p['norm_ffn1']), p['ffn1_w_gate'], p['ffn1_w_up'], p['ffn1_w_down'])
    n = _rmsnorm(h, p['norm_mix'])
    proj = n @ p['w_in']
    offs = [int(o) for o in np.cumsum(IN_SPLITS)[:-1]]
    q, k, v, g_ret, z, xbc, dt_raw, gates = jnp.split(proj, offs, axis=-1)

    pos = jnp.arange(l, dtype=f32) + pos0
    qh = _rope(q.reshape(b, l, RET_HEADS, RET_DK), pos)
    kh = _rope(k.reshape(b, l, RET_HEADS, RET_DK), pos) * (RET_DK ** -0.5)
    vh = v.reshape(b, l, RET_HEADS, RET_DV).astype(f32)
    o, s_ret_new = _retention(qh, kh, vh, s_ret.astype(f32), chunk)
    mu = jnp.mean(o, axis=-1, keepdims=True)
    var = jnp.mean(jnp.square(o - mu), axis=-1, keepdims=True)
    o = ((o - mu) * lax.rsqrt(var + EPS)).reshape(b, l, RET_V) * p['ret_norm_w'].astype(f32)
    y_ret = (jax.nn.silu(g_ret.astype(f32)) * o).astype(x.dtype) @ p['w_out_ret']

    conv_in = jnp.concatenate([s_conv.astype(xbc.dtype), xbc], axis=1)
    conv = p['conv_b'] + sum(conv_in[:, i:i + l] * p['conv_w'][i] for i in range(SSM_CONV))
    conv = jax.nn.silu(conv)
    conv_new = conv_in[:, -(SSM_CONV - 1):]
    xs, bm, cm = jnp.split(conv, [SSM_INNER, SSM_INNER + SSM_GROUPS * SSM_STATE], axis=-1)
    xs = xs.reshape(b, l, SSM_GROUPS, SSM_HPG, SSM_HEAD_DIM).astype(f32)
    bm = bm.reshape(b, l, SSM_GROUPS, SSM_STATE).astype(f32)
    cm = cm.reshape(b, l, SSM_GROUPS, SSM_STATE).astype(f32)
    dt = jax.nn.softplus(dt_raw.astype(f32) + p['dt_bias'].astype(f32)).reshape(b, l, SSM_GROUPS, SSM_HPG)
    a = -jnp.exp(p['a_log'].astype(f32)).reshape(SSM_GROUPS, SSM_HPG)
    h0 = s_ssm.astype(f32).reshape(b, SSM_GROUPS, SSM_HPG, SSM_HEAD_DIM, SSM_STATE)
    ys, h_ssm = _ssd(xs, dt, a, bm, cm, h0, chunk)
    ys = ys + xs * p['d_skip'].astype(f32).reshape(SSM_GROUPS, SSM_HPG)[..., None]
    ys = ys.reshape(b, l, SSM_INNER) * jax.nn.silu(z.astype(f32))
    yg = ys.reshape(b, l, SSM_GROUPS, SSM_INNER // SSM_GROUPS)
    yg = yg * lax.rsqrt(jnp.mean(yg * yg, axis=-1, keepdims=True) + EPS)
    yg = yg.reshape(b, l, SSM_INNER) * p['ssm_norm_w'].astype(f32)
    y_ssm = yg.astype(x.dtype) @ p['w_out_ssm']

    g_a, g_b = jnp.split(jax.nn.sigmoid(gates + p['b_gate']), 2, axis=-1)
    h = h + (g_a * y_ret + g_b * y_ssm) @ p['w_out']
    h = h + 0.5 * _swiglu(_rmsnorm(h, p['norm_ffn2']), p['ffn2_w_gate'], p['ffn2_w_up'], p['ffn2_w_down'])
    s_ssm_new = h_ssm.reshape(b, SSM_HEADS, SSM_HEAD_DIM, SSM_STATE)
    return h, s_ret_new.astype(x.dtype), s_ssm_new.astype(x.dtype), conv_new


def setup_inputs(seed: int = 0) -> dict:
    key = jax.random.key(seed)
    ks = jax.random.split(key, 32)
    f32 = jnp.float32

    def nrm(k, shape, scale):
        return jax.random.normal(k, shape, f32) * scale

    def gain(k, n):
        return 1.0 + 0.02 * jax.random.normal(k, (n,), f32)

    dt0 = jnp.exp(jax.random.uniform(ks[16], (SSM_HEADS,), f32) * (math.log(0.1) - math.log(1e-3)) + math.log(1e-3))
    return {
        "x_prompt": nrm(ks[0], (BATCH, SEQ, D_MODEL), 1.0),
        "x_sample": nrm(ks[1], (DEC_BATCH, DEC_SEQ, D_MODEL), 1.0),
        "state_ret": nrm(ks[2], (DEC_BATCH, RET_HEADS, RET_DK, RET_DV), 0.5),
        "state_ssm": nrm(ks[3], (DEC_BATCH, SSM_HEADS, SSM_HEAD_DIM, SSM_STATE), 0.5),
        "state_conv": nrm(ks[4], (DEC_BATCH, SSM_CONV - 1, SSM_XBC), 1.0),
        "norm_ffn1": gain(ks[5], D_MODEL),
        "ffn1_w_gate": nrm(ks[6], (D_MODEL, D_FF), D_MODEL ** -0.5),
        "ffn1_w_up": nrm(ks[7], (D_MODEL, D_FF), D_MODEL ** -0.5),
        "ffn1_w_down": nrm(ks[8], (D_FF, D_MODEL), D_FF ** -0.5),
        "norm_mix": gain(ks[9], D_MODEL),
        "w_in": nrm(ks[10], (D_MODEL, IN_TOTAL), D_MODEL ** -0.5),
        "b_gate": nrm(ks[11], (N_BRANCH * D_MODEL,), 0.02),
        "ret_norm_w": gain(ks[12], RET_V),
        "w_out_ret": nrm(ks[13], (RET_V, D_MODEL), RET_V ** -0.5),
        "conv_w": nrm(ks[14], (SSM_CONV, SSM_XBC), SSM_CONV ** -0.5),
        "conv_b": nrm(ks[15], (SSM_XBC,), 0.02),
        "dt_bias": dt0 + jnp.log(-jnp.expm1(-dt0)),
        "a_log": jnp.log(jax.random.uniform(ks[17], (SSM_HEADS,), f32, 1.0, 16.0)),
        "d_skip": gain(ks[18], SSM_HEADS),
        "ssm_norm_w": gain(ks[19], SSM_INNER),
        "w_out_ssm": nrm(ks[20], (SSM_INNER, D_MODEL), SSM_INNER ** -0.5),
        "w_out": nrm(ks[21], (D_MODEL, D_MODEL), D_MODEL ** -0.5),
        "norm_ffn2": gain(ks[22], D_MODEL),
        "ffn2_w_gate": nrm(ks[23], (D_MODEL, D_FF), D_MODEL ** -0.5),
        "ffn2_w_up": nrm(ks[24], (D_MODEL, D_FF), D_MODEL ** -0.5),
        "ffn2_w_down": nrm(ks[25], (D_FF, D_MODEL), D_FF ** -0.5),
        "norm_final": gain(ks[26], D_MODEL),
    }


def reference(x_prompt, x_sample, state_ret, state_ssm, state_conv,
              norm_ffn1, ffn1_w_gate, ffn1_w_up, ffn1_w_down, norm_mix, w_in, b_gate,
              ret_norm_w, w_out_ret, conv_w, conv_b, dt_bias, a_log, d_skip, ssm_norm_w,
              w_out_ssm, w_out, norm_ffn2, ffn2_w_gate, ffn2_w_up, ffn2_w_down, norm_final):
    p = dict(norm_ffn1=norm_ffn1, ffn1_w_gate=ffn1_w_gate, ffn1_w_up=ffn1_w_up, ffn1_w_down=ffn1_w_down,
             norm_mix=norm_mix, w_in=w_in, b_gate=b_gate, ret_norm_w=ret_norm_w, w_out_ret=w_out_ret,
             conv_w=conv_w, conv_b=conv_b, dt_bias=dt_bias, a_log=a_log, d_skip=d_skip,
             ssm_norm_w=ssm_norm_w, w_out_ssm=w_out_ssm, w_out=w_out, norm_ffn2=norm_ffn2,
             ffn2_w_gate=ffn2_w_gate, ffn2_w_up=ffn2_w_up, ffn2_w_down=ffn2_w_down)
    b, l, _ = x_prompt.shape
    ret_p0 = jnp.zeros((b, RET_HEADS, RET_DK, RET_DV), jnp.float32)
    ssm_p0 = jnp.zeros((b, SSM_HEADS, SSM_HEAD_DIM, SSM_STATE), jnp.float32)
    conv_p0 = jnp.zeros((b, SSM_CONV - 1, SSM_XBC), x_prompt.dtype)
    hp, hs = x_prompt, x_sample
    for _ in range(DEPTH):
        hp, ret_p, ssm_p, conv_p = _layer(hp, ret_p0, ssm_p0, conv_p0, 0, min(CHUNK, l), p)
        hs, ret_s, ssm_s, conv_s = _layer(hs, state_ret, state_ssm, state_conv, PAST_LEN, hs.shape[1], p)
    y_prompt = _rmsnorm(hp, norm_final)
    y_sample = _rmsnorm(hs, norm_final)
    return (y_prompt, y_sample, ret_p, ssm_p, conv_p, ret_s, ssm_s, conv_s)
```

```python
import functools
import math

import jax
import jax.numpy as jnp
from jax import lax
from jax.experimental import pallas as pl
from jax.experimental.pallas import tpu as pltpu

F32 = jnp.float32
BF16 = jnp.bfloat16

D_MODEL = 2048
PAST_LEN = 1024
RET_HEADS = 8
RET_DK = D_MODEL // RET_HEADS
RET_DV = 2 * D_MODEL // RET_HEADS
RET_QK = RET_HEADS * RET_DK
RET_V = RET_HEADS * RET_DV
ROPE_BASE = 10000.0
ROPE_HALF = RET_DK // 2
SSM_INNER = 2 * D_MODEL
SSM_HEAD_DIM = 64
SSM_HEADS = SSM_INNER // SSM_HEAD_DIM
SSM_GROUPS = 8
SSM_HPG = SSM_HEADS // SSM_GROUPS
SSM_STATE = 128
SSM_CONV = 4
SSM_BC = SSM_GROUPS * SSM_STATE
SSM_XBC = SSM_INNER + 2 * SSM_BC
SSM_GW = SSM_INNER // SSM_GROUPS
D_FF = 5632
EPS = 1e-6

LANES = 128
SUBLANES = 8
V7X_VMEM_BYTES = 64 * 1024 * 1024

OFF_Q = 0
OFF_K = OFF_Q + RET_QK
OFF_V = OFF_K + RET_QK
OFF_G = OFF_V + RET_V
OFF_Z = OFF_G + RET_V
OFF_X = OFF_Z + SSM_INNER
OFF_B = OFF_X + SSM_INNER
OFF_C = OFF_B + SSM_BC
OFF_GA = OFF_C + SSM_BC
OFF_GB = OFF_GA + D_MODEL
PROJ_W = OFF_GB + D_MODEL
DT_PAD = LANES
CONV_PAD = SUBLANES


def _vmem_limit(nbytes):
    return int(min(V7X_VMEM_BYTES - 4 * 1024 * 1024, max(nbytes, 16 * 1024 * 1024)))


def _dot(a, b):
    return jnp.dot(a.astype(BF16), b.astype(BF16), preferred_element_type=F32)


def _dot_nt(a, b):
    return lax.dot_general(a.astype(BF16), b.astype(BF16), (((1,), (1,)), ((), ())),
                           preferred_element_type=F32)


def _dot_tn(a, b):
    t = a.shape[0]
    pad = (-t) % LANES
    a = a.astype(F32)
    b = b.astype(BF16)
    if pad:
        a = jnp.concatenate([a, jnp.zeros((pad, a.shape[1]), F32)], axis=0)
        b = jnp.concatenate([b, jnp.zeros((pad, b.shape[1]), BF16)], axis=0)
    return jnp.dot(a.T.astype(BF16), b, preferred_element_type=F32)


def _rms(x, w):
    return x * lax.rsqrt(jnp.mean(x * x, axis=-1, keepdims=True) + EPS) * w


def _silu(x):
    return x * jax.nn.sigmoid(x)


def _softplus(x):
    return jnp.maximum(x, 0.0) + jnp.log1p(jnp.exp(-jnp.abs(x)))


def _split3(x):
    hi = x.astype(BF16)
    r = x - hi.astype(F32)
    mid = r.astype(BF16)
    lo = (r - mid.astype(F32)).astype(BF16)
    return hi, mid, lo


def _ffn_kernel(x_ref, nw_ref, wg_ref, wu_ref, wd_ref, fw_ref, o_ref, n_ref, acc_ref, *, final_norm):
    j = pl.program_id(1)

    @pl.when(j == 0)
    def _():
        n_ref[...] = _rms(x_ref[...], nw_ref[...]).astype(BF16)
        acc_ref[...] = jnp.zeros_like(acc_ref)

    n = n_ref[...]
    g = jnp.dot(n, wg_ref[...], preferred_element_type=F32)
    u = jnp.dot(n, wu_ref[...], preferred_element_type=F32)
    a = (_silu(g) * u).astype(BF16)
    acc_ref[...] += jnp.dot(a, wd_ref[...], preferred_element_type=F32)

    @pl.when(j == pl.num_programs(1) - 1)
    def _():
        h = x_ref[...] + 0.5 * acc_ref[...]
        if final_norm:
            h = _rms(h, fw_ref[...])
        o_ref[...] = h


def _ffn(x, norm_w, wg, wu, wd, final_w, *, final_norm):
    rows = x.shape[0]
    tm = min(512, rows)
    tf = 512
    est = (2 * 2 * tm * D_MODEL * 4 + tm * D_MODEL * (4 + 2) + 2 * 3 * D_MODEL * tf * 2
           + 3 * tm * tf * 4 + tm * D_MODEL * 4)
    return pl.pallas_call(
        functools.partial(_ffn_kernel, final_norm=final_norm),
        grid=(rows // tm, D_FF // tf),
        in_specs=[
            pl.BlockSpec((tm, D_MODEL), lambda i, j: (i, 0)),
            pl.BlockSpec((1, D_MODEL), lambda i, j: (0, 0)),
            pl.BlockSpec((D_MODEL, tf), lambda i, j: (0, j)),
            pl.BlockSpec((D_MODEL, tf), lambda i, j: (0, j)),
            pl.BlockSpec((tf, D_MODEL), lambda i, j: (j, 0)),
            pl.BlockSpec((1, D_MODEL), lambda i, j: (0, 0)),
        ],
        out_specs=pl.BlockSpec((tm, D_MODEL), lambda i, j: (i, 0)),
        out_shape=jax.ShapeDtypeStruct((rows, D_MODEL), F32),
        scratch_shapes=[pltpu.VMEM((tm, D_MODEL), BF16), pltpu.VMEM((tm, D_MODEL), F32)],
        compiler_params=pltpu.CompilerParams(
            dimension_semantics=("parallel", "arbitrary"), vmem_limit_bytes=_vmem_limit(est)),
        name="ffn_final" if final_norm else "ffn",
    )(x, norm_w, wg, wu, wd, final_w)


def _proj_kernel(h_ref, nw_ref, w_ref, wdt_ref, wdtt_ref, inv_ref, o_ref, dt_ref, dtt_ref,
                 n_ref, cos_ref, sin_ref, *, seq, pos0, tn):
    i = pl.program_id(0)
    j = pl.program_id(1)
    tm = h_ref.shape[0]
    n_q = RET_QK // tn

    @pl.when(j == 0)
    def _():
        n = _rms(h_ref[...], nw_ref[...]).astype(BF16)
        n_ref[...] = n
        dt_ref[...] = jnp.dot(n, wdt_ref[...], preferred_element_type=F32)
        dtt_ref[...] = _dot_nt(wdtt_ref[...], n)
        row = i * tm + lax.broadcasted_iota(jnp.int32, (tm, 1), 0)
        pos = ((row & (seq - 1)) + pos0).astype(F32)
        ang = pos * inv_ref[...]
        cos_ref[...] = jnp.cos(ang)
        sin_ref[...] = jnp.sin(ang)

    acc = jnp.dot(n_ref[...], w_ref[...], preferred_element_type=F32)

    @pl.when(j < 2 * n_q)
    def _():
        cos = cos_ref[...]
        sin = sin_ref[...]
        scale = jnp.where(j < n_q, 1.0, RET_DK ** -0.5).astype(F32)
        for hd in range(tn // RET_DK):
            lo = hd * RET_DK
            x1 = acc[:, lo:lo + ROPE_HALF]
            x2 = acc[:, lo + ROPE_HALF:lo + RET_DK]
            o_ref[:, lo:lo + ROPE_HALF] = ((x1 * cos - x2 * sin) * scale).astype(BF16)
            o_ref[:, lo + ROPE_HALF:lo + RET_DK] = ((x1 * sin + x2 * cos) * scale).astype(BF16)

    @pl.when(j >= 2 * n_q)
    def _():
        o_ref[...] = acc.astype(BF16)


def _proj(h, norm_w, w_main, w_dt, w_dtt, inv_freq, *, seq, pos0):
    rows = h.shape[0]
    tm = min(1024, rows)
    tn = 1024
    assert seq & (seq - 1) == 0 and RET_QK % tn == 0 and PROJ_W % tn == 0
    est = (2 * tm * D_MODEL * 4 + tm * D_MODEL * 2 + 2 * D_MODEL * tn * 2 + 2 * tm * tn * 2
           + 2 * tm * tn * 4 + 4 * tm * LANES * 4 + 4 * D_MODEL * DT_PAD * 2 + 4 * tm * DT_PAD * 4)
    return pl.pallas_call(
        functools.partial(_proj_kernel, seq=seq, pos0=pos0, tn=tn),
        grid=(rows // tm, PROJ_W // tn),
        in_specs=[
            pl.BlockSpec((tm, D_MODEL), lambda i, j: (i, 0)),
            pl.BlockSpec((1, D_MODEL), lambda i, j: (0, 0)),
            pl.BlockSpec((D_MODEL, tn), lambda i, j: (0, j)),
            pl.BlockSpec((D_MODEL, DT_PAD), lambda i, j: (0, 0)),
            pl.BlockSpec((DT_PAD, D_MODEL), lambda i, j: (0, 0)),
            pl.BlockSpec((1, ROPE_HALF), lambda i, j: (0, 0)),
        ],
        out_specs=[
            pl.BlockSpec((tm, tn), lambda i, j: (i, j)),
            pl.BlockSpec((tm, DT_PAD), lambda i, j: (i, 0)),
            pl.BlockSpec((DT_PAD, tm), lambda i, j: (0, i)),
        ],
        out_shape=[
            jax.ShapeDtypeStruct((rows, PROJ_W), BF16),
            jax.ShapeDtypeStruct((rows, DT_PAD), F32),
            jax.ShapeDtypeStruct((DT_PAD, rows), F32),
        ],
        scratch_shapes=[pltpu.VMEM((tm, D_MODEL), BF16), pltpu.VMEM((tm, ROPE_HALF), F32),
                        pltpu.VMEM((tm, ROPE_HALF), F32)],
        compiler_params=pltpu.CompilerParams(
            dimension_semantics=("parallel", "arbitrary"), vmem_limit_bytes=_vmem_limit(est)),
        name="proj",
    )(h, norm_w, w_main, w_dt, w_dtt, inv_freq)


def _ret_kernel(*refs, chunk, has_init):
    if has_init:
        q_ref, k_ref, v_ref, g_ref, nw_ref, s0_ref, o_ref, s_ref = refs
    else:
        q_ref, k_ref, v_ref, g_ref, nw_ref, o_ref, s_ref = refs
    c = pl.program_id(1)

    @pl.when(c == 0)
    def _():
        if has_init:
            s_ref[...] = s0_ref[...]
        else:
            s_ref[...] = jnp.zeros_like(s_ref)

    ii = lax.broadcasted_iota(jnp.int32, (chunk, chunk), 0)
    jj = lax.broadcasted_iota(jnp.int32, (chunk, chunk), 1)
    causal = ii >= jj
    diff = jnp.where(causal, ii - jj, 0).astype(F32)
    idx = lax.broadcasted_iota(jnp.int32, (chunk, 1), 0).astype(F32)
    for hd in range(RET_HEADS):
        lg = math.log1p(-(2.0 ** (-5.0 - hd)))
        decay_mask = jnp.where(causal, jnp.exp(diff * lg), 0.0)
        inner_decay = jnp.exp((idx + 1.0) * lg)
        state_decay = jnp.exp((chunk - 1.0 - idx) * lg)
        chunk_decay = math.exp(chunk * lg)
        q = q_ref[0, :, hd * RET_DK:(hd + 1) * RET_DK]
        k = k_ref[0, :, hd * RET_DK:(hd + 1) * RET_DK]
        v = v_ref[0, :, hd * RET_DV:(hd + 1) * RET_DV]
        s = s_ref[0, hd]
        scores = _dot_nt(q, k) * decay_mask
        o = _dot(scores, v) + _dot(q, s) * inner_decay
        s_ref[0, hd] = s * chunk_decay + _dot_tn(k.astype(F32) * state_decay, v)
        mu = jnp.mean(o, axis=-1, keepdims=True)
        var = jnp.mean(jnp.square(o - mu), axis=-1, keepdims=True)
        on = (o - mu) * lax.rsqrt(var + EPS) * nw_ref[:, hd * RET_DV:(hd + 1) * RET_DV]
        gate = g_ref[0, :, hd * RET_DV:(hd + 1) * RET_DV].astype(F32)
        o_ref[0, :, hd * RET_DV:(hd + 1) * RET_DV] = (_silu(gate) * on).astype(BF16)


def _retention(proj3, norm_w, s0, *, chunk):
    b, seq, _ = proj3.shape
    has_init = s0 is not None
    state_block = pl.BlockSpec((1, RET_HEADS, RET_DK, RET_DV), lambda bi, c: (bi, 0, 0, 0))
    in_specs = [
        pl.BlockSpec((1, chunk, RET_QK), lambda bi, c: (bi, c, OFF_Q // RET_QK)),
        pl.BlockSpec((1, chunk, RET_QK), lambda bi, c: (bi, c, OFF_K // RET_QK)),
        pl.BlockSpec((1, chunk, RET_V), lambda bi, c: (bi, c, OFF_V // RET_V)),
        pl.BlockSpec((1, chunk, RET_V), lambda bi, c: (bi, c, OFF_G // RET_V)),
        pl.BlockSpec((1, RET_V), lambda bi, c: (0, 0)),
    ]
    args = [proj3, proj3, proj3, proj3, norm_w]
    if has_init:
        in_specs.append(state_block)
        args.append(s0)
    state_bytes = RET_HEADS * RET_DK * RET_DV * 4
    est = (2 * chunk * (2 * RET_QK + 3 * RET_V) * 2 + (4 if has_init else 2) * state_bytes
           + 8 * chunk * max(chunk, RET_DV) * 4)
    return pl.pallas_call(
        functools.partial(_ret_kernel, chunk=chunk, has_init=has_init),
        grid=(b, seq // chunk),
        in_specs=in_specs,
        out_specs=[pl.BlockSpec((1, chunk, RET_V), lambda bi, c: (bi, c, 0)), state_block],
        out_shape=[jax.ShapeDtypeStruct((b, seq, RET_V), BF16),
                   jax.ShapeDtypeStruct((b, RET_HEADS, RET_DK, RET_DV), F32)],
        compiler_params=pltpu.CompilerParams(
            dimension_semantics=("parallel", "arbitrary"), vmem_limit_bytes=_vmem_limit(est)),
        name="retention",
    )(*args)


def _ssd_kernel(*refs, chunk, has_init):
    if has_init:
        (z_ref, x_ref, b_ref, c_ref, dt_ref, dtt_ref, cw_ref, cb_ref, dtb_ref, dtbt_ref, al_ref,
         alt_ref, dsk_ref, nw_ref, h0_ref, cs0_ref, y_ref, hout_ref, csout_ref,
         ht_ref, ex_ref, eb_ref, ec_ref) = refs
    else:
        (z_ref, x_ref, b_ref, c_ref, dt_ref, dtt_ref, cw_ref, cb_ref, dtb_ref, dtbt_ref, al_ref,
         alt_ref, dsk_ref, nw_ref, y_ref, hout_ref, csout_ref,
         ht_ref, ex_ref, eb_ref, ec_ref) = refs
    c = pl.program_id(1)
    taps = SSM_CONV - 1
    t0 = CONV_PAD - taps

    @pl.when(c == 0)
    def _():
        for g in range(SSM_GROUPS):
            if has_init:
                blk = h0_ref[0, g * SSM_HPG:(g + 1) * SSM_HPG].reshape(SSM_GW, SSM_STATE)
                ht_ref[g] = blk.T
            else:
                ht_ref[g] = jnp.zeros((SSM_STATE, SSM_GW), F32)
        if has_init:
            ex_ref[t0:CONV_PAD, :] = cs0_ref[0, :, 0:SSM_INNER]
            eb_ref[t0:CONV_PAD, :] = cs0_ref[0, :, SSM_INNER:SSM_INNER + SSM_BC]
            ec_ref[t0:CONV_PAD, :] = cs0_ref[0, :, SSM_INNER + SSM_BC:SSM_XBC]
        else:
            ex_ref[t0:CONV_PAD, :] = jnp.zeros((taps, SSM_INNER), F32)
            eb_ref[t0:CONV_PAD, :] = jnp.zeros((taps, SSM_BC), F32)
            ec_ref[t0:CONV_PAD, :] = jnp.zeros((taps, SSM_BC), F32)

    ex_ref[CONV_PAD:CONV_PAD + chunk, :] = x_ref[0].astype(F32)
    eb_ref[CONV_PAD:CONV_PAD + chunk, :] = b_ref[0].astype(F32)
    ec_ref[CONV_PAD:CONV_PAD + chunk, :] = c_ref[0].astype(F32)

    def conv(e_ref, woff, lo, width):
        acc = cb_ref[:, woff + lo:woff + lo + width]
        for i in range(SSM_CONV):
            acc = acc + e_ref[t0 + i:t0 + i + chunk, lo:lo + width] * cw_ref[i:i + 1, woff + lo:woff + lo + width]
        return _silu(acc)

    a_row = -jnp.exp(al_ref[...])
    a_col = -jnp.exp(alt_ref[...])
    dt = _softplus(dt_ref[0] + dtb_ref[...])
    dtt = _softplus(dtt_ref[0] + dtbt_ref[...])
    ii = lax.broadcasted_iota(jnp.int32, (chunk, chunk), 0)
    jj = lax.broadcasted_iota(jnp.int32, (chunk, chunk), 1)
    tri = ii >= jj
    lower = tri.astype(BF16)
    upper = (ii <= jj).astype(BF16)
    cum = sum(jnp.dot(lower, p, preferred_element_type=F32) for p in _split3(dt * a_row))
    cumt = sum(jnp.dot(p, upper, preferred_element_type=F32) for p in _split3(dtt * a_col))
    ecum = jnp.exp(cum)
    dend = jnp.exp(cum[chunk - 1:chunk, :] - cum)

    first_half = lax.broadcasted_iota(jnp.int32, (chunk, LANES), 1) < SSM_HEAD_DIM

    def pair(arr, j0):
        return jnp.where(first_half,
                         jnp.broadcast_to(arr[:, j0:j0 + 1], (chunk, LANES)),
                         jnp.broadcast_to(arr[:, j0 + 1:j0 + 2], (chunk, LANES)))

    def lmat(j):
        seg = jnp.broadcast_to(cum[:, j:j + 1], (chunk, chunk)) - jnp.broadcast_to(cumt[j:j + 1, :], (chunk, chunk))
        return jnp.where(tri, jnp.exp(jnp.where(tri, seg, 0.0)), 0.0)

    for g in range(SSM_GROUPS):
        xs = conv(ex_ref, 0, g * SSM_GW, SSM_GW)
        bm = conv(eb_ref, SSM_INNER, g * SSM_STATE, SSM_STATE)
        cm = conv(ec_ref, SSM_INNER + SSM_BC, g * SSM_STATE, SSM_STATE)
        cb = _dot_nt(cm, bm)
        y_parts, e_parts, xd_parts = [], [], []
        for s in range(SSM_GW // LANES):
            j0 = g * SSM_HPG + 2 * s
            xdt = xs[:, s * LANES:(s + 1) * LANES] * pair(dt, j0)
            m0 = cb * lmat(j0)
            m1 = cb * lmat(j0 + 1)
            y_parts.append(_dot(m0, jnp.where(first_half, xdt, 0.0))
                           + _dot(m1, jnp.where(first_half, 0.0, xdt)))
            e_parts.append(pair(ecum, j0))
            xd_parts.append(xdt * pair(dend, j0))
        e_all = jnp.concatenate(e_parts, axis=1)
        htg = ht_ref[g]
        y = (jnp.concatenate(y_parts, axis=1) + _dot(cm, htg) * e_all
             + xs * dsk_ref[:, g * SSM_GW:(g + 1) * SSM_GW])
        ht_ref[g] = htg * e_all[chunk - 1:chunk, :] + _dot_tn(bm, jnp.concatenate(xd_parts, axis=1))
        y = y * _silu(z_ref[0, :, g * SSM_GW:(g + 1) * SSM_GW].astype(F32))
        y = _rms(y, nw_ref[:, g * SSM_GW:(g + 1) * SSM_GW])
        y_ref[0, :, g * SSM_GW:(g + 1) * SSM_GW] = y.astype(BF16)

    tail_x = ex_ref[chunk + t0:chunk + CONV_PAD, :]
    tail_b = eb_ref[chunk + t0:chunk + CONV_PAD, :]
    tail_c = ec_ref[chunk + t0:chunk + CONV_PAD, :]
    ex_ref[t0:CONV_PAD, :] = tail_x
    eb_ref[t0:CONV_PAD, :] = tail_b
    ec_ref[t0:CONV_PAD, :] = tail_c

    @pl.when(c == pl.num_programs(1) - 1)
    def _():
        csout_ref[0, :, 0:SSM_INNER] = tail_x
        csout_ref[0, :, SSM_INNER:SSM_INNER + SSM_BC] = tail_b
        csout_ref[0, :, SSM_INNER + SSM_BC:SSM_XBC] = tail_c
        for g in range(SSM_GROUPS):
            hout_ref[0, g * SSM_HPG:(g + 1) * SSM_HPG] = ht_ref[g].T.reshape(SSM_HPG, SSM_HEAD_DIM, SSM_STATE)


def _ssd(proj3, dt3, dtt3, conv_w, conv_b, dt_bias, a_log, d_skip, norm_w, h0, cs0, *, chunk):
    b, seq, _ = proj3.shape
    has_init = h0 is not None
    assert chunk >= SSM_CONV - 1
    pad = DT_PAD - SSM_HEADS
    dtb = jnp.pad(dt_bias.astype(F32), (0, pad))
    al = jnp.pad(a_log.astype(F32), (0, pad))
    dsk = jnp.repeat(d_skip.astype(F32), SSM_HEAD_DIM).reshape(1, SSM_INNER)
    const = lambda *shape: pl.BlockSpec(shape, lambda bi, c: (0,) * len(shape))
    in_specs = [
        pl.BlockSpec((1, chunk, SSM_INNER), lambda bi, c: (bi, c, OFF_Z // SSM_INNER)),
        pl.BlockSpec((1, chunk, SSM_INNER), lambda bi, c: (bi, c, OFF_X // SSM_INNER)),
        pl.BlockSpec((1, chunk, SSM_BC), lambda bi, c: (bi, c, OFF_B // SSM_BC)),
        pl.BlockSpec((1, chunk, SSM_BC), lambda bi, c: (bi, c, OFF_C // SSM_BC)),
        pl.BlockSpec((1, chunk, DT_PAD), lambda bi, c: (bi, c, 0)),
        pl.BlockSpec((1, DT_PAD, chunk), lambda bi, c: (bi, 0, c)),
        const(SSM_CONV, SSM_XBC), const(1, SSM_XBC), const(1, DT_PAD), const(DT_PAD, 1),
        const(1, DT_PAD), const(DT_PAD, 1), const(1, SSM_INNER), const(1, SSM_INNER),
    ]
    args = [proj3, proj3, proj3, proj3, dt3, dtt3, conv_w.astype(F32), conv_b.astype(F32).reshape(1, SSM_XBC),
            dtb.reshape(1, DT_PAD), dtb.reshape(DT_PAD, 1), al.reshape(1, DT_PAD), al.reshape(DT_PAD, 1),
            dsk, norm_w.astype(F32).reshape(1, SSM_INNER)]
    h_block = pl.BlockSpec((1, SSM_HEADS, SSM_HEAD_DIM, SSM_STATE), lambda bi, c: (bi, 0, 0, 0))
    cs_block = pl.BlockSpec((1, SSM_CONV - 1, SSM_XBC), lambda bi, c: (bi, 0, 0))
    if has_init:
        in_specs += [h_block, cs_block]
        args += [h0, cs0]
    state_bytes = SSM_HEADS * SSM_HEAD_DIM * SSM_STATE * 4
    est = (2 * chunk * (3 * SSM_INNER + 2 * SSM_BC) * 2 + (5 if has_init else 3) * state_bytes
           + (chunk + CONV_PAD) * SSM_XBC * 4 + 24 * chunk * max(chunk, SSM_GW) * 4 + 4 * 1024 * 1024)
    return pl.pallas_call(
        functools.partial(_ssd_kernel, chunk=chunk, has_init=has_init),
        grid=(b, seq // chunk),
        in_specs=in_specs,
        out_specs=[pl.BlockSpec((1, chunk, SSM_INNER), lambda bi, c: (bi, c, 0)), h_block, cs_block],
        out_shape=[jax.ShapeDtypeStruct((b, seq, SSM_INNER), BF16),
                   jax.ShapeDtypeStruct((b, SSM_HEADS, SSM_HEAD_DIM, SSM_STATE), F32),
                   jax.ShapeDtypeStruct((b, SSM_CONV - 1, SSM_XBC), F32)],
        scratch_shapes=[pltpu.VMEM((SSM_GROUPS, SSM_STATE, SSM_GW), F32),
                        pltpu.VMEM((chunk + CONV_PAD, SSM_INNER), F32),
                        pltpu.VMEM((chunk + CONV_PAD, SSM_BC), F32),
                        pltpu.VMEM((chunk + CONV_PAD, SSM_BC), F32)],
        compiler_params=pltpu.CompilerParams(
            dimension_semantics=("parallel", "arbitrary"), vmem_limit_bytes=_vmem_limit(est)),
        name="ssd",
    )(*args)


def _merge_kernel(ar_ref, as_ref, wr_ref, ws_ref, ga_ref, gb_ref, ba_ref, bb_ref, o_ref):
    yr = jnp.dot(ar_ref[...], wr_ref[...], preferred_element_type=F32)
    ys = jnp.dot(as_ref[...], ws_ref[...], preferred_element_type=F32)
    ga = jax.nn.sigmoid(ga_ref[...].astype(F32) + ba_ref[...])
    gb = jax.nn.sigmoid(gb_ref[...].astype(F32) + bb_ref[...])
    o_ref[...] = (ga * yr + gb * ys).astype(BF16)


def _merge(a_ret, a_ssm, w_ret, w_ssm, proj, b_gate):
    rows = a_ret.shape[0]
    tm = min(1024, rows)
    tn = 256
    nb = D_MODEL // tn
    est = 2 * 2 * tm * RET_V * 2 + 2 * 2 * RET_V * tn * 2 + 6 * tm * tn * 4
    return pl.pallas_call(
        _merge_kernel,
        grid=(rows // tm, nb),
        in_specs=[
            pl.BlockSpec((tm, RET_V), lambda i, j: (i, 0)),
            pl.BlockSpec((tm, SSM_INNER), lambda i, j: (i, 0)),
            pl.BlockSpec((RET_V, tn), lambda i, j: (0, j)),
            pl.BlockSpec((SSM_INNER, tn), lambda i, j: (0, j)),
            pl.BlockSpec((tm, tn), lambda i, j: (i, OFF_GA // tn + j)),
            pl.BlockSpec((tm, tn), lambda i, j: (i, OFF_GB // tn + j)),
            pl.BlockSpec((1, tn), lambda i, j: (0, j)),
            pl.BlockSpec((1, tn), lambda i, j: (0, nb + j)),
        ],
        out_specs=pl.BlockSpec((tm, tn), lambda i, j: (i, j)),
        out_shape=jax.ShapeDtypeStruct((rows, D_MODEL), BF16),
        compiler_params=pltpu.CompilerParams(
            dimension_semantics=("parallel", "arbitrary"), vmem_limit_bytes=_vmem_limit(est)),
        name="merge",
    )(a_ret, a_ssm, w_ret, w_ssm, proj, proj, b_gate, b_gate)


def _outproj_kernel(m_ref, w_ref, h_ref, o_ref):
    o_ref[...] = h_ref[...] + jnp.dot(m_ref[...], w_ref[...], preferred_element_type=F32)


def _outproj(m, w_out, h):
    rows = m.shape[0]
    tm = min(1024, rows)
    tn = 512
    est = 2 * tm * D_MODEL * 2 + 2 * D_MODEL * tn * 2 + 5 * tm * tn * 4
    return pl.pallas_call(
        _outproj_kernel,
        grid=(rows // tm, D_MODEL // tn),
        in_specs=[
            pl.BlockSpec((tm, D_MODEL), lambda i, j: (i, 0)),
            pl.BlockSpec((D_MODEL, tn), lambda i, j: (0, j)),
            pl.BlockSpec((tm, tn), lambda i, j: (i, j)),
        ],
        out_specs=pl.BlockSpec((tm, tn), lambda i, j: (i, j)),
        out_shape=jax.ShapeDtypeStruct((rows, D_MODEL), F32),
        compiler_params=pltpu.CompilerParams(
            dimension_semantics=("parallel", "arbitrary"), vmem_limit_bytes=_vmem_limit(est)),
        name="outproj",
    )(m, w_out, h)


def _layer(x, s_ret, s_ssm, s_conv, pos0, p, *, ret_chunk, ssd_chunk):
    b, seq, _ = x.shape
    rows = b * seq
    h = _ffn(x.reshape(rows, D_MODEL), p["norm_ffn1"], p["ffn1_w_gate"], p["ffn1_w_up"], p["ffn1_w_down"],
             p["norm_final"], final_norm=False)
    proj, dt, dtt = _proj(h, p["norm_mix"], p["w_main"], p["w_dt"], p["w_dtt"], p["inv_freq"], seq=seq, pos0=pos0)
    proj3 = proj.reshape(b, seq, PROJ_W)
    dt3 = dt.reshape(b, seq, DT_PAD)
    dtt3 = dtt.reshape(DT_PAD, b, seq).transpose(1, 0, 2)
    a_ret, ret_new = _retention(proj3, p["ret_norm_w"], s_ret, chunk=min(ret_chunk, seq))
    a_ssm, ssm_new, conv_new = _ssd(proj3, dt3, dtt3, p["conv_w"], p["conv_b"], p["dt_bias"], p["a_log"],
                                    p["d_skip"], p["ssm_norm_w"], s_ssm, s_conv, chunk=min(ssd_chunk, seq))
    m = _merge(a_ret.reshape(rows, RET_V), a_ssm.reshape(rows, SSM_INNER), p["w_out_ret"], p["w_out_ssm"],
               proj, p["b_gate"])
    h2 = _outproj(m, p["w_out"], h)
    y = _ffn(h2, p["norm_ffn2"], p["ffn2_w_gate"], p["ffn2_w_up"], p["ffn2_w_down"], p["norm_final"],
             final_norm=True)
    return y.reshape(b, seq, D_MODEL), ret_new, ssm_new, conv_new


def kernel(x_prompt, x_sample, state_ret, state_ssm, state_conv, norm_ffn1, ffn1_w_gate, ffn1_w_up, ffn1_w_down, norm_mix, w_in, b_gate, ret_norm_w, w_out_ret, conv_w, conv_b, dt_bias, a_log, d_skip, ssm_norm_w, w_out_ssm, w_out, norm_ffn2, ffn2_w_gate, ffn2_w_up, ffn2_w_down, norm_final):
    row = lambda v: v.astype(F32).reshape(1, -1)
    dt_lo = 2 * RET_QK + 2 * RET_V + SSM_INNER + SSM_XBC
    w_dt = jnp.pad(w_in[:, dt_lo:dt_lo + SSM_HEADS], ((0, 0), (0, DT_PAD - SSM_HEADS))).astype(BF16)
    p = dict(
        norm_ffn1=row(norm_ffn1), norm_mix=row(norm_mix), norm_ffn2=row(norm_ffn2), norm_final=row(norm_final),
        ffn1_w_gate=ffn1_w_gate.astype(BF16), ffn1_w_up=ffn1_w_up.astype(BF16), ffn1_w_down=ffn1_w_down.astype(BF16),
        ffn2_w_gate=ffn2_w_gate.astype(BF16), ffn2_w_up=ffn2_w_up.astype(BF16), ffn2_w_down=ffn2_w_down.astype(BF16),
        w_main=jnp.concatenate([w_in[:, :dt_lo], w_in[:, dt_lo + SSM_HEADS:]], axis=1).astype(BF16),
        w_dt=w_dt, w_dtt=w_dt.T,
        inv_freq=(ROPE_BASE ** (-jnp.arange(ROPE_HALF, dtype=F32) / ROPE_HALF)).reshape(1, ROPE_HALF),
        b_gate=row(b_gate), ret_norm_w=row(ret_norm_w),
        w_out_ret=w_out_ret.astype(BF16), w_out_ssm=w_out_ssm.astype(BF16), w_out=w_out.astype(BF16),
        conv_w=conv_w, conv_b=conv_b, dt_bias=dt_bias, a_log=a_log, d_skip=d_skip, ssm_norm_w=ssm_norm_w,
    )
    y_p, ret_p, ssm_p, conv_p = _layer(x_prompt, None, None, None, 0, p, ret_chunk=256, ssd_chunk=128)
    y_s, ret_s, ssm_s, conv_s = _layer(x_sample, state_ret.astype(F32), state_ssm.astype(F32),
                                       state_conv.astype(F32), PAST_LEN, p, ret_chunk=256, ssd_chunk=128)
    return (y_p, y_s, ret_p, ssm_p, conv_p, ret_s, ssm_s, conv_s)
```

```python
import functools
import math

import numpy as np
import jax
import jax.numpy as jnp
from jax import lax
from jax.experimental import pallas as pl
from jax.experimental.pallas import tpu as pltpu

F32 = jnp.float32
BF16 = jnp.bfloat16

D_MODEL = 2048
PAST_LEN = 1024
RET_HEADS = 8
RET_DK = D_MODEL // RET_HEADS
RET_DV = 2 * D_MODEL // RET_HEADS
RET_QK = RET_HEADS * RET_DK
RET_V = RET_HEADS * RET_DV
ROPE_BASE = 10000.0
ROPE_HALF = RET_DK // 2
SSM_INNER = 2 * D_MODEL
SSM_HEAD_DIM = 64
SSM_HEADS = SSM_INNER // SSM_HEAD_DIM
SSM_GROUPS = 8
SSM_HPG = SSM_HEADS // SSM_GROUPS
SSM_STATE = 128
SSM_CONV = 4
SSM_BC = SSM_GROUPS * SSM_STATE
SSM_XBC = SSM_INNER + 2 * SSM_BC
SSM_GW = SSM_INNER // SSM_GROUPS
D_FF = 5632
EPS = 1e-6
LOG2E = math.log2(math.e)

LANES = 128
SUBLANES = 8
BF16_SUBLANES = 16
V7X_VMEM_BYTES = 64 * 1024 * 1024

OFF_Q = 0
OFF_K = OFF_Q + RET_QK
OFF_V = OFF_K + RET_QK
OFF_G = OFF_V + RET_V
OFF_Z = OFF_G + RET_V
OFF_X = OFF_Z + SSM_INNER
OFF_B = OFF_X + SSM_INNER
OFF_C = OFF_B + SSM_BC
OFF_GA = OFF_C + SSM_BC
OFF_GB = OFF_GA + D_MODEL
PROJ_W = OFF_GB + D_MODEL
DT_PAD = LANES
assert DT_PAD == 2 * SSM_HEADS
CONV_TAIL = BF16_SUBLANES


def _vmem_limit(nbytes):
    return int(min(V7X_VMEM_BYTES - 4 * 1024 * 1024, max(nbytes, 16 * 1024 * 1024)))


def _dot(a, b):
    return jnp.dot(a.astype(BF16), b.astype(BF16), preferred_element_type=F32)


def _dot_nt(a, b):
    return lax.dot_general(a.astype(BF16), b.astype(BF16), (((1,), (1,)), ((), ())),
                           preferred_element_type=F32)


def _dot_tn(a, b):
    t = a.shape[0]
    pad = (-t) % LANES
    a = a.astype(F32)
    b = b.astype(BF16)
    if pad:
        a = jnp.concatenate([a, jnp.zeros((pad, a.shape[1]), F32)], axis=0)
        b = jnp.concatenate([b, jnp.zeros((pad, b.shape[1]), BF16)], axis=0)
    return jnp.dot(a.T.astype(BF16), b, preferred_element_type=F32)


def _rms(x, w):
    return x * lax.rsqrt(jnp.mean(x * x, axis=-1, keepdims=True) + EPS) * w


def _silu(x):
    return x * jax.nn.sigmoid(x)


def _softplus(x):
    return jnp.maximum(x, 0.0) + jnp.log1p(jnp.exp(-jnp.abs(x)))


def _split3(x):
    hi = x.astype(BF16)
    r = x - hi.astype(F32)
    mid = r.astype(BF16)
    lo = (r - mid.astype(F32)).astype(BF16)
    return hi, mid, lo


def _ffn_kernel(x_ref, nw_ref, wg_ref, wu_ref, wd_ref, fw_ref, o_ref, n_ref, acc_ref, *, final_norm):
    j = pl.program_id(1)

    @pl.when(j == 0)
    def _():
        n_ref[...] = _rms(x_ref[...], nw_ref[...]).astype(BF16)
        acc_ref[...] = jnp.zeros_like(acc_ref)

    n = n_ref[...]
    g = jnp.dot(n, wg_ref[...], preferred_element_type=F32)
    u = jnp.dot(n, wu_ref[...], preferred_element_type=F32)
    a = (_silu(g) * u).astype(BF16)
    acc_ref[...] += jnp.dot(a, wd_ref[...], preferred_element_type=F32)

    @pl.when(j == pl.num_programs(1) - 1)
    def _():
        h = x_ref[...] + 0.5 * acc_ref[...]
        if final_norm:
            h = _rms(h, fw_ref[...])
        o_ref[...] = h


def _ffn(x, norm_w, wg, wu, wd, final_w, *, final_norm):
    rows = x.shape[0]
    tm = min(512, rows)
    tf = 512
    est = (2 * 2 * tm * D_MODEL * 4 + tm * D_MODEL * (4 + 2) + 2 * 3 * D_MODEL * tf * 2
           + 3 * tm * tf * 4 + tm * D_MODEL * 4)
    return pl.pallas_call(
        functools.partial(_ffn_kernel, final_norm=final_norm),
        grid=(rows // tm, D_FF // tf),
        in_specs=[
            pl.BlockSpec((tm, D_MODEL), lambda i, j: (i, 0)),
            pl.BlockSpec((1, D_MODEL), lambda i, j: (0, 0)),
            pl.BlockSpec((D_MODEL, tf), lambda i, j: (0, j)),
            pl.BlockSpec((D_MODEL, tf), lambda i, j: (0, j)),
            pl.BlockSpec((tf, D_MODEL), lambda i, j: (j, 0)),
            pl.BlockSpec((1, D_MODEL), lambda i, j: (0, 0)),
        ],
        out_specs=pl.BlockSpec((tm, D_MODEL), lambda i, j: (i, 0)),
        out_shape=jax.ShapeDtypeStruct((rows, D_MODEL), F32),
        scratch_shapes=[pltpu.VMEM((tm, D_MODEL), BF16), pltpu.VMEM((tm, D_MODEL), F32)],
        compiler_params=pltpu.CompilerParams(
            dimension_semantics=("parallel", "arbitrary"), vmem_limit_bytes=_vmem_limit(est)),
        name="ffn_final" if final_norm else "ffn",
    )(x, norm_w, wg, wu, wd, final_w)


def _rope_kernel(inv_ref, cos_ref, sin_ref, *, seq, pos0):
    tr = cos_ref.shape[0]
    row = pl.program_id(0) * tr + lax.broadcasted_iota(jnp.int32, (tr, 1), 0)
    ang = ((row & (seq - 1)) + pos0).astype(F32) * inv_ref[...]
    cos_ref[...] = jnp.cos(ang)
    sin_ref[...] = jnp.sin(ang)


def _rope_tables(inv_freq, *, length, seq, pos0):
    tr = min(512, length)
    spec = pl.BlockSpec((tr, ROPE_HALF), lambda i: (i, 0))
    return pl.pallas_call(
        functools.partial(_rope_kernel, seq=seq, pos0=pos0),
        grid=(length // tr,),
        in_specs=[pl.BlockSpec((1, ROPE_HALF), lambda i: (0, 0))],
        out_specs=[spec, spec],
        out_shape=[jax.ShapeDtypeStruct((length, ROPE_HALF), F32)] * 2,
        name="rope",
    )(inv_freq)


def _proj_kernel(h_ref, nw_ref, wa_ref, wg_ref, wdt_ref, wdtt_ref, cos_ref, sin_ref, o_ref, dt_ref, dtt_ref,
                 n_ref, *, tn, batched_dtt):
    j = pl.program_id(1)
    n_q = RET_QK // tn
    n_main = OFF_GA // tn

    @pl.when(j == 0)
    def _():
        n = _rms(h_ref[...], nw_ref[...]).astype(BF16)
        n_ref[...] = n
        dt_ref[...] = jnp.dot(n, wdt_ref[...], preferred_element_type=F32)
        dtt = _dot_nt(wdtt_ref[...], n)
        if batched_dtt:
            dtt_ref[0] = dtt
        else:
            dtt_ref[...] = dtt

    @pl.when(j < 2 * n_q)
    def _():
        acc = jnp.dot(n_ref[...], wa_ref[...], preferred_element_type=F32)
        cos = cos_ref[...]
        sin = sin_ref[...]
        scale = jnp.where(j < n_q, 1.0, RET_DK ** -0.5).astype(F32)
        for hd in range(tn // RET_DK):
            lo = hd * RET_DK
            x1 = acc[:, lo:lo + ROPE_HALF]
            x2 = acc[:, lo + ROPE_HALF:lo + RET_DK]
            o_ref[:, lo:lo + ROPE_HALF] = ((x1 * cos - x2 * sin) * scale).astype(BF16)
            o_ref[:, lo + ROPE_HALF:lo + RET_DK] = ((x1 * sin + x2 * cos) * scale).astype(BF16)

    @pl.when((j >= 2 * n_q) & (j < n_main))
    def _():
        o_ref[...] = jnp.dot(n_ref[...], wa_ref[...], preferred_element_type=F32).astype(BF16)

    @pl.when(j >= n_main)
    def _():
        o_ref[...] = jnp.dot(n_ref[...], wg_ref[...], preferred_element_type=F32).astype(BF16)


def _proj(h, norm_w, w_all, w_gates, w_dt, w_dtt, inv_freq, *, batch, seq, pos0):
    rows = h.shape[0]
    tm = min(1024, rows)
    tn = 1024
    assert seq & (seq - 1) == 0 and RET_QK % tn == 0 and OFF_GA % tn == 0 and PROJ_W % tn == 0
    n_main = OFF_GA // tn
    table_len = max(seq, tm)
    cos, sin = _rope_tables(inv_freq, length=table_len, seq=seq, pos0=pos0)
    n_tab = table_len // tm
    batched_dtt = tm <= seq
    if batched_dtt:
        per_seq = seq // tm
        dtt_spec = pl.BlockSpec((1, DT_PAD, tm), lambda i, j: (i // per_seq, 0, i % per_seq))
        dtt_shape = jax.ShapeDtypeStruct((batch, DT_PAD, seq), F32)
    else:
        dtt_spec = pl.BlockSpec((DT_PAD, tm), lambda i, j: (0, i))
        dtt_shape = jax.ShapeDtypeStruct((DT_PAD, rows), F32)
    est = (2 * tm * D_MODEL * 4 + tm * D_MODEL * 2 + 4 * D_MODEL * tn * 2 + 2 * tm * tn * 2
           + 2 * tm * tn * 4 + 4 * tm * LANES * 4 + 4 * D_MODEL * DT_PAD * 2 + 4 * tm * DT_PAD * 4)
    proj, dt, dtt = pl.pallas_call(
        functools.partial(_proj_kernel, tn=tn, batched_dtt=batched_dtt),
        grid=(rows // tm, PROJ_W // tn),
        in_specs=[
            pl.BlockSpec((tm, D_MODEL), lambda i, j: (i, 0)),
            pl.BlockSpec((1, D_MODEL), lambda i, j: (0, 0)),
            pl.BlockSpec((D_MODEL, tn), lambda i, j: (0, jnp.minimum(j, n_main - 1))),
            pl.BlockSpec((D_MODEL, tn), lambda i, j: (0, jnp.maximum(j - n_main, 0))),
            pl.BlockSpec((D_MODEL, DT_PAD), lambda i, j: (0, 0)),
            pl.BlockSpec((DT_PAD, D_MODEL), lambda i, j: (0, 0)),
            pl.BlockSpec((tm, ROPE_HALF), lambda i, j: (i % n_tab, 0)),
            pl.BlockSpec((tm, ROPE_HALF), lambda i, j: (i % n_tab, 0)),
        ],
        out_specs=[
            pl.BlockSpec((tm, tn), lambda i, j: (i, j)),
            pl.BlockSpec((tm, DT_PAD), lambda i, j: (i, 0)),
            dtt_spec,
        ],
        out_shape=[
            jax.ShapeDtypeStruct((rows, PROJ_W), BF16),
            jax.ShapeDtypeStruct((rows, DT_PAD), F32),
            dtt_shape,
        ],
        scratch_shapes=[pltpu.VMEM((tm, D_MODEL), BF16)],
        compiler_params=pltpu.CompilerParams(
            dimension_semantics=("parallel", "arbitrary"), vmem_limit_bytes=_vmem_limit(est)),
        name="proj",
    )(h, norm_w, w_all, w_gates, w_dt, w_dtt, cos, sin)
    if not batched_dtt:
        dtt = dtt.reshape(DT_PAD, batch, seq).transpose(1, 0, 2)
    return proj, dt, dtt


def _ret_kernel(*refs, chunk, has_init):
    if has_init:
        q_ref, k_ref, v_ref, g_ref, nw_ref, s0_ref, o_ref, s_ref, mask_ref = refs
    else:
        q_ref, k_ref, v_ref, g_ref, nw_ref, o_ref, s_ref, mask_ref = refs
    c = pl.program_id(1)
    log_decay = [math.log1p(-(2.0 ** (-5.0 - hd))) for hd in range(RET_HEADS)]

    @pl.when((pl.program_id(0) == 0) & (c == 0))
    def _():
        ii = lax.broadcasted_iota(jnp.int32, (chunk, chunk), 0)
        jj = lax.broadcasted_iota(jnp.int32, (chunk, chunk), 1)
        causal = ii >= jj
        diff = jnp.where(causal, ii - jj, 0).astype(F32)
        for hd in range(RET_HEADS):
            mask_ref[hd] = jnp.where(causal, jnp.exp(diff * log_decay[hd]), 0.0)

    @pl.when(c == 0)
    def _():
        if has_init:
            s_ref[...] = s0_ref[...]
        else:
            s_ref[...] = jnp.zeros_like(s_ref)

    idx = lax.broadcasted_iota(jnp.int32, (chunk, 1), 0).astype(F32)
    for hd in range(RET_HEADS):
        lg = log_decay[hd]
        decay_mask = mask_ref[hd]
        inner_decay = jnp.exp((idx + 1.0) * lg)
        state_decay = jnp.exp((chunk - 1.0 - idx) * lg)
        chunk_decay = math.exp(chunk * lg)
        q = q_ref[0, :, hd * RET_DK:(hd + 1) * RET_DK]
        k = k_ref[0, :, hd * RET_DK:(hd + 1) * RET_DK]
        v = v_ref[0, :, hd * RET_DV:(hd + 1) * RET_DV]
        s = s_ref[0, hd]
        scores = _dot_nt(q, k) * decay_mask
        o = _dot(scores, v) + _dot(q, s) * inner_decay
        s_ref[0, hd] = s * chunk_decay + _dot_tn(k.astype(F32) * state_decay, v)
        mu = jnp.mean(o, axis=-1, keepdims=True)
        var = jnp.mean(jnp.square(o - mu), axis=-1, keepdims=True)
        on = (o - mu) * lax.rsqrt(var + EPS) * nw_ref[:, hd * RET_DV:(hd + 1) * RET_DV]
        gate = g_ref[0, :, hd * RET_DV:(hd + 1) * RET_DV].astype(F32)
        o_ref[0, :, hd * RET_DV:(hd + 1) * RET_DV] = (_silu(gate) * on).astype(BF16)


def _retention(proj3, norm_w, s0, *, chunk):
    b, seq, _ = proj3.shape
    has_init = s0 is not None
    state_block = pl.BlockSpec((1, RET_HEADS, RET_DK, RET_DV), lambda bi, c: (bi, 0, 0, 0))
    in_specs = [
        pl.BlockSpec((1, chunk, RET_QK), lambda bi, c: (bi, c, OFF_Q // RET_QK)),
        pl.BlockSpec((1, chunk, RET_QK), lambda bi, c: (bi, c, OFF_K // RET_QK)),
        pl.BlockSpec((1, chunk, RET_V), lambda bi, c: (bi, c, OFF_V // RET_V)),
        pl.BlockSpec((1, chunk, RET_V), lambda bi, c: (bi, c, OFF_G // RET_V)),
        pl.BlockSpec((1, RET_V), lambda bi, c: (0, 0)),
    ]
    args = [proj3, proj3, proj3, proj3, norm_w]
    if has_init:
        in_specs.append(state_block)
        args.append(s0)
    state_bytes = RET_HEADS * RET_DK * RET_DV * 4
    est = (2 * chunk * (2 * RET_QK + 3 * RET_V) * 2 + (4 if has_init else 2) * state_bytes
           + 8 * chunk * max(chunk, RET_DV) * 4 + RET_HEADS * chunk * chunk * 4)
    return pl.pallas_call(
        functools.partial(_ret_kernel, chunk=chunk, has_init=has_init),
        grid=(b, seq // chunk),
        in_specs=in_specs,
        out_specs=[pl.BlockSpec((1, chunk, RET_V), lambda bi, c: (bi, c, 0)), state_block],
        out_shape=[jax.ShapeDtypeStruct((b, seq, RET_V), BF16),
                   jax.ShapeDtypeStruct((b, RET_HEADS, RET_DK, RET_DV), F32)],
        scratch_shapes=[pltpu.VMEM((RET_HEADS, chunk, chunk), F32)],
        compiler_params=pltpu.CompilerParams(
            dimension_semantics=("arbitrary", "arbitrary"), vmem_limit_bytes=_vmem_limit(est)),
        name="retention",
    )(*args)


def _expand_table():
    e = np.zeros((SSM_GROUPS, 2 * DT_PAD, SSM_GW), np.float32)
    for g in range(SSM_GROUPS):
        for col in range(SSM_GW):
            head = g * SSM_HPG + col // SSM_HEAD_DIM
            for piece in range(3):
                e[g, piece * SSM_HEADS + head, col] = 1.0
    return jnp.asarray(e, BF16)


def _ssd_kernel(*refs, chunk, has_init):
    if has_init:
        (z_ref, x_ref, b_ref, c_ref, dt_ref, dtt_ref, cw_ref, cb_ref, dtb_ref, dtbt_ref, al_ref,
         alt_ref, dsk_ref, nw_ref, ex_ref, h0_ref, cs0_ref, y_ref, hout_ref, csout_ref,
         ht_ref, xc_ref, fix_ref, ext_ref) = refs
    else:
        (z_ref, x_ref, b_ref, c_ref, dt_ref, dtt_ref, cw_ref, cb_ref, dtb_ref, dtbt_ref, al_ref,
         alt_ref, dsk_ref, nw_ref, ex_ref, y_ref, hout_ref, csout_ref,
         ht_ref, xc_ref) = refs
    c = pl.program_id(1)
    taps = SSM_CONV - 1
    kk = CONV_TAIL + chunk

    @pl.when(c == 0)
    def _():
        for g in range(SSM_GROUPS):
            if has_init:
                blk = h0_ref[0, g * SSM_HPG:(g + 1) * SSM_HPG].reshape(SSM_GW, SSM_STATE)
                ht_ref[g] = blk.T
            else:
                ht_ref[g] = jnp.zeros((SSM_STATE, SSM_GW), F32)
        xc_ref[0:CONV_TAIL, :] = jnp.zeros((CONV_TAIL, SSM_XBC), BF16)
        if has_init:
            ext_ref[...] = jnp.zeros_like(ext_ref)
            ext_ref[SUBLANES - taps:SUBLANES, :] = cs0_ref[0]
            fix = jnp.zeros((SUBLANES, SSM_XBC), F32)
            for i in range(taps):
                fix = fix + ext_ref[SUBLANES - taps + i:2 * SUBLANES - taps + i, :] * cw_ref[i:i + 1, :]
            fix_ref[...] = fix

    if has_init:
        @pl.when(c == 1)
        def _():
            fix_ref[...] = jnp.zeros_like(fix_ref)

    xc_ref[CONV_TAIL:kk, 0:SSM_INNER] = x_ref[0]
    xc_ref[CONV_TAIL:kk, SSM_INNER:SSM_INNER + SSM_BC] = b_ref[0]
    xc_ref[CONV_TAIL:kk, SSM_INNER + SSM_BC:SSM_XBC] = c_ref[0]

    si = lax.broadcasted_iota(jnp.int32, (taps * chunk, kk), 0)
    sj = lax.broadcasted_iota(jnp.int32, (taps * chunk, kk), 1)
    tap = si >> (chunk.bit_length() - 1)
    shift = jnp.where(sj == (si & (chunk - 1)) + tap + (CONV_TAIL - taps), 1.0, 0.0).astype(BF16)

    def conv(cur, lo, width):
        prev = jnp.dot(shift, xc_ref[:, lo:lo + width], preferred_element_type=F32)
        acc = cb_ref[:, lo:lo + width] + cur.astype(F32) * cw_ref[taps:taps + 1, lo:lo + width]
        for i in range(taps):
            acc = acc + prev[i * chunk:(i + 1) * chunk] * cw_ref[i:i + 1, lo:lo + width]
        if has_init:
            acc = jnp.concatenate([acc[:SUBLANES] + fix_ref[:, lo:lo + width], acc[SUBLANES:]], axis=0)
        return _silu(acc)

    a_row = -jnp.exp(al_ref[...]) * LOG2E
    a_col = -jnp.exp(alt_ref[...]) * LOG2E
    dt = _softplus(dt_ref[0] + dtb_ref[...])
    dtt = _softplus(dtt_ref[0] + dtbt_ref[...])
    ii = lax.broadcasted_iota(jnp.int32, (chunk, chunk), 0)
    jj = lax.broadcasted_iota(jnp.int32, (chunk, chunk), 1)
    tri = ii >= jj
    lower = tri.astype(BF16)
    upper = (ii <= jj).astype(BF16)
    cum = sum(jnp.dot(lower, p, preferred_element_type=F32) for p in _split3(dt * a_row))
    cumt = sum(jnp.dot(p, upper, preferred_element_type=F32) for p in _split3(dtt * a_col))

    first_half = lax.broadcasted_iota(jnp.int32, (chunk, LANES), 1) < SSM_HEADS

    def pieces(v):
        hi, mid, lo = _split3(v)
        return jnp.concatenate([jnp.where(first_half, hi, mid), lo], axis=1)

    dt_p = pieces(dt)
    cum_p = pieces(cum)
    lane_head = lax.broadcasted_iota(jnp.int32, (chunk, LANES), 1) < SSM_HEAD_DIM

    for g in range(SSM_GROUPS):
        xs = conv(x_ref[0, :, g * SSM_GW:(g + 1) * SSM_GW], g * SSM_GW, SSM_GW)
        bm = conv(b_ref[0, :, g * SSM_STATE:(g + 1) * SSM_STATE], SSM_INNER + g * SSM_STATE, SSM_STATE)
        cm = conv(c_ref[0, :, g * SSM_STATE:(g + 1) * SSM_STATE], SSM_INNER + SSM_BC + g * SSM_STATE, SSM_STATE)
        cbm = jnp.where(tri, _dot_nt(cm, bm), 0.0)
        dt_e = jnp.dot(dt_p, ex_ref[g], preferred_element_type=F32)
        cum_e = jnp.dot(cum_p, ex_ref[g], preferred_element_type=F32)
        e_all = jnp.exp2(cum_e)
        xdt = xs * dt_e
        xd = xdt * jnp.exp2(cum_e[chunk - 1:chunk, :] - cum_e)
        xdt16 = xdt.astype(BF16)
        y_parts = []
        for s in range(SSM_GW // LANES):
            rhs = xdt16[:, s * LANES:(s + 1) * LANES]
            prod = []
            for j in (g * SSM_HPG + 2 * s, g * SSM_HPG + 2 * s + 1):
                seg = (jnp.broadcast_to(cum[:, j:j + 1], (chunk, chunk))
                       - jnp.broadcast_to(cumt[j:j + 1, :], (chunk, chunk)))
                m = cbm * jnp.exp2(jnp.minimum(seg, 0.0))
                prod.append(jnp.dot(m.astype(BF16), rhs, preferred_element_type=F32))
            y_parts.append(jnp.where(lane_head, prod[0], prod[1]))
        htg = ht_ref[g]
        y = (jnp.concatenate(y_parts, axis=1) + _dot(cm, htg) * e_all
             + xs * dsk_ref[:, g * SSM_GW:(g + 1) * SSM_GW])
        ht_ref[g] = htg * e_all[chunk - 1:chunk, :] + _dot_tn(bm, xd)
        y = y * _silu(z_ref[0, :, g * SSM_GW:(g + 1) * SSM_GW].astype(F32))
        y = _rms(y, nw_ref[:, g * SSM_GW:(g + 1) * SSM_GW])
        y_ref[0, :, g * SSM_GW:(g + 1) * SSM_GW] = y.astype(BF16)

    xc_ref[0:CONV_TAIL, :] = xc_ref[chunk:kk, :]

    @pl.when(c == pl.num_programs(1) - 1)
    def _():
        csout_ref[0] = xc_ref[chunk:kk, :].astype(F32)[CONV_TAIL - taps:CONV_TAIL, :]
        for g in range(SSM_GROUPS):
            hout_ref[0, g * SSM_HPG:(g + 1) * SSM_HPG] = ht_ref[g].T.reshape(SSM_HPG, SSM_HEAD_DIM, SSM_STATE)


def _ssd(proj3, dt3, dtt3, conv_w, conv_b, dt_bias, a_log, d_skip, norm_w, h0, cs0, *, chunk):
    b, seq, _ = proj3.shape
    has_init = h0 is not None
    assert chunk % SUBLANES == 0 and chunk >= SUBLANES
    dtb = jnp.tile(dt_bias.astype(F32), 2)
    al = jnp.tile(a_log.astype(F32), 2)
    dsk = jnp.repeat(d_skip.astype(F32), SSM_HEAD_DIM).reshape(1, SSM_INNER)
    const = lambda *shape: pl.BlockSpec(shape, lambda bi, c: (0,) * len(shape))
    in_specs = [
        pl.BlockSpec((1, chunk, SSM_INNER), lambda bi, c: (bi, c, OFF_Z // SSM_INNER)),
        pl.BlockSpec((1, chunk, SSM_INNER), lambda bi, c: (bi, c, OFF_X // SSM_INNER)),
        pl.BlockSpec((1, chunk, SSM_BC), lambda bi, c: (bi, c, OFF_B // SSM_BC)),
        pl.BlockSpec((1, chunk, SSM_BC), lambda bi, c: (bi, c, OFF_C // SSM_BC)),
        pl.BlockSpec((1, chunk, DT_PAD), lambda bi, c: (bi, c, 0)),
        pl.BlockSpec((1, DT_PAD, chunk), lambda bi, c: (bi, 0, c)),
        const(SSM_CONV, SSM_XBC), const(1, SSM_XBC), const(1, DT_PAD), const(DT_PAD, 1),
        const(1, DT_PAD), const(DT_PAD, 1), const(1, SSM_INNER), const(1, SSM_INNER),
        const(SSM_GROUPS, 2 * DT_PAD, SSM_GW),
    ]
    args = [proj3, proj3, proj3, proj3, dt3, dtt3, conv_w.astype(F32), conv_b.astype(F32).reshape(1, SSM_XBC),
            dtb.reshape(1, DT_PAD), dtb.reshape(DT_PAD, 1), al.reshape(1, DT_PAD), al.reshape(DT_PAD, 1),
            dsk, norm_w.astype(F32).reshape(1, SSM_INNER), _expand_table()]
    h_block = pl.BlockSpec((1, SSM_HEADS, SSM_HEAD_DIM, SSM_STATE), lambda bi, c: (bi, 0, 0, 0))
    cs_block = pl.BlockSpec((1, SSM_CONV - 1, SSM_XBC), lambda bi, c: (bi, 0, 0))
    scratch = [pltpu.VMEM((SSM_GROUPS, SSM_STATE, SSM_GW), F32),
               pltpu.VMEM((CONV_TAIL + chunk, SSM_XBC), BF16)]
    if has_init:
        in_specs += [h_block, cs_block]
        args += [h0, cs0]
        scratch += [pltpu.VMEM((SUBLANES, SSM_XBC), F32), pltpu.VMEM((2 * SUBLANES, SSM_XBC), F32)]
    state_bytes = SSM_HEADS * SSM_HEAD_DIM * SSM_STATE * 4
    est = (2 * chunk * (3 * SSM_INNER + 2 * SSM_BC) * 2 + (5 if has_init else 3) * state_bytes
           + (chunk + CONV_TAIL) * SSM_XBC * 2 + 32 * chunk * max(chunk, SSM_GW) * 4 + 8 * 1024 * 1024)
    return pl.pallas_call(
        functools.partial(_ssd_kernel, chunk=chunk, has_init=has_init),
        grid=(b, seq // chunk),
        in_specs=in_specs,
        out_specs=[pl.BlockSpec((1, chunk, SSM_INNER), lambda bi, c: (bi, c, 0)), h_block, cs_block],
        out_shape=[jax.ShapeDtypeStruct((b, seq, SSM_INNER), BF16),
                   jax.ShapeDtypeStruct((b, SSM_HEADS, SSM_HEAD_DIM, SSM_STATE), F32),
                   jax.ShapeDtypeStruct((b, SSM_CONV - 1, SSM_XBC), F32)],
        scratch_shapes=scratch,
        compiler_params=pltpu.CompilerParams(
            dimension_semantics=("parallel", "arbitrary"), vmem_limit_bytes=_vmem_limit(est)),
        name="ssd",
    )(*args)


def _merge_kernel(ar_ref, as_ref, wr_ref, ws_ref, ga_ref, gb_ref, ba_ref, bb_ref, o_ref):
    yr = jnp.dot(ar_ref[...], wr_ref[...], preferred_element_type=F32)
    ys = jnp.dot(as_ref[...], ws_ref[...], preferred_element_type=F32)
    ga = jax.nn.sigmoid(ga_ref[...].astype(F32) + ba_ref[...])
    gb = jax.nn.sigmoid(gb_ref[...].astype(F32) + bb_ref[...])
    o_ref[...] = (ga * yr + gb * ys).astype(BF16)


def _merge(a_ret, a_ssm, w_ret, w_ssm, proj, b_gate):
    rows = a_ret.shape[0]
    tm = min(1024, rows)
    tn = 256
    nb = D_MODEL // tn
    est = 2 * 2 * tm * RET_V * 2 + 2 * 2 * RET_V * tn * 2 + 6 * tm * tn * 4
    return pl.pallas_call(
        _merge_kernel,
        grid=(rows // tm, nb),
        in_specs=[
            pl.BlockSpec((tm, RET_V), lambda i, j: (i, 0)),
            pl.BlockSpec((tm, SSM_INNER), lambda i, j: (i, 0)),
            pl.BlockSpec((RET_V, tn), lambda i, j: (0, j)),
            pl.BlockSpec((SSM_INNER, tn), lambda i, j: (0, j)),
            pl.BlockSpec((tm, tn), lambda i, j: (i, OFF_GA // tn + j)),
            pl.BlockSpec((tm, tn), lambda i, j: (i, OFF_GB // tn + j)),
            pl.BlockSpec((1, tn), lambda i, j: (0, j)),
            pl.BlockSpec((1, tn), lambda i, j: (0, nb + j)),
        ],
        out_specs=pl.BlockSpec((tm, tn), lambda i, j: (i, j)),
        out_shape=jax.ShapeDtypeStruct((rows, D_MODEL), BF16),
        compiler_params=pltpu.CompilerParams(
            dimension_semantics=("parallel", "arbitrary"), vmem_limit_bytes=_vmem_limit(est)),
        name="merge",
    )(a_ret, a_ssm, w_ret, w_ssm, proj, proj, b_gate, b_gate)


def _outproj_kernel(m_ref, w_ref, h_ref, o_ref):
    o_ref[...] = h_ref[...] + jnp.dot(m_ref[...], w_ref[...], preferred_element_type=F32)


def _outproj(m, w_out, h):
    rows = m.shape[0]
    tm = min(1024, rows)
    tn = 512
    est = 2 * tm * D_MODEL * 2 + 2 * D_MODEL * tn * 2 + 5 * tm * tn * 4
    return pl.pallas_call(
        _outproj_kernel,
        grid=(rows // tm, D_MODEL // tn),
        in_specs=[
            pl.BlockSpec((tm, D_MODEL), lambda i, j: (i, 0)),
            pl.BlockSpec((D_MODEL, tn), lambda i, j: (0, j)),
            pl.BlockSpec((tm, tn), lambda i, j: (i, j)),
        ],
        out_specs=pl.BlockSpec((tm, tn), lambda i, j: (i, j)),
        out_shape=jax.ShapeDtypeStruct((rows, D_MODEL), F32),
        compiler_params=pltpu.CompilerParams(
            dimension_semantics=("parallel", "arbitrary"), vmem_limit_bytes=_vmem_limit(est)),
        name="outproj",
    )(m, w_out, h)


def _layer(x, s_ret, s_ssm, s_conv, pos0, p, *, ret_chunk, ssd_chunk):
    b, seq, _ = x.shape
    rows = b * seq
    h = _ffn(x.reshape(rows, D_MODEL), p["norm_ffn1"], p["ffn1_w_gate"], p["ffn1_w_up"], p["ffn1_w_down"],
             p["norm_final"], final_norm=False)
    proj, dt, dtt3 = _proj(h, p["norm_mix"], p["w_all"], p["w_gates"], p["w_dt"], p["w_dtt"], p["inv_freq"],
                           batch=b, seq=seq, pos0=pos0)
    proj3 = proj.reshape(b, seq, PROJ_W)
    dt3 = dt.reshape(b, seq, DT_PAD)
    a_ret, ret_new = _retention(proj3, p["ret_norm_w"], s_ret, chunk=min(ret_chunk, seq))
    a_ssm, ssm_new, conv_new = _ssd(proj3, dt3, dtt3, p["conv_w"], p["conv_b"], p["dt_bias"], p["a_log"],
                                    p["d_skip"], p["ssm_norm_w"], s_ssm, s_conv, chunk=min(ssd_chunk, seq))
    m = _merge(a_ret.reshape(rows, RET_V), a_ssm.reshape(rows, SSM_INNER), p["w_out_ret"], p["w_out_ssm"],
               proj, p["b_gate"])
    h2 = _outproj(m, p["w_out"], h)
    y = _ffn(h2, p["norm_ffn2"], p["ffn2_w_gate"], p["ffn2_w_up"], p["ffn2_w_down"], p["norm_final"],
             final_norm=True)
    return y.reshape(b, seq, D_MODEL), ret_new, ssm_new, conv_new


def kernel(x_prompt, x_sample, state_ret, state_ssm, state_conv, norm_ffn1, ffn1_w_gate, ffn1_w_up, ffn1_w_down, norm_mix, w_in, b_gate, ret_norm_w, w_out_ret, conv_w, conv_b, dt_bias, a_log, d_skip, ssm_norm_w, w_out_ssm, w_out, norm_ffn2, ffn2_w_gate, ffn2_w_up, ffn2_w_down, norm_final):
    row = lambda v: v.astype(F32).reshape(1, -1)
    dt_lo = 2 * RET_QK + 2 * RET_V + SSM_INNER + SSM_XBC
    w_dt = jnp.tile(w_in[:, dt_lo:dt_lo + SSM_HEADS], (1, 2)).astype(BF16)
    p = dict(
        norm_ffn1=row(norm_ffn1), norm_mix=row(norm_mix), norm_ffn2=row(norm_ffn2), norm_final=row(norm_final),
        ffn1_w_gate=ffn1_w_gate.astype(BF16), ffn1_w_up=ffn1_w_up.astype(BF16), ffn1_w_down=ffn1_w_down.astype(BF16),
        ffn2_w_gate=ffn2_w_gate.astype(BF16), ffn2_w_up=ffn2_w_up.astype(BF16), ffn2_w_down=ffn2_w_down.astype(BF16),
        w_all=w_in.astype(BF16), w_gates=w_in[:, dt_lo + SSM_HEADS:].astype(BF16),
        w_dt=w_dt, w_dtt=w_dt.T,
        inv_freq=(ROPE_BASE ** (-jnp.arange(ROPE_HALF, dtype=F32) / ROPE_HALF)).reshape(1, ROPE_HALF),
        b_gate=row(b_gate), ret_norm_w=row(ret_norm_w),
        w_out_ret=w_out_ret.astype(BF16), w_out_ssm=w_out_ssm.astype(BF16), w_out=w_out.astype(BF16),
        conv_w=conv_w, conv_b=conv_b, dt_bias=dt_bias, a_log=a_log, d_skip=d_skip, ssm_norm_w=ssm_norm_w,
    )
    y_p, ret_p, ssm_p, conv_p = _layer(x_prompt, None, None, None, 0, p, ret_chunk=256, ssd_chunk=128)
    y_s, ret_s, ssm_s, conv_s = _layer(x_sample, state_ret.astype(F32), state_ssm.astype(F32),
                                       state_conv.astype(F32), PAST_LEN, p, ret_chunk=256, ssd_chunk=128)
    return (y_p, y_s, ret_p, ssm_p, conv_p, ret_s, ssm_s, conv_s)
```

```python
import functools
import math

import numpy as np
import jax
import jax.numpy as jnp
from jax import lax
from jax.experimental import pallas as pl
from jax.experimental.pallas import tpu as pltpu

F32 = jnp.float32
BF16 = jnp.bfloat16

D_MODEL = 2048
PAST_LEN = 1024
RET_HEADS = 8
RET_DK = D_MODEL // RET_HEADS
RET_DV = 2 * D_MODEL // RET_HEADS
RET_QK = RET_HEADS * RET_DK
RET_V = RET_HEADS * RET_DV
ROPE_BASE = 10000.0
ROPE_HALF = RET_DK // 2
SSM_INNER = 2 * D_MODEL
SSM_HEAD_DIM = 64
SSM_HEADS = SSM_INNER // SSM_HEAD_DIM
SSM_GROUPS = 8
SSM_HPG = SSM_HEADS // SSM_GROUPS
SSM_STATE = 128
SSM_CONV = 4
SSM_BC = SSM_GROUPS * SSM_STATE
SSM_XBC = SSM_INNER + 2 * SSM_BC
SSM_GW = SSM_INNER // SSM_GROUPS
D_FF = 5632
EPS = 1e-6
LOG2E = math.log2(math.e)

LANES = 128
SUBLANES = 8
BF16_SUBLANES = 16
V7X_VMEM_BYTES = 64 * 1024 * 1024

OFF_Q = 0
OFF_K = OFF_Q + RET_QK
OFF_V = OFF_K + RET_QK
OFF_G = OFF_V + RET_V
OFF_Z = OFF_G + RET_V
OFF_X = OFF_Z + SSM_INNER
OFF_B = OFF_X + SSM_INNER
OFF_C = OFF_B + SSM_BC
OFF_GA = OFF_C + SSM_BC
OFF_GB = OFF_GA + D_MODEL
PROJ_W = OFF_GB + D_MODEL
DT_PAD = LANES
assert DT_PAD == 2 * SSM_HEADS
CONV_TAIL = BF16_SUBLANES
FFN_TF = 512
MERGE_TN = 256


def _vmem_limit(nbytes):
    return int(min(V7X_VMEM_BYTES - 4 * 1024 * 1024, max(nbytes, 16 * 1024 * 1024)))


def _dot(a, b):
    return jnp.dot(a.astype(BF16), b.astype(BF16), preferred_element_type=F32)


def _dot_nt(a, b):
    return lax.dot_general(a.astype(BF16), b.astype(BF16), (((1,), (1,)), ((), ())),
                           preferred_element_type=F32)


def _dot_tn(a, b):
    t = a.shape[0]
    pad = (-t) % LANES
    a = a.astype(F32)
    b = b.astype(BF16)
    if pad:
        a = jnp.concatenate([a, jnp.zeros((pad, a.shape[1]), F32)], axis=0)
        b = jnp.concatenate([b, jnp.zeros((pad, b.shape[1]), BF16)], axis=0)
    return jnp.dot(a.T.astype(BF16), b, preferred_element_type=F32)


def _rms(x, w):
    return x * lax.rsqrt(jnp.mean(x * x, axis=-1, keepdims=True) + EPS) * w


def _silu(x):
    return x * jax.nn.sigmoid(x)


def _softplus(x):
    return jnp.maximum(x, 0.0) + jnp.log1p(jnp.exp(-jnp.abs(x)))


def _split3(x):
    hi = x.astype(BF16)
    r = x - hi.astype(F32)
    mid = r.astype(BF16)
    lo = (r - mid.astype(F32)).astype(BF16)
    return hi, mid, lo


def _ffn_kernel(x_ref, nw_ref, wg_ref, wu_ref, wd_ref, fw_ref, o_ref, n_ref, acc_ref, *, final_norm, sub):
    j = pl.program_id(1)

    @pl.when(j == 0)
    def _():
        n_ref[...] = _rms(x_ref[...], nw_ref[...]).astype(BF16)
        acc_ref[...] = jnp.zeros_like(acc_ref)

    n = n_ref[...]
    tf = wg_ref.shape[2]
    acc = acc_ref[...]
    for lo in range(0, tf, sub):
        g = jnp.dot(n, wg_ref[0, :, lo:lo + sub], preferred_element_type=F32)
        u = jnp.dot(n, wu_ref[0, :, lo:lo + sub], preferred_element_type=F32)
        a = (_silu(g) * u).astype(BF16)
        acc = acc + jnp.dot(a, wd_ref[lo:lo + sub, :], preferred_element_type=F32)
    acc_ref[...] = acc

    @pl.when(j == pl.num_programs(1) - 1)
    def _():
        h = x_ref[...] + 0.5 * acc_ref[...]
        if final_norm:
            h = _rms(h, fw_ref[...])
        o_ref[...] = h


def _ffn(x, norm_w, wg, wu, wd, final_w, *, final_norm):
    rows = x.shape[0]
    tm = min(512, rows)
    tf = FFN_TF
    est = (2 * 2 * tm * D_MODEL * 4 + tm * D_MODEL * (4 + 2) + 2 * 3 * D_MODEL * tf * 2
           + 3 * tm * tf * 4 + 3 * tm * D_MODEL * 4)
    return pl.pallas_call(
        functools.partial(_ffn_kernel, final_norm=final_norm, sub=256),
        grid=(rows // tm, D_FF // tf),
        in_specs=[
            pl.BlockSpec((tm, D_MODEL), lambda i, j: (i, 0)),
            pl.BlockSpec((1, D_MODEL), lambda i, j: (0, 0)),
            pl.BlockSpec((1, D_MODEL, tf), lambda i, j: (j, 0, 0)),
            pl.BlockSpec((1, D_MODEL, tf), lambda i, j: (j, 0, 0)),
            pl.BlockSpec((tf, D_MODEL), lambda i, j: (j, 0)),
            pl.BlockSpec((1, D_MODEL), lambda i, j: (0, 0)),
        ],
        out_specs=pl.BlockSpec((tm, D_MODEL), lambda i, j: (i, 0)),
        out_shape=jax.ShapeDtypeStruct((rows, D_MODEL), F32),
        scratch_shapes=[pltpu.VMEM((tm, D_MODEL), BF16), pltpu.VMEM((tm, D_MODEL), F32)],
        compiler_params=pltpu.CompilerParams(
            dimension_semantics=("parallel", "arbitrary"), vmem_limit_bytes=_vmem_limit(est)),
        name="ffn_final" if final_norm else "ffn",
    )(x, norm_w, wg, wu, wd, final_w)


def _rope_kernel(inv_ref, cos_ref, sin_ref, *, seq, pos0):
    tr = cos_ref.shape[0]
    row = pl.program_id(0) * tr + lax.broadcasted_iota(jnp.int32, (tr, 1), 0)
    ang = ((row & (seq - 1)) + pos0).astype(F32) * inv_ref[...]
    cos_ref[...] = jnp.cos(ang)
    sin_ref[...] = jnp.sin(ang)


def _rope_tables(inv_freq, *, length, seq, pos0):
    tr = min(512, length)
    spec = pl.BlockSpec((tr, ROPE_HALF), lambda i: (i, 0))
    return pl.pallas_call(
        functools.partial(_rope_kernel, seq=seq, pos0=pos0),
        grid=(length // tr,),
        in_specs=[pl.BlockSpec((1, ROPE_HALF), lambda i: (0, 0))],
        out_specs=[spec, spec],
        out_shape=[jax.ShapeDtypeStruct((length, ROPE_HALF), F32)] * 2,
        name="rope",
    )(inv_freq)


def _proj_kernel(h_ref, nw_ref, wa_ref, wg_ref, wdt_ref, wdtt_ref, cos_ref, sin_ref, o_ref, dt_ref, dtt_ref,
                 n_ref, *, tn, batched_dtt):
    j = pl.program_id(1)
    n_q = RET_QK // tn
    n_main = OFF_GA // tn

    @pl.when(j == 0)
    def _():
        n = _rms(h_ref[...], nw_ref[...]).astype(BF16)
        n_ref[...] = n
        dt_ref[...] = jnp.dot(n, wdt_ref[...], preferred_element_type=F32)
        dtt = _dot_nt(wdtt_ref[...], n)
        if batched_dtt:
            dtt_ref[0] = dtt
        else:
            dtt_ref[...] = dtt

    @pl.when(j < 2 * n_q)
    def _():
        acc = jnp.dot(n_ref[...], wa_ref[...], preferred_element_type=F32)
        cos = cos_ref[...]
        sin = sin_ref[...]
        scale = jnp.where(j < n_q, 1.0, RET_DK ** -0.5).astype(F32)
        for lo in range(0, tn, RET_DK):
            x1 = acc[:, lo:lo + ROPE_HALF]
            x2 = acc[:, lo + ROPE_HALF:lo + RET_DK]
            o_ref[:, lo:lo + ROPE_HALF] = ((x1 * cos - x2 * sin) * scale).astype(BF16)
            o_ref[:, lo + ROPE_HALF:lo + RET_DK] = ((x1 * sin + x2 * cos) * scale).astype(BF16)

    @pl.when((j >= 2 * n_q) & (j < n_main))
    def _():
        o_ref[...] = jnp.dot(n_ref[...], wa_ref[...], preferred_element_type=F32).astype(BF16)

    @pl.when(j >= n_main)
    def _():
        o_ref[...] = jnp.dot(n_ref[...], wg_ref[...], preferred_element_type=F32).astype(BF16)


def _proj(h, norm_w, w_all, w_gates, w_dt, w_dtt, inv_freq, *, batch, seq, pos0):
    rows = h.shape[0]
    tm = min(1024, rows)
    tn = 1024
    assert seq & (seq - 1) == 0 and RET_QK % tn == 0 and OFF_GA % tn == 0 and PROJ_W % tn == 0
    n_main = OFF_GA // tn
    table_len = max(seq, tm)
    cos, sin = _rope_tables(inv_freq, length=table_len, seq=seq, pos0=pos0)
    n_tab = table_len // tm
    batched_dtt = tm <= seq
    if batched_dtt:
        per_seq = seq // tm
        dtt_spec = pl.BlockSpec((1, DT_PAD, tm), lambda i, j: (i // per_seq, 0, i % per_seq))
        dtt_shape = jax.ShapeDtypeStruct((batch, DT_PAD, seq), F32)
    else:
        dtt_spec = pl.BlockSpec((DT_PAD, tm), lambda i, j: (0, i))
        dtt_shape = jax.ShapeDtypeStruct((DT_PAD, rows), F32)
    est = (2 * tm * D_MODEL * 4 + tm * D_MODEL * 2 + 4 * D_MODEL * tn * 2 + 2 * tm * tn * 2
           + 2 * tm * tn * 4 + 4 * tm * LANES * 4 + 4 * D_MODEL * DT_PAD * 2 + 4 * tm * DT_PAD * 4)
    proj, dt, dtt = pl.pallas_call(
        functools.partial(_proj_kernel, tn=tn, batched_dtt=batched_dtt),
        grid=(rows // tm, PROJ_W // tn),
        in_specs=[
            pl.BlockSpec((tm, D_MODEL), lambda i, j: (i, 0)),
            pl.BlockSpec((1, D_MODEL), lambda i, j: (0, 0)),
            pl.BlockSpec((D_MODEL, tn), lambda i, j: (0, jnp.minimum(j, n_main - 1))),
            pl.BlockSpec((D_MODEL, tn), lambda i, j: (0, jnp.maximum(j - n_main, 0))),
            pl.BlockSpec((D_MODEL, DT_PAD), lambda i, j: (0, 0)),
            pl.BlockSpec((DT_PAD, D_MODEL), lambda i, j: (0, 0)),
            pl.BlockSpec((tm, ROPE_HALF), lambda i, j: (i % n_tab, 0)),
            pl.BlockSpec((tm, ROPE_HALF), lambda i, j: (i % n_tab, 0)),
        ],
        out_specs=[
            pl.BlockSpec((tm, tn), lambda i, j: (i, j)),
            pl.BlockSpec((tm, DT_PAD), lambda i, j: (i, 0)),
            dtt_spec,
        ],
        out_shape=[
            jax.ShapeDtypeStruct((rows, PROJ_W), BF16),
            jax.ShapeDtypeStruct((rows, DT_PAD), F32),
            dtt_shape,
        ],
        scratch_shapes=[pltpu.VMEM((tm, D_MODEL), BF16)],
        compiler_params=pltpu.CompilerParams(
            dimension_semantics=("parallel", "arbitrary"), vmem_limit_bytes=_vmem_limit(est)),
        name="proj",
    )(h, norm_w, w_all, w_gates, w_dt, w_dtt, cos, sin)
    if not batched_dtt:
        dtt = dtt.reshape(DT_PAD, batch, seq).transpose(1, 0, 2)
    return proj, dt, dtt


def _ret_kernel(*refs, chunk, has_init):
    if has_init:
        q_ref, k_ref, v_ref, g_ref, nw_ref, s0_ref, o_ref, s_ref, mask_ref = refs
    else:
        q_ref, k_ref, v_ref, g_ref, nw_ref, o_ref, s_ref, mask_ref = refs
    c = pl.program_id(1)
    log_decay = [math.log1p(-(2.0 ** (-5.0 - hd))) for hd in range(RET_HEADS)]

    @pl.when((pl.program_id(0) == 0) & (c == 0))
    def _():
        ii = lax.broadcasted_iota(jnp.int32, (chunk, chunk), 0)
        jj = lax.broadcasted_iota(jnp.int32, (chunk, chunk), 1)
        causal = ii >= jj
        diff = jnp.where(causal, ii - jj, 0).astype(F32)
        for hd in range(RET_HEADS):
            mask_ref[hd] = jnp.where(causal, jnp.exp(diff * log_decay[hd]), 0.0)

    @pl.when(c == 0)
    def _():
        if has_init:
            s_ref[...] = s0_ref[...]
        else:
            s_ref[...] = jnp.zeros_like(s_ref)

    idx = lax.broadcasted_iota(jnp.int32, (chunk, 1), 0).astype(F32)
    for hd in range(RET_HEADS):
        lg = log_decay[hd]
        decay_mask = mask_ref[hd]
        inner_decay = jnp.exp((idx + 1.0) * lg)
        state_decay = jnp.exp((chunk - 1.0 - idx) * lg)
        chunk_decay = math.exp(chunk * lg)
        q = q_ref[0, :, hd * RET_DK:(hd + 1) * RET_DK]
        k = k_ref[0, :, hd * RET_DK:(hd + 1) * RET_DK]
        v = v_ref[0, :, hd * RET_DV:(hd + 1) * RET_DV]
        s = s_ref[0, hd]
        scores = _dot_nt(q, k) * decay_mask
        o = _dot(scores, v) + _dot(q, s) * inner_decay
        s_ref[0, hd] = s * chunk_decay + _dot_tn(k.astype(F32) * state_decay, v)
        mu = jnp.mean(o, axis=-1, keepdims=True)
        var = jnp.mean(jnp.square(o - mu), axis=-1, keepdims=True)
        on = (o - mu) * lax.rsqrt(var + EPS) * nw_ref[:, hd * RET_DV:(hd + 1) * RET_DV]
        gate = g_ref[0, :, hd * RET_DV:(hd + 1) * RET_DV].astype(F32)
        o_ref[0, :, hd * RET_DV:(hd + 1) * RET_DV] = (_silu(gate) * on).astype(BF16)


def _retention(proj3, norm_w, s0, *, chunk):
    b, seq, _ = proj3.shape
    has_init = s0 is not None
    state_block = pl.BlockSpec((1, RET_HEADS, RET_DK, RET_DV), lambda bi, c: (bi, 0, 0, 0))
    in_specs = [
        pl.BlockSpec((1, chunk, RET_QK), lambda bi, c: (bi, c, OFF_Q // RET_QK)),
        pl.BlockSpec((1, chunk, RET_QK), lambda bi, c: (bi, c, OFF_K // RET_QK)),
        pl.BlockSpec((1, chunk, RET_V), lambda bi, c: (bi, c, OFF_V // RET_V)),
        pl.BlockSpec((1, chunk, RET_V), lambda bi, c: (bi, c, OFF_G // RET_V)),
        pl.BlockSpec((1, RET_V), lambda bi, c: (0, 0)),
    ]
    args = [proj3, proj3, proj3, proj3, norm_w]
    if has_init:
        in_specs.append(state_block)
        args.append(s0)
    state_bytes = RET_HEADS * RET_DK * RET_DV * 4
    est = (2 * chunk * (2 * RET_QK + 3 * RET_V) * 2 + (4 if has_init else 2) * state_bytes
           + 8 * chunk * max(chunk, RET_DV) * 4 + RET_HEADS * chunk * chunk * 4)
    return pl.pallas_call(
        functools.partial(_ret_kernel, chunk=chunk, has_init=has_init),
        grid=(b, seq // chunk),
        in_specs=in_specs,
        out_specs=[pl.BlockSpec((1, chunk, RET_V), lambda bi, c: (bi, c, 0)), state_block],
        out_shape=[jax.ShapeDtypeStruct((b, seq, RET_V), BF16),
                   jax.ShapeDtypeStruct((b, RET_HEADS, RET_DK, RET_DV), F32)],
        scratch_shapes=[pltpu.VMEM((RET_HEADS, chunk, chunk), F32)],
        compiler_params=pltpu.CompilerParams(
            dimension_semantics=("arbitrary", "arbitrary"), vmem_limit_bytes=_vmem_limit(est)),
        name="retention",
    )(*args)


def _expand_table():
    e = np.zeros((SSM_GROUPS, 2 * DT_PAD, SSM_GW), np.float32)
    for g in range(SSM_GROUPS):
        for col in range(SSM_GW):
            head = g * SSM_HPG + col // SSM_HEAD_DIM
            for piece in range(3):
                e[g, piece * SSM_HEADS + head, col] = 1.0
    return jnp.asarray(e, BF16)


def _ssd_kernel(*refs, chunk, has_init):
    if has_init:
        (z_ref, x_ref, b_ref, c_ref, dt_ref, dtt_ref, cw_ref, cb_ref, dtb_ref, dtbt_ref, al_ref,
         alt_ref, dsk_ref, nw_ref, ex_ref, h0_ref, cs0_ref, y_ref, hout_ref, csout_ref,
         ht_ref, xc_ref, fix_ref, ext_ref) = refs
    else:
        (z_ref, x_ref, b_ref, c_ref, dt_ref, dtt_ref, cw_ref, cb_ref, dtb_ref, dtbt_ref, al_ref,
         alt_ref, dsk_ref, nw_ref, ex_ref, y_ref, hout_ref, csout_ref,
         ht_ref, xc_ref) = refs
    c = pl.program_id(1)
    taps = SSM_CONV - 1
    kk = CONV_TAIL + chunk

    @pl.when(c == 0)
    def _():
        for g in range(SSM_GROUPS):
            if has_init:
                blk = h0_ref[0, g * SSM_HPG:(g + 1) * SSM_HPG].reshape(SSM_GW, SSM_STATE)
                ht_ref[g] = blk.T
            else:
                ht_ref[g] = jnp.zeros((SSM_STATE, SSM_GW), F32)
        xc_ref[0:CONV_TAIL, :] = jnp.zeros((CONV_TAIL, SSM_XBC), BF16)
        if has_init:
            ext_ref[...] = jnp.zeros_like(ext_ref)
            ext_ref[SUBLANES - taps:SUBLANES, :] = cs0_ref[0]
            fix = jnp.zeros((SUBLANES, SSM_XBC), F32)
            for i in range(taps):
                fix = fix + ext_ref[SUBLANES - taps + i:2 * SUBLANES - taps + i, :] * cw_ref[i:i + 1, :]
            fix_ref[...] = fix

    if has_init:
        @pl.when(c == 1)
        def _():
            fix_ref[...] = jnp.zeros_like(fix_ref)

    xc_ref[CONV_TAIL:kk, 0:SSM_INNER] = x_ref[0]
    xc_ref[CONV_TAIL:kk, SSM_INNER:SSM_INNER + SSM_BC] = b_ref[0]
    xc_ref[CONV_TAIL:kk, SSM_INNER + SSM_BC:SSM_XBC] = c_ref[0]

    si = lax.broadcasted_iota(jnp.int32, (taps * chunk, kk), 0)
    sj = lax.broadcasted_iota(jnp.int32, (taps * chunk, kk), 1)
    tap = si >> (chunk.bit_length() - 1)
    shift = jnp.where(sj == (si & (chunk - 1)) + tap + (CONV_TAIL - taps), 1.0, 0.0).astype(BF16)

    def conv(cur, lo, width):
        prev = jnp.dot(shift, xc_ref[:, lo:lo + width], preferred_element_type=F32)
        acc = cb_ref[:, lo:lo + width] + cur.astype(F32) * cw_ref[taps:taps + 1, lo:lo + width]
        for i in range(taps):
            acc = acc + prev[i * chunk:(i + 1) * chunk] * cw_ref[i:i + 1, lo:lo + width]
        if has_init:
            acc = jnp.concatenate([acc[:SUBLANES] + fix_ref[:, lo:lo + width], acc[SUBLANES:]], axis=0)
        return _silu(acc)

    a_row = -jnp.exp(al_ref[...]) * LOG2E
    a_col = -jnp.exp(alt_ref[...]) * LOG2E
    dt = _softplus(dt_ref[0] + dtb_ref[...])
    dtt = _softplus(dtt_ref[0] + dtbt_ref[...])
    ii = lax.broadcasted_iota(jnp.int32, (chunk, chunk), 0)
    jj = lax.broadcasted_iota(jnp.int32, (chunk, chunk), 1)
    tri = ii >= jj
    lower = tri.astype(BF16)
    upper = (ii <= jj).astype(BF16)
    cum = sum(jnp.dot(lower, p, preferred_element_type=F32) for p in _split3(dt * a_row))
    cumt = sum(jnp.dot(p, upper, preferred_element_type=F32) for p in _split3(dtt * a_col))

    first_half = lax.broadcasted_iota(jnp.int32, (chunk, LANES), 1) < SSM_HEADS

    def pieces(v):
        hi, mid, lo = _split3(v)
        return jnp.concatenate([jnp.where(first_half, hi, mid), lo], axis=1)

    dt_p = pieces(dt)
    cum_p = pieces(cum)
    lane_head = lax.broadcasted_iota(jnp.int32, (chunk, LANES), 1) < SSM_HEAD_DIM

    def stage_a(g):
        xs = conv(x_ref[0, :, g * SSM_GW:(g + 1) * SSM_GW], g * SSM_GW, SSM_GW)
        bm = conv(b_ref[0, :, g * SSM_STATE:(g + 1) * SSM_STATE], SSM_INNER + g * SSM_STATE, SSM_STATE)
        cm = conv(c_ref[0, :, g * SSM_STATE:(g + 1) * SSM_STATE], SSM_INNER + SSM_BC + g * SSM_STATE, SSM_STATE)
        cbm = jnp.where(tri, _dot_nt(cm, bm), 0.0)
        dt_e = jnp.dot(dt_p, ex_ref[g], preferred_element_type=F32)
        cum_e = jnp.dot(cum_p, ex_ref[g], preferred_element_type=F32)
        e_all = jnp.exp2(cum_e)
        xdt = xs * dt_e
        xd = xdt * jnp.exp2(cum_e[chunk - 1:chunk, :] - cum_e)
        return xs, bm, cm, cbm, e_all, xd, xdt.astype(BF16)

    nxt = stage_a(0)
    for g in range(SSM_GROUPS):
        xs, bm, cm, cbm, e_all, xd, xdt16 = nxt
        if g + 1 < SSM_GROUPS:
            nxt = stage_a(g + 1)
        y_parts = []
        for s in range(SSM_GW // LANES):
            rhs = xdt16[:, s * LANES:(s + 1) * LANES]
            prod = []
            for j in (g * SSM_HPG + 2 * s, g * SSM_HPG + 2 * s + 1):
                seg = (jnp.broadcast_to(cum[:, j:j + 1], (chunk, chunk))
                       - jnp.broadcast_to(cumt[j:j + 1, :], (chunk, chunk)))
                m = cbm * jnp.exp2(jnp.minimum(seg, 0.0))
                prod.append(jnp.dot(m.astype(BF16), rhs, preferred_element_type=F32))
            y_parts.append(jnp.where(lane_head, prod[0], prod[1]))
        htg = ht_ref[g]
        y = (jnp.concatenate(y_parts, axis=1) + _dot(cm, htg) * e_all
             + xs * dsk_ref[:, g * SSM_GW:(g + 1) * SSM_GW])
        ht_ref[g] = htg * e_all[chunk - 1:chunk, :] + _dot_tn(bm, xd)
        y = y * _silu(z_ref[0, :, g * SSM_GW:(g + 1) * SSM_GW].astype(F32))
        y = _rms(y, nw_ref[:, g * SSM_GW:(g + 1) * SSM_GW])
        y_ref[0, :, g * SSM_GW:(g + 1) * SSM_GW] = y.astype(BF16)

    xc_ref[0:CONV_TAIL, :] = xc_ref[chunk:kk, :]

    @pl.when(c == pl.num_programs(1) - 1)
    def _():
        csout_ref[0] = xc_ref[chunk:kk, :].astype(F32)[CONV_TAIL - taps:CONV_TAIL, :]
        for g in range(SSM_GROUPS):
            hout_ref[0, g * SSM_HPG:(g + 1) * SSM_HPG] = ht_ref[g].T.reshape(SSM_HPG, SSM_HEAD_DIM, SSM_STATE)


def _ssd(proj3, dt3, dtt3, conv_w, conv_b, dt_bias, a_log, d_skip, norm_w, h0, cs0, *, chunk):
    b, seq, _ = proj3.shape
    has_init = h0 is not None
    assert chunk % SUBLANES == 0 and chunk >= SUBLANES
    dtb = jnp.tile(dt_bias.astype(F32), 2)
    al = jnp.tile(a_log.astype(F32), 2)
    dsk = jnp.repeat(d_skip.astype(F32), SSM_HEAD_DIM).reshape(1, SSM_INNER)
    const = lambda *shape: pl.BlockSpec(shape, lambda bi, c: (0,) * len(shape))
    in_specs = [
        pl.BlockSpec((1, chunk, SSM_INNER), lambda bi, c: (bi, c, OFF_Z // SSM_INNER)),
        pl.BlockSpec((1, chunk, SSM_INNER), lambda bi, c: (bi, c, OFF_X // SSM_INNER)),
        pl.BlockSpec((1, chunk, SSM_BC), lambda bi, c: (bi, c, OFF_B // SSM_BC)),
        pl.BlockSpec((1, chunk, SSM_BC), lambda bi, c: (bi, c, OFF_C // SSM_BC)),
        pl.BlockSpec((1, chunk, DT_PAD), lambda bi, c: (bi, c, 0)),
        pl.BlockSpec((1, DT_PAD, chunk), lambda bi, c: (bi, 0, c)),
        const(SSM_CONV, SSM_XBC), const(1, SSM_XBC), const(1, DT_PAD), const(DT_PAD, 1),
        const(1, DT_PAD), const(DT_PAD, 1), const(1, SSM_INNER), const(1, SSM_INNER),
        const(SSM_GROUPS, 2 * DT_PAD, SSM_GW),
    ]
    args = [proj3, proj3, proj3, proj3, dt3, dtt3, conv_w.astype(F32), conv_b.astype(F32).reshape(1, SSM_XBC),
            dtb.reshape(1, DT_PAD), dtb.reshape(DT_PAD, 1), al.reshape(1, DT_PAD), al.reshape(DT_PAD, 1),
            dsk, norm_w.astype(F32).reshape(1, SSM_INNER), _expand_table()]
    h_block = pl.BlockSpec((1, SSM_HEADS, SSM_HEAD_DIM, SSM_STATE), lambda bi, c: (bi, 0, 0, 0))
    cs_block = pl.BlockSpec((1, SSM_CONV - 1, SSM_XBC), lambda bi, c: (bi, 0, 0))
    scratch = [pltpu.VMEM((SSM_GROUPS, SSM_STATE, SSM_GW), F32),
               pltpu.VMEM((CONV_TAIL + chunk, SSM_XBC), BF16)]
    if has_init:
        in_specs += [h_block, cs_block]
        args += [h0, cs0]
        scratch += [pltpu.VMEM((SUBLANES, SSM_XBC), F32), pltpu.VMEM((2 * SUBLANES, SSM_XBC), F32)]
    state_bytes = SSM_HEADS * SSM_HEAD_DIM * SSM_STATE * 4
    est = (2 * chunk * (3 * SSM_INNER + 2 * SSM_BC) * 2 + (5 if has_init else 3) * state_bytes
           + (chunk + CONV_TAIL) * SSM_XBC * 2 + 32 * chunk * max(chunk, SSM_GW) * 4 + 8 * 1024 * 1024)
    return pl.pallas_call(
        functools.partial(_ssd_kernel, chunk=chunk, has_init=has_init),
        grid=(b, seq // chunk),
        in_specs=in_specs,
        out_specs=[pl.BlockSpec((1, chunk, SSM_INNER), lambda bi, c: (bi, c, 0)), h_block, cs_block],
        out_shape=[jax.ShapeDtypeStruct((b, seq, SSM_INNER), BF16),
                   jax.ShapeDtypeStruct((b, SSM_HEADS, SSM_HEAD_DIM, SSM_STATE), F32),
                   jax.ShapeDtypeStruct((b, SSM_CONV - 1, SSM_XBC), F32)],
        scratch_shapes=scratch,
        compiler_params=pltpu.CompilerParams(
            dimension_semantics=("parallel", "arbitrary"), vmem_limit_bytes=_vmem_limit(est)),
        name="ssd",
    )(*args)


def _merge_kernel(ar_ref, as_ref, wr_ref, ws_ref, ga_ref, gb_ref, ba_ref, bb_ref, o_ref):
    yr = jnp.dot(ar_ref[...], wr_ref[0], preferred_element_type=F32)
    ys = jnp.dot(as_ref[...], ws_ref[0], preferred_element_type=F32)
    ga = jax.nn.sigmoid(ga_ref[...].astype(F32) + ba_ref[...])
    gb = jax.nn.sigmoid(gb_ref[...].astype(F32) + bb_ref[...])
    o_ref[...] = (ga * yr + gb * ys).astype(BF16)


def _merge(a_ret, a_ssm, w_ret, w_ssm, proj, b_gate):
    rows = a_ret.shape[0]
    tm = min(1024, rows)
    tn = MERGE_TN
    nb = D_MODEL // tn
    est = 2 * 2 * tm * RET_V * 2 + 2 * 2 * RET_V * tn * 2 + 6 * tm * tn * 4
    return pl.pallas_call(
        _merge_kernel,
        grid=(rows // tm, nb),
        in_specs=[
            pl.BlockSpec((tm, RET_V), lambda i, j: (i, 0)),
            pl.BlockSpec((tm, SSM_INNER), lambda i, j: (i, 0)),
            pl.BlockSpec((1, RET_V, tn), lambda i, j: (j, 0, 0)),
            pl.BlockSpec((1, SSM_INNER, tn), lambda i, j: (j, 0, 0)),
            pl.BlockSpec((tm, tn), lambda i, j: (i, OFF_GA // tn + j)),
            pl.BlockSpec((tm, tn), lambda i, j: (i, OFF_GB // tn + j)),
            pl.BlockSpec((1, tn), lambda i, j: (0, j)),
            pl.BlockSpec((1, tn), lambda i, j: (0, nb + j)),
        ],
        out_specs=pl.BlockSpec((tm, tn), lambda i, j: (i, j)),
        out_shape=jax.ShapeDtypeStruct((rows, D_MODEL), BF16),
        compiler_params=pltpu.CompilerParams(
            dimension_semantics=("parallel", "arbitrary"), vmem_limit_bytes=_vmem_limit(est)),
        name="merge",
    )(a_ret, a_ssm, w_ret, w_ssm, proj, proj, b_gate, b_gate)


def _outproj_kernel(m_ref, w_ref, h_ref, o_ref):
    o_ref[...] = h_ref[...] + jnp.dot(m_ref[...], w_ref[...], preferred_element_type=F32)


def _outproj(m, w_out, h):
    rows = m.shape[0]
    tm = min(1024, rows)
    tn = 512
    est = 2 * tm * D_MODEL * 2 + 2 * D_MODEL * tn * 2 + 5 * tm * tn * 4
    return pl.pallas_call(
        _outproj_kernel,
        grid=(rows // tm, D_MODEL // tn),
        in_specs=[
            pl.BlockSpec((tm, D_MODEL), lambda i, j: (i, 0)),
            pl.BlockSpec((D_MODEL, tn), lambda i, j: (0, j)),
            pl.BlockSpec((tm, tn), lambda i, j: (i, j)),
        ],
        out_specs=pl.BlockSpec((tm, tn), lambda i, j: (i, j)),
        out_shape=jax.ShapeDtypeStruct((rows, D_MODEL), F32),
        compiler_params=pltpu.CompilerParams(
            dimension_semantics=("parallel", "arbitrary"), vmem_limit_bytes=_vmem_limit(est)),
        name="outproj",
    )(m, w_out, h)


def _layer(x, s_ret, s_ssm, s_conv, pos0, p, *, ret_chunk, ssd_chunk):
    b, seq, _ = x.shape
    rows = b * seq
    h = _ffn(x.reshape(rows, D_MODEL), p["norm_ffn1"], p["ffn1_w_gate"], p["ffn1_w_up"], p["ffn1_w_down"],
             p["norm_final"], final_norm=False)
    proj, dt, dtt3 = _proj(h, p["norm_mix"], p["w_all"], p["w_gates"], p["w_dt"], p["w_dtt"], p["inv_freq"],
                           batch=b, seq=seq, pos0=pos0)
    proj3 = proj.reshape(b, seq, PROJ_W)
    dt3 = dt.reshape(b, seq, DT_PAD)
    a_ret, ret_new = _retention(proj3, p["ret_norm_w"], s_ret, chunk=min(ret_chunk, seq))
    a_ssm, ssm_new, conv_new = _ssd(proj3, dt3, dtt3, p["conv_w"], p["conv_b"], p["dt_bias"], p["a_log"],
                                    p["d_skip"], p["ssm_norm_w"], s_ssm, s_conv, chunk=min(ssd_chunk, seq))
    m = _merge(a_ret.reshape(rows, RET_V), a_ssm.reshape(rows, SSM_INNER), p["w_out_ret"], p["w_out_ssm"],
               proj, p["b_gate"])
    h2 = _outproj(m, p["w_out"], h)
    y = _ffn(h2, p["norm_ffn2"], p["ffn2_w_gate"], p["ffn2_w_up"], p["ffn2_w_down"], p["norm_final"],
             final_norm=True)
    return y.reshape(b, seq, D_MODEL), ret_new, ssm_new, conv_new


def kernel(x_prompt, x_sample, state_ret, state_ssm, state_conv, norm_ffn1, ffn1_w_gate, ffn1_w_up, ffn1_w_down, norm_mix, w_in, b_gate, ret_norm_w, w_out_ret, conv_w, conv_b, dt_bias, a_log, d_skip, ssm_norm_w, w_out_ssm, w_out, norm_ffn2, ffn2_w_gate, ffn2_w_up, ffn2_w_down, norm_final):
    row = lambda v: v.astype(F32).reshape(1, -1)

    def col_tiles(w, tn):
        k, n = w.shape
        return w.astype(BF16).reshape(k, n // tn, tn).transpose(1, 0, 2)
    dt_lo = 2 * RET_QK + 2 * RET_V + SSM_INNER + SSM_XBC
    w_dt = jnp.tile(w_in[:, dt_lo:dt_lo + SSM_HEADS], (1, 2)).astype(BF16)
    p = dict(
        norm_ffn1=row(norm_ffn1), norm_mix=row(norm_mix), norm_ffn2=row(norm_ffn2), norm_final=row(norm_final),
        ffn1_w_gate=col_tiles(ffn1_w_gate, FFN_TF), ffn1_w_up=col_tiles(ffn1_w_up, FFN_TF),
        ffn1_w_down=ffn1_w_down.astype(BF16),
        ffn2_w_gate=col_tiles(ffn2_w_gate, FFN_TF), ffn2_w_up=col_tiles(ffn2_w_up, FFN_TF),
        ffn2_w_down=ffn2_w_down.astype(BF16),
        w_all=w_in.astype(BF16), w_gates=w_in[:, dt_lo + SSM_HEADS:].astype(BF16),
        w_dt=w_dt, w_dtt=w_dt.T,
        inv_freq=(ROPE_BASE ** (-jnp.arange(ROPE_HALF, dtype=F32) / ROPE_HALF)).reshape(1, ROPE_HALF),
        b_gate=row(b_gate), ret_norm_w=row(ret_norm_w),
        w_out_ret=col_tiles(w_out_ret, MERGE_TN), w_out_ssm=col_tiles(w_out_ssm, MERGE_TN),
        w_out=w_out.astype(BF16),
        conv_w=conv_w, conv_b=conv_b, dt_bias=dt_bias, a_log=a_log, d_skip=d_skip, ssm_norm_w=ssm_norm_w,
    )
    y_p, ret_p, ssm_p, conv_p = _layer(x_prompt, None, None, None, 0, p, ret_chunk=256, ssd_chunk=128)
    y_s, ret_s, ssm_s, conv_s = _layer(x_sample, state_ret.astype(F32), state_ssm.astype(F32),
                                       state_conv.astype(F32), PAST_LEN, p, ret_chunk=256, ssd_chunk=128)
    return (y_p, y_s, ret_p, ssm_p, conv_p, ret_s, ssm_s, conv_s)
```

```python
import functools
import math

import numpy as np
import jax
import jax.numpy as jnp
from jax import lax
from jax.experimental import pallas as pl
from jax.experimental.pallas import tpu as pltpu

F32 = jnp.float32
BF16 = jnp.bfloat16

D_MODEL = 2048
PAST_LEN = 1024
RET_HEADS = 8
RET_DK = D_MODEL // RET_HEADS
RET_DV = 2 * D_MODEL // RET_HEADS
RET_QK = RET_HEADS * RET_DK
RET_V = RET_HEADS * RET_DV
ROPE_BASE = 10000.0
ROPE_HALF = RET_DK // 2
SSM_INNER = 2 * D_MODEL
SSM_HEAD_DIM = 64
SSM_HEADS = SSM_INNER // SSM_HEAD_DIM
SSM_GROUPS = 8
SSM_HPG = SSM_HEADS // SSM_GROUPS
SSM_STATE = 128
SSM_CONV = 4
SSM_BC = SSM_GROUPS * SSM_STATE
SSM_XBC = SSM_INNER + 2 * SSM_BC
SSM_GW = SSM_INNER // SSM_GROUPS
D_FF = 5632
EPS = 1e-6
LOG2E = math.log2(math.e)

LANES = 128
SUBLANES = 8
BF16_SUBLANES = 16
V7X_VMEM_BYTES = 64 * 1024 * 1024

OFF_Q = 0
OFF_K = OFF_Q + RET_QK
OFF_V = OFF_K + RET_QK
OFF_G = OFF_V + RET_V
OFF_Z = OFF_G + RET_V
OFF_X = OFF_Z + SSM_INNER
OFF_B = OFF_X + SSM_INNER
OFF_C = OFF_B + SSM_BC
PROJ_W = OFF_C + SSM_BC
DT_PAD = LANES
assert DT_PAD == 2 * SSM_HEADS
CONV_TAIL = BF16_SUBLANES
FFN_TF = 512


def _vmem_limit(nbytes):
    return int(min(V7X_VMEM_BYTES - 4 * 1024 * 1024, max(nbytes, 16 * 1024 * 1024)))


def _dot(a, b):
    return jnp.dot(a.astype(BF16), b.astype(BF16), preferred_element_type=F32)


def _dot_nt(a, b):
    return lax.dot_general(a.astype(BF16), b.astype(BF16), (((1,), (1,)), ((), ())),
                           preferred_element_type=F32)


def _dot_tn(a, b):
    t = a.shape[0]
    pad = (-t) % LANES
    a = a.astype(F32)
    b = b.astype(BF16)
    if pad:
        a = jnp.concatenate([a, jnp.zeros((pad, a.shape[1]), F32)], axis=0)
        b = jnp.concatenate([b, jnp.zeros((pad, b.shape[1]), BF16)], axis=0)
    return jnp.dot(a.T.astype(BF16), b, preferred_element_type=F32)


def _rms(x, w):
    return x * lax.rsqrt(jnp.mean(x * x, axis=-1, keepdims=True) + EPS) * w


def _silu(x):
    return x * jax.nn.sigmoid(x)


def _softplus(x):
    return jnp.maximum(x, 0.0) + jnp.log1p(jnp.exp(-jnp.abs(x)))


def _split3(x):
    hi = x.astype(BF16)
    r = x - hi.astype(F32)
    mid = r.astype(BF16)
    lo = (r - mid.astype(F32)).astype(BF16)
    return hi, mid, lo


def _ffn_kernel(x_ref, nw_ref, wg_ref, wu_ref, wd_ref, fw_ref, *refs, tail, sub):
    if tail == "emit_norm":
        o_ref, no_ref, n_ref, acc_ref = refs
    else:
        o_ref, n_ref, acc_ref = refs
    j = pl.program_id(1)

    @pl.when(j == 0)
    def _():
        n_ref[...] = _rms(x_ref[...], nw_ref[...]).astype(BF16)
        acc_ref[...] = jnp.zeros_like(acc_ref)

    n = n_ref[...]
    tf = wg_ref.shape[1]
    acc = acc_ref[...]
    for lo in range(0, tf, sub):
        g = jnp.dot(n, wg_ref[:, lo:lo + sub], preferred_element_type=F32)
        u = jnp.dot(n, wu_ref[:, lo:lo + sub], preferred_element_type=F32)
        a = (_silu(g) * u).astype(BF16)
        acc = acc + jnp.dot(a, wd_ref[lo:lo + sub, :], preferred_element_type=F32)
    acc_ref[...] = acc

    @pl.when(j == pl.num_programs(1) - 1)
    def _():
        h = x_ref[...] + 0.5 * acc_ref[...]
        if tail == "emit_norm":
            o_ref[...] = h
            no_ref[...] = _rms(h, fw_ref[...]).astype(BF16)
        else:
            o_ref[...] = _rms(h, fw_ref[...])


def _ffn(x, norm_w, wg, wu, wd, tail_w, *, tail):
    rows = x.shape[0]
    tm = min(512, rows)
    tf = FFN_TF
    row_block = pl.BlockSpec((tm, D_MODEL), lambda i, j: (i, 0))
    out_specs = [row_block]
    out_shape = [jax.ShapeDtypeStruct((rows, D_MODEL), F32)]
    if tail == "emit_norm":
        out_specs.append(row_block)
        out_shape.append(jax.ShapeDtypeStruct((rows, D_MODEL), BF16))
    est = (2 * 2 * tm * D_MODEL * 4 + tm * D_MODEL * (4 + 2) + 2 * 3 * D_MODEL * tf * 2
           + 3 * tm * tf * 4 + 3 * tm * D_MODEL * 4 + 2 * tm * D_MODEL * 2)
    return pl.pallas_call(
        functools.partial(_ffn_kernel, tail=tail, sub=256),
        grid=(rows // tm, D_FF // tf),
        in_specs=[
            row_block,
            pl.BlockSpec((1, D_MODEL), lambda i, j: (0, 0)),
            pl.BlockSpec((D_MODEL, tf), lambda i, j: (0, j)),
            pl.BlockSpec((D_MODEL, tf), lambda i, j: (0, j)),
            pl.BlockSpec((tf, D_MODEL), lambda i, j: (j, 0)),
            pl.BlockSpec((1, D_MODEL), lambda i, j: (0, 0)),
        ],
        out_specs=out_specs,
        out_shape=out_shape,
        scratch_shapes=[pltpu.VMEM((tm, D_MODEL), BF16), pltpu.VMEM((tm, D_MODEL), F32)],
        compiler_params=pltpu.CompilerParams(
            dimension_semantics=("parallel", "arbitrary"), vmem_limit_bytes=_vmem_limit(est)),
        name="ffn" if tail == "emit_norm" else "ffn_final",
    )(x, norm_w, wg, wu, wd, tail_w)


def _rope_kernel(inv_ref, cos_ref, sin_ref, *, seq, pos0):
    tr = cos_ref.shape[0]
    row = pl.program_id(0) * tr + lax.broadcasted_iota(jnp.int32, (tr, 1), 0)
    ang = ((row & (seq - 1)) + pos0).astype(F32) * inv_ref[...]
    cos_ref[...] = jnp.cos(ang)
    sin_ref[...] = jnp.sin(ang)


def _rope_tables(inv_freq, *, length, seq, pos0):
    tr = min(512, length)
    spec = pl.BlockSpec((tr, ROPE_HALF), lambda i: (i, 0))
    return pl.pallas_call(
        functools.partial(_rope_kernel, seq=seq, pos0=pos0),
        grid=(length // tr,),
        in_specs=[pl.BlockSpec((1, ROPE_HALF), lambda i: (0, 0))],
        out_specs=[spec, spec],
        out_shape=[jax.ShapeDtypeStruct((length, ROPE_HALF), F32)] * 2,
        name="rope",
    )(inv_freq)


def _proj_kernel(n_ref, w_ref, cos_ref, sin_ref, o_ref, *, tn):
    j = pl.program_id(1)
    n_q = RET_QK // tn

    @pl.when(j < 2 * n_q)
    def _():
        acc = jnp.dot(n_ref[...], w_ref[...], preferred_element_type=F32)
        cos = cos_ref[...]
        sin = sin_ref[...]
        scale = jnp.where(j < n_q, 1.0, RET_DK ** -0.5).astype(F32)
        for lo in range(0, tn, RET_DK):
            x1 = acc[:, lo:lo + ROPE_HALF]
            x2 = acc[:, lo + ROPE_HALF:lo + RET_DK]
            o_ref[:, lo:lo + ROPE_HALF] = ((x1 * cos - x2 * sin) * scale).astype(BF16)
            o_ref[:, lo + ROPE_HALF:lo + RET_DK] = ((x1 * sin + x2 * cos) * scale).astype(BF16)

    @pl.when(j >= 2 * n_q)
    def _():
        o_ref[...] = jnp.dot(n_ref[...], w_ref[...], preferred_element_type=F32).astype(BF16)


def _proj(n, w_all, inv_freq, *, seq, pos0):
    rows = n.shape[0]
    tm = min(1024, rows)
    tn = 2048
    assert seq & (seq - 1) == 0 and RET_QK % tn == 0 and PROJ_W % tn == 0
    table_len = max(seq, tm)
    cos, sin = _rope_tables(inv_freq, length=table_len, seq=seq, pos0=pos0)
    n_tab = table_len // tm
    est = (2 * tm * D_MODEL * 2 + 2 * D_MODEL * tn * 2 + 2 * tm * tn * 2 + 2 * tm * tn * 4
           + 4 * tm * LANES * 4)
    return pl.pallas_call(
        functools.partial(_proj_kernel, tn=tn),
        grid=(rows // tm, PROJ_W // tn),
        in_specs=[
            pl.BlockSpec((tm, D_MODEL), lambda i, j: (i, 0)),
            pl.BlockSpec((D_MODEL, tn), lambda i, j: (0, j)),
            pl.BlockSpec((tm, ROPE_HALF), lambda i, j: (i % n_tab, 0)),
            pl.BlockSpec((tm, ROPE_HALF), lambda i, j: (i % n_tab, 0)),
        ],
        out_specs=pl.BlockSpec((tm, tn), lambda i, j: (i, j)),
        out_shape=jax.ShapeDtypeStruct((rows, PROJ_W), BF16),
        compiler_params=pltpu.CompilerParams(
            dimension_semantics=("parallel", "arbitrary"), vmem_limit_bytes=_vmem_limit(est)),
        name="proj",
    )(n, w_all, cos, sin)


def _gates_kernel(n_ref, w_ref, wdt_ref, wdtt_ref, o_ref, dt_ref, dtt_ref, *, batched_dtt):
    @pl.when(pl.program_id(1) == 0)
    def _():
        n = n_ref[...]
        dt_ref[...] = jnp.dot(n, wdt_ref[...], preferred_element_type=F32)
        dtt = _dot_nt(wdtt_ref[...], n)
        if batched_dtt:
            dtt_ref[0] = dtt
        else:
            dtt_ref[...] = dtt

    o_ref[...] = jnp.dot(n_ref[...], w_ref[...], preferred_element_type=F32).astype(BF16)


def _gates(n, w_gates, w_dt, w_dtt, *, batch, seq):
    rows = n.shape[0]
    tm = min(1024, rows)
    tn = 2048
    batched_dtt = tm <= seq
    if batched_dtt:
        per_seq = seq // tm
        dtt_spec = pl.BlockSpec((1, DT_PAD, tm), lambda i, j: (i // per_seq, 0, i % per_seq))
        dtt_shape = jax.ShapeDtypeStruct((batch, DT_PAD, seq), F32)
    else:
        dtt_spec = pl.BlockSpec((DT_PAD, tm), lambda i, j: (0, i))
        dtt_shape = jax.ShapeDtypeStruct((DT_PAD, rows), F32)
    est = (2 * tm * D_MODEL * 2 + 2 * D_MODEL * tn * 2 + 2 * tm * tn * 2 + 2 * tm * tn * 4
           + 4 * D_MODEL * DT_PAD * 2 + 6 * tm * DT_PAD * 4)
    gates, dt, dtt = pl.pallas_call(
        functools.partial(_gates_kernel, batched_dtt=batched_dtt),
        grid=(rows // tm, 2 * D_MODEL // tn),
        in_specs=[
            pl.BlockSpec((tm, D_MODEL), lambda i, j: (i, 0)),
            pl.BlockSpec((D_MODEL, tn), lambda i, j: (0, j)),
            pl.BlockSpec((D_MODEL, DT_PAD), lambda i, j: (0, 0)),
            pl.BlockSpec((DT_PAD, D_MODEL), lambda i, j: (0, 0)),
        ],
        out_specs=[
            pl.BlockSpec((tm, tn), lambda i, j: (i, j)),
            pl.BlockSpec((tm, DT_PAD), lambda i, j: (i, 0)),
            dtt_spec,
        ],
        out_shape=[
            jax.ShapeDtypeStruct((rows, 2 * D_MODEL), BF16),
            jax.ShapeDtypeStruct((rows, DT_PAD), F32),
            dtt_shape,
        ],
        compiler_params=pltpu.CompilerParams(
            dimension_semantics=("parallel", "arbitrary"), vmem_limit_bytes=_vmem_limit(est)),
        name="gates",
    )(n, w_gates, w_dt, w_dtt)
    if not batched_dtt:
        dtt = dtt.reshape(DT_PAD, batch, seq).transpose(1, 0, 2)
    return gates, dt, dtt


def _ret_kernel(*refs, chunk, has_init):
    if has_init:
        q_ref, k_ref, v_ref, g_ref, nw_ref, s0_ref, o_ref, s_ref, mask_ref = refs
    else:
        q_ref, k_ref, v_ref, g_ref, nw_ref, o_ref, s_ref, mask_ref = refs
    c = pl.program_id(1)
    log_decay = [math.log1p(-(2.0 ** (-5.0 - hd))) for hd in range(RET_HEADS)]

    @pl.when((pl.program_id(0) == 0) & (c == 0))
    def _():
        ii = lax.broadcasted_iota(jnp.int32, (chunk, chunk), 0)
        jj = lax.broadcasted_iota(jnp.int32, (chunk, chunk), 1)
        causal = ii >= jj
        diff = jnp.where(causal, ii - jj, 0).astype(F32)
        for hd in range(RET_HEADS):
            mask_ref[hd] = jnp.where(causal, jnp.exp(diff * log_decay[hd]), 0.0)

    @pl.when(c == 0)
    def _():
        if has_init:
            s_ref[...] = s0_ref[...]
        else:
            s_ref[...] = jnp.zeros_like(s_ref)

    idx = lax.broadcasted_iota(jnp.int32, (chunk, 1), 0).astype(F32)
    for hd in range(RET_HEADS):
        lg = log_decay[hd]
        decay_mask = mask_ref[hd]
        inner_decay = jnp.exp((idx + 1.0) * lg)
        state_decay = jnp.exp((chunk - 1.0 - idx) * lg)
        chunk_decay = math.exp(chunk * lg)
        q = q_ref[0, :, hd * RET_DK:(hd + 1) * RET_DK]
        k = k_ref[0, :, hd * RET_DK:(hd + 1) * RET_DK]
        v = v_ref[0, :, hd * RET_DV:(hd + 1) * RET_DV]
        s = s_ref[0, hd]
        scores = _dot_nt(q, k) * decay_mask
        o = _dot(scores, v) + _dot(q, s) * inner_decay
        s_ref[0, hd] = s * chunk_decay + _dot_tn(k.astype(F32) * state_decay, v)
        mu = jnp.mean(o, axis=-1, keepdims=True)
        var = jnp.mean(jnp.square(o - mu), axis=-1, keepdims=True)
        on = (o - mu) * lax.rsqrt(var + EPS) * nw_ref[:, hd * RET_DV:(hd + 1) * RET_DV]
        gate = g_ref[0, :, hd * RET_DV:(hd + 1) * RET_DV].astype(F32)
        o_ref[0, :, hd * RET_DV:(hd + 1) * RET_DV] = (_silu(gate) * on).astype(BF16)


def _retention(proj3, norm_w, s0, *, chunk):
    b, seq, _ = proj3.shape
    has_init = s0 is not None
    state_block = pl.BlockSpec((1, RET_HEADS, RET_DK, RET_DV), lambda bi, c: (bi, 0, 0, 0))
    in_specs = [
        pl.BlockSpec((1, chunk, RET_QK), lambda bi, c: (bi, c, OFF_Q // RET_QK)),
        pl.BlockSpec((1, chunk, RET_QK), lambda bi, c: (bi, c, OFF_K // RET_QK)),
        pl.BlockSpec((1, chunk, RET_V), lambda bi, c: (bi, c, OFF_V // RET_V)),
        pl.BlockSpec((1, chunk, RET_V), lambda bi, c: (bi, c, OFF_G // RET_V)),
        pl.BlockSpec((1, RET_V), lambda bi, c: (0, 0)),
    ]
    args = [proj3, proj3, proj3, proj3, norm_w]
    if has_init:
        in_specs.append(state_block)
        args.append(s0)
    state_bytes = RET_HEADS * RET_DK * RET_DV * 4
    est = (2 * chunk * (2 * RET_QK + 3 * RET_V) * 2 + (4 if has_init else 2) * state_bytes
           + 8 * chunk * max(chunk, RET_DV) * 4 + RET_HEADS * chunk * chunk * 4)
    return pl.pallas_call(
        functools.partial(_ret_kernel, chunk=chunk, has_init=has_init),
        grid=(b, seq // chunk),
        in_specs=in_specs,
        out_specs=[pl.BlockSpec((1, chunk, RET_V), lambda bi, c: (bi, c, 0)), state_block],
        out_shape=[jax.ShapeDtypeStruct((b, seq, RET_V), BF16),
                   jax.ShapeDtypeStruct((b, RET_HEADS, RET_DK, RET_DV), F32)],
        scratch_shapes=[pltpu.VMEM((RET_HEADS, chunk, chunk), F32)],
        compiler_params=pltpu.CompilerParams(
            dimension_semantics=("arbitrary", "arbitrary"), vmem_limit_bytes=_vmem_limit(est)),
        name="retention",
    )(*args)


def _expand_table():
    e = np.zeros((SSM_GROUPS, 2 * DT_PAD, SSM_GW), np.float32)
    for g in range(SSM_GROUPS):
        for col in range(SSM_GW):
            head = g * SSM_HPG + col // SSM_HEAD_DIM
            for piece in range(3):
                e[g, piece * SSM_HEADS + head, col] = 1.0
    return jnp.asarray(e, BF16)


def _ssd_kernel(*refs, chunk, has_init):
    if has_init:
        (z_ref, x_ref, b_ref, c_ref, dt_ref, dtt_ref, cw_ref, cb_ref, dtb_ref, dtbt_ref, al_ref,
         alt_ref, dsk_ref, nw_ref, ex_ref, h0_ref, cs0_ref, y_ref, hout_ref, csout_ref,
         ht_ref, xc_ref, fix_ref, ext_ref) = refs
    else:
        (z_ref, x_ref, b_ref, c_ref, dt_ref, dtt_ref, cw_ref, cb_ref, dtb_ref, dtbt_ref, al_ref,
         alt_ref, dsk_ref, nw_ref, ex_ref, y_ref, hout_ref, csout_ref,
         ht_ref, xc_ref) = refs
    c = pl.program_id(1)
    taps = SSM_CONV - 1
    kk = CONV_TAIL + chunk

    @pl.when(c == 0)
    def _():
        for g in range(SSM_GROUPS):
            if has_init:
                blk = h0_ref[0, g * SSM_HPG:(g + 1) * SSM_HPG].reshape(SSM_GW, SSM_STATE)
                ht_ref[g] = blk.T
            else:
                ht_ref[g] = jnp.zeros((SSM_STATE, SSM_GW), F32)
        xc_ref[0:CONV_TAIL, :] = jnp.zeros((CONV_TAIL, SSM_XBC), BF16)
        if has_init:
            ext_ref[...] = jnp.zeros_like(ext_ref)
            ext_ref[SUBLANES - taps:SUBLANES, :] = cs0_ref[0]
            fix = jnp.zeros((SUBLANES, SSM_XBC), F32)
            for i in range(taps):
                fix = fix + ext_ref[SUBLANES - taps + i:2 * SUBLANES - taps + i, :] * cw_ref[i:i + 1, :]
            fix_ref[...] = fix

    if has_init:
        @pl.when(c == 1)
        def _():
            fix_ref[...] = jnp.zeros_like(fix_ref)

    xc_ref[CONV_TAIL:kk, 0:SSM_INNER] = x_ref[0]
    xc_ref[CONV_TAIL:kk, SSM_INNER:SSM_INNER + SSM_BC] = b_ref[0]
    xc_ref[CONV_TAIL:kk, SSM_INNER + SSM_BC:SSM_XBC] = c_ref[0]

    si = lax.broadcasted_iota(jnp.int32, (taps * chunk, kk), 0)
    sj = lax.broadcasted_iota(jnp.int32, (taps * chunk, kk), 1)
    tap = si >> (chunk.bit_length() - 1)
    shift = jnp.where(sj == (si & (chunk - 1)) + tap + (CONV_TAIL - taps), 1.0, 0.0).astype(BF16)

    def conv(cur, lo, width):
        prev = jnp.dot(shift, xc_ref[:, lo:lo + width], preferred_element_type=F32)
        acc = cb_ref[:, lo:lo + width] + cur.astype(F32) * cw_ref[taps:taps + 1, lo:lo + width]
        for i in range(taps):
            acc = acc + prev[i * chunk:(i + 1) * chunk] * cw_ref[i:i + 1, lo:lo + width]
        if has_init:
            acc = jnp.concatenate([acc[:SUBLANES] + fix_ref[:, lo:lo + width], acc[SUBLANES:]], axis=0)
        return _silu(acc)

    a_row = -jnp.exp(al_ref[...]) * LOG2E
    a_col = -jnp.exp(alt_ref[...]) * LOG2E
    dt = _softplus(dt_ref[0] + dtb_ref[...])
    dtt = _softplus(dtt_ref[0] + dtbt_ref[...])
    ii = lax.broadcasted_iota(jnp.int32, (chunk, chunk), 0)
    jj = lax.broadcasted_iota(jnp.int32, (chunk, chunk), 1)
    tri = ii >= jj
    lower = tri.astype(BF16)
    upper = (ii <= jj).astype(BF16)
    cum = sum(jnp.dot(lower, p, preferred_element_type=F32) for p in _split3(dt * a_row))
    cumt = sum(jnp.dot(p, upper, preferred_element_type=F32) for p in _split3(dtt * a_col))

    first_half = lax.broadcasted_iota(jnp.int32, (chunk, LANES), 1) < SSM_HEADS

    def pieces(v):
        hi, mid, lo = _split3(v)
        return jnp.concatenate([jnp.where(first_half, hi, mid), lo], axis=1)

    dt_p = pieces(dt)
    cum_p = pieces(cum)
    lane_head = lax.broadcasted_iota(jnp.int32, (chunk, LANES), 1) < SSM_HEAD_DIM

    def stage_a(g):
        xs = conv(x_ref[0, :, g * SSM_GW:(g + 1) * SSM_GW], g * SSM_GW, SSM_GW)
        bm = conv(b_ref[0, :, g * SSM_STATE:(g + 1) * SSM_STATE], SSM_INNER + g * SSM_STATE, SSM_STATE)
        cm = conv(c_ref[0, :, g * SSM_STATE:(g + 1) * SSM_STATE], SSM_INNER + SSM_BC + g * SSM_STATE, SSM_STATE)
        cbm = jnp.where(tri, _dot_nt(cm, bm), 0.0)
        dt_e = jnp.dot(dt_p, ex_ref[g], preferred_element_type=F32)
        cum_e = jnp.dot(cum_p, ex_ref[g], preferred_element_type=F32)
        e_all = jnp.exp2(cum_e)
        xdt = xs * dt_e
        xd = xdt * jnp.exp2(cum_e[chunk - 1:chunk, :] - cum_e)
        return xs, bm, cm, cbm, e_all, xd, xdt.astype(BF16)

    nxt = stage_a(0)
    for g in range(SSM_GROUPS):
        xs, bm, cm, cbm, e_all, xd, xdt16 = nxt
        if g + 1 < SSM_GROUPS:
            nxt = stage_a(g + 1)
        y_parts = []
        for s in range(SSM_GW // LANES):
            rhs = xdt16[:, s * LANES:(s + 1) * LANES]
            prod = []
            for j in (g * SSM_HPG + 2 * s, g * SSM_HPG + 2 * s + 1):
                seg = (jnp.broadcast_to(cum[:, j:j + 1], (chunk, chunk))
                       - jnp.broadcast_to(cumt[j:j + 1, :], (chunk, chunk)))
                m = cbm * jnp.exp2(jnp.minimum(seg, 0.0))
                prod.append(jnp.dot(m.astype(BF16), rhs, preferred_element_type=F32))
            y_parts.append(jnp.where(lane_head, prod[0], prod[1]))
        htg = ht_ref[g]
        y = (jnp.concatenate(y_parts, axis=1) + _dot(cm, htg) * e_all
             + xs * dsk_ref[:, g * SSM_GW:(g + 1) * SSM_GW])
        ht_ref[g] = htg * e_all[chunk - 1:chunk, :] + _dot_tn(bm, xd)
        y = y * _silu(z_ref[0, :, g * SSM_GW:(g + 1) * SSM_GW].astype(F32))
        y = _rms(y, nw_ref[:, g * SSM_GW:(g + 1) * SSM_GW])
        y_ref[0, :, g * SSM_GW:(g + 1) * SSM_GW] = y.astype(BF16)

    xc_ref[0:CONV_TAIL, :] = xc_ref[chunk:kk, :]

    @pl.when(c == pl.num_programs(1) - 1)
    def _():
        csout_ref[0] = xc_ref[chunk:kk, :].astype(F32)[CONV_TAIL - taps:CONV_TAIL, :]
        for g in range(SSM_GROUPS):
            hout_ref[0, g * SSM_HPG:(g + 1) * SSM_HPG] = ht_ref[g].T.reshape(SSM_HPG, SSM_HEAD_DIM, SSM_STATE)


def _ssd(proj3, dt3, dtt3, conv_w, conv_b, dt_bias, a_log, d_skip, norm_w, h0, cs0, *, chunk):
    b, seq, _ = proj3.shape
    has_init = h0 is not None
    assert chunk % SUBLANES == 0 and chunk >= SUBLANES
    dtb = jnp.tile(dt_bias.astype(F32), 2)
    al = jnp.tile(a_log.astype(F32), 2)
    dsk = jnp.repeat(d_skip.astype(F32), SSM_HEAD_DIM).reshape(1, SSM_INNER)
    const = lambda *shape: pl.BlockSpec(shape, lambda bi, c: (0,) * len(shape))
    in_specs = [
        pl.BlockSpec((1, chunk, SSM_INNER), lambda bi, c: (bi, c, OFF_Z // SSM_INNER)),
        pl.BlockSpec((1, chunk, SSM_INNER), lambda bi, c: (bi, c, OFF_X // SSM_INNER)),
        pl.BlockSpec((1, chunk, SSM_BC), lambda bi, c: (bi, c, OFF_B // SSM_BC)),
        pl.BlockSpec((1, chunk, SSM_BC), lambda bi, c: (bi, c, OFF_C // SSM_BC)),
        pl.BlockSpec((1, chunk, DT_PAD), lambda bi, c: (bi, c, 0)),
        pl.BlockSpec((1, DT_PAD, chunk), lambda bi, c: (bi, 0, c)),
        const(SSM_CONV, SSM_XBC), const(1, SSM_XBC), const(1, DT_PAD), const(DT_PAD, 1),
        const(1, DT_PAD), const(DT_PAD, 1), const(1, SSM_INNER), const(1, SSM_INNER),
        const(SSM_GROUPS, 2 * DT_PAD, SSM_GW),
    ]
    args = [proj3, proj3, proj3, proj3, dt3, dtt3, conv_w.astype(F32), conv_b.astype(F32).reshape(1, SSM_XBC),
            dtb.reshape(1, DT_PAD), dtb.reshape(DT_PAD, 1), al.reshape(1, DT_PAD), al.reshape(DT_PAD, 1),
            dsk, norm_w.astype(F32).reshape(1, SSM_INNER), _expand_table()]
    h_block = pl.BlockSpec((1, SSM_HEADS, SSM_HEAD_DIM, SSM_STATE), lambda bi, c: (bi, 0, 0, 0))
    cs_block = pl.BlockSpec((1, SSM_CONV - 1, SSM_XBC), lambda bi, c: (bi, 0, 0))
    scratch = [pltpu.VMEM((SSM_GROUPS, SSM_STATE, SSM_GW), F32),
               pltpu.VMEM((CONV_TAIL + chunk, SSM_XBC), BF16)]
    if has_init:
        in_specs += [h_block, cs_block]
        args += [h0, cs0]
        scratch += [pltpu.VMEM((SUBLANES, SSM_XBC), F32), pltpu.VMEM((2 * SUBLANES, SSM_XBC), F32)]
    state_bytes = SSM_HEADS * SSM_HEAD_DIM * SSM_STATE * 4
    est = (2 * chunk * (3 * SSM_INNER + 2 * SSM_BC) * 2 + (5 if has_init else 3) * state_bytes
           + (chunk + CONV_TAIL) * SSM_XBC * 2 + 32 * chunk * max(chunk, SSM_GW) * 4 + 8 * 1024 * 1024)
    return pl.pallas_call(
        functools.partial(_ssd_kernel, chunk=chunk, has_init=has_init),
        grid=(b, seq // chunk),
        in_specs=in_specs,
        out_specs=[pl.BlockSpec((1, chunk, SSM_INNER), lambda bi, c: (bi, c, 0)), h_block, cs_block],
        out_shape=[jax.ShapeDtypeStruct((b, seq, SSM_INNER), BF16),
                   jax.ShapeDtypeStruct((b, SSM_HEADS, SSM_HEAD_DIM, SSM_STATE), F32),
                   jax.ShapeDtypeStruct((b, SSM_CONV - 1, SSM_XBC), F32)],
        scratch_shapes=scratch,
        compiler_params=pltpu.CompilerParams(
            dimension_semantics=("parallel", "arbitrary"), vmem_limit_bytes=_vmem_limit(est)),
        name="ssd",
    )(*args)


def _merge_kernel(ar_ref, as_ref, wr_ref, ws_ref, ga_ref, gb_ref, ba_ref, bb_ref, o_ref, *, sub):
    for lo in range(0, o_ref.shape[0], sub):
        rows = slice(lo, lo + sub)
        yr = jnp.dot(ar_ref[rows, :], wr_ref[...], preferred_element_type=F32)
        ys = jnp.dot(as_ref[rows, :], ws_ref[...], preferred_element_type=F32)
        ga = jax.nn.sigmoid(ga_ref[rows, :].astype(F32) + ba_ref[...])
        gb = jax.nn.sigmoid(gb_ref[rows, :].astype(F32) + bb_ref[...])
        o_ref[rows, :] = (ga * yr + gb * ys).astype(BF16)


def _merge(a_ret, a_ssm, w_ret, w_ssm, gates, b_gate):
    rows = a_ret.shape[0]
    tm = min(1024, rows)
    tn = 512
    nb = D_MODEL // tn
    est = 2 * 2 * tm * RET_V * 2 + 2 * 2 * RET_V * tn * 2 + 8 * tm * tn * 4
    return pl.pallas_call(
        functools.partial(_merge_kernel, sub=min(512, tm)),
        grid=(rows // tm, nb),
        in_specs=[
            pl.BlockSpec((tm, RET_V), lambda i, j: (i, 0)),
            pl.BlockSpec((tm, SSM_INNER), lambda i, j: (i, 0)),
            pl.BlockSpec((RET_V, tn), lambda i, j: (0, j)),
            pl.BlockSpec((SSM_INNER, tn), lambda i, j: (0, j)),
            pl.BlockSpec((tm, tn), lambda i, j: (i, j)),
            pl.BlockSpec((tm, tn), lambda i, j: (i, nb + j)),
            pl.BlockSpec((1, tn), lambda i, j: (0, j)),
            pl.BlockSpec((1, tn), lambda i, j: (0, nb + j)),
        ],
        out_specs=pl.BlockSpec((tm, tn), lambda i, j: (i, j)),
        out_shape=jax.ShapeDtypeStruct((rows, D_MODEL), BF16),
        compiler_params=pltpu.CompilerParams(
            dimension_semantics=("parallel", "arbitrary"), vmem_limit_bytes=_vmem_limit(est)),
        name="merge",
    )(a_ret, a_ssm, w_ret, w_ssm, gates, gates, b_gate, b_gate)


def _outproj_kernel(m_ref, w_ref, h_ref, o_ref):
    o_ref[...] = h_ref[...] + jnp.dot(m_ref[...], w_ref[...], preferred_element_type=F32)


def _outproj(m, w_out, h):
    rows = m.shape[0]
    tm = min(1024, rows)
    tn = 1024
    est = 2 * tm * D_MODEL * 2 + 2 * D_MODEL * tn * 2 + 5 * tm * tn * 4
    return pl.pallas_call(
        _outproj_kernel,
        grid=(rows // tm, D_MODEL // tn),
        in_specs=[
            pl.BlockSpec((tm, D_MODEL), lambda i, j: (i, 0)),
            pl.BlockSpec((D_MODEL, tn), lambda i, j: (0, j)),
            pl.BlockSpec((tm, tn), lambda i, j: (i, j)),
        ],
        out_specs=pl.BlockSpec((tm, tn), lambda i, j: (i, j)),
        out_shape=jax.ShapeDtypeStruct((rows, D_MODEL), F32),
        compiler_params=pltpu.CompilerParams(
            dimension_semantics=("parallel", "arbitrary"), vmem_limit_bytes=_vmem_limit(est)),
        name="outproj",
    )(m, w_out, h)


def _layer(x, s_ret, s_ssm, s_conv, pos0, p, *, ret_chunk, ssd_chunk):
    b, seq, _ = x.shape
    rows = b * seq
    h, n = _ffn(x.reshape(rows, D_MODEL), p["norm_ffn1"], p["ffn1_w_gate"], p["ffn1_w_up"], p["ffn1_w_down"],
                p["norm_mix"], tail="emit_norm")
    proj = _proj(n, p["w_all"], p["inv_freq"], seq=seq, pos0=pos0)
    gates, dt, dtt3 = _gates(n, p["w_gates"], p["w_dt"], p["w_dtt"], batch=b, seq=seq)
    proj3 = proj.reshape(b, seq, PROJ_W)
    dt3 = dt.reshape(b, seq, DT_PAD)
    a_ret, ret_new = _retention(proj3, p["ret_norm_w"], s_ret, chunk=min(ret_chunk, seq))
    a_ssm, ssm_new, conv_new = _ssd(proj3, dt3, dtt3, p["conv_w"], p["conv_b"], p["dt_bias"], p["a_log"],
                                    p["d_skip"], p["ssm_norm_w"], s_ssm, s_conv, chunk=min(ssd_chunk, seq))
    m = _merge(a_ret.reshape(rows, RET_V), a_ssm.reshape(rows, SSM_INNER), p["w_out_ret"], p["w_out_ssm"],
               gates, p["b_gate"])
    h2 = _outproj(m, p["w_out"], h)
    y, = _ffn(h2, p["norm_ffn2"], p["ffn2_w_gate"], p["ffn2_w_up"], p["ffn2_w_down"], p["norm_final"],
              tail="final_norm")
    return y.reshape(b, seq, D_MODEL), ret_new, ssm_new, conv_new


def kernel(x_prompt, x_sample, state_ret, state_ssm, state_conv, norm_ffn1, ffn1_w_gate, ffn1_w_up, ffn1_w_down, norm_mix, w_in, b_gate, ret_norm_w, w_out_ret, conv_w, conv_b, dt_bias, a_log, d_skip, ssm_norm_w, w_out_ssm, w_out, norm_ffn2, ffn2_w_gate, ffn2_w_up, ffn2_w_down, norm_final):
    row = lambda v: v.astype(F32).reshape(1, -1)
    dt_lo = 2 * RET_QK + 2 * RET_V + SSM_INNER + SSM_XBC
    w_dt = jnp.tile(w_in[:, dt_lo:dt_lo + SSM_HEADS], (1, 2)).astype(BF16)
    p = dict(
        norm_ffn1=row(norm_ffn1), norm_mix=row(norm_mix), norm_ffn2=row(norm_ffn2), norm_final=row(norm_final),
        ffn1_w_gate=ffn1_w_gate.astype(BF16), ffn1_w_up=ffn1_w_up.astype(BF16), ffn1_w_down=ffn1_w_down.astype(BF16),
        ffn2_w_gate=ffn2_w_gate.astype(BF16), ffn2_w_up=ffn2_w_up.astype(BF16), ffn2_w_down=ffn2_w_down.astype(BF16),
        w_all=w_in.astype(BF16), w_gates=w_in[:, dt_lo + SSM_HEADS:].astype(BF16),
        w_dt=w_dt, w_dtt=w_dt.T,
        inv_freq=(ROPE_BASE ** (-jnp.arange(ROPE_HALF, dtype=F32) / ROPE_HALF)).reshape(1, ROPE_HALF),
        b_gate=row(b_gate), ret_norm_w=row(ret_norm_w),
        w_out_ret=w_out_ret.astype(BF16), w_out_ssm=w_out_ssm.astype(BF16), w_out=w_out.astype(BF16),
        conv_w=conv_w, conv_b=conv_b, dt_bias=dt_bias, a_log=a_log, d_skip=d_skip, ssm_norm_w=ssm_norm_w,
    )
    y_p, ret_p, ssm_p, conv_p = _layer(x_prompt, None, None, None, 0, p, ret_chunk=256, ssd_chunk=128)
    y_s, ret_s, ssm_s, conv_s = _layer(x_sample, state_ret.astype(F32), state_ssm.astype(F32),
                                       state_conv.astype(F32), PAST_LEN, p, ret_chunk=256, ssd_chunk=128)
    return (y_p, y_s, ret_p, ssm_p, conv_p, ret_s, ssm_s, conv_s)
```

```python
import functools
import math

import numpy as np
import jax
import jax.numpy as jnp
from jax import lax
from jax.experimental import pallas as pl
from jax.experimental.pallas import tpu as pltpu

F32 = jnp.float32
BF16 = jnp.bfloat16

D_MODEL = 2048
PAST_LEN = 1024
RET_HEADS = 8
RET_DK = D_MODEL // RET_HEADS
RET_DV = 2 * D_MODEL // RET_HEADS
RET_QK = RET_HEADS * RET_DK
RET_V = RET_HEADS * RET_DV
ROPE_BASE = 10000.0
ROPE_HALF = RET_DK // 2
SSM_INNER = 2 * D_MODEL
SSM_HEAD_DIM = 64
SSM_HEADS = SSM_INNER // SSM_HEAD_DIM
SSM_GROUPS = 8
SSM_HPG = SSM_HEADS // SSM_GROUPS
SSM_STATE = 128
SSM_CONV = 4
SSM_BC = SSM_GROUPS * SSM_STATE
SSM_XBC = SSM_INNER + 2 * SSM_BC
SSM_GW = SSM_INNER // SSM_GROUPS
D_FF = 5632
EPS = 1e-6
LOG2E = math.log2(math.e)

LANES = 128
SUBLANES = 8
BF16_SUBLANES = 16
V7X_VMEM_BYTES = 64 * 1024 * 1024

OFF_Q = 0
OFF_K = OFF_Q + RET_QK
OFF_V = OFF_K + RET_QK
OFF_G = OFF_V + RET_V
OFF_Z = OFF_G + RET_V
OFF_X = OFF_Z + SSM_INNER
OFF_B = OFF_X + SSM_INNER
OFF_C = OFF_B + SSM_BC
PROJ_W = OFF_C + SSM_BC
DT_PAD = LANES
assert DT_PAD == 2 * SSM_HEADS
CONV_TAIL = BF16_SUBLANES
FFN_TF = 1024


def _vmem_limit(nbytes):
    return int(min(V7X_VMEM_BYTES - 4 * 1024 * 1024, max(nbytes, 16 * 1024 * 1024)))


def _dot(a, b):
    return jnp.dot(a.astype(BF16), b.astype(BF16), preferred_element_type=F32)


def _dot_nt(a, b):
    return lax.dot_general(a.astype(BF16), b.astype(BF16), (((1,), (1,)), ((), ())),
                           preferred_element_type=F32)


def _dot_tn(a, b):
    t = a.shape[0]
    pad = (-t) % LANES
    a = a.astype(F32)
    b = b.astype(BF16)
    if pad:
        a = jnp.concatenate([a, jnp.zeros((pad, a.shape[1]), F32)], axis=0)
        b = jnp.concatenate([b, jnp.zeros((pad, b.shape[1]), BF16)], axis=0)
    return jnp.dot(a.T.astype(BF16), b, preferred_element_type=F32)


def _rms(x, w):
    return x * lax.rsqrt(jnp.mean(x * x, axis=-1, keepdims=True) + EPS) * w


def _silu(x):
    return x * jax.nn.sigmoid(x)


def _softplus(x):
    return jnp.maximum(x, 0.0) + jnp.log1p(jnp.exp(-jnp.abs(x)))


def _split3(x):
    hi = x.astype(BF16)
    r = x - hi.astype(F32)
    mid = r.astype(BF16)
    lo = (r - mid.astype(F32)).astype(BF16)
    return hi, mid, lo


def _ffn_kernel(x_ref, nw_ref, wg_ref, wu_ref, wd_ref, fw_ref, *refs, tail, sub, last_cols):
    if tail == "emit_norm":
        o_ref, no_ref, n_ref, acc_ref = refs
    else:
        o_ref, n_ref, acc_ref = refs
    j = pl.program_id(1)

    @pl.when(j == 0)
    def _():
        n_ref[...] = _rms(x_ref[...], nw_ref[...]).astype(BF16)
        acc_ref[...] = jnp.zeros_like(acc_ref)

    def accumulate(cols):
        n = n_ref[...]
        acc = acc_ref[...]
        for lo in range(0, cols, sub):
            g = jnp.dot(n, wg_ref[:, lo:lo + sub], preferred_element_type=F32)
            u = jnp.dot(n, wu_ref[:, lo:lo + sub], preferred_element_type=F32)
            a = (_silu(g) * u).astype(BF16)
            acc = acc + jnp.dot(a, wd_ref[lo:lo + sub, :], preferred_element_type=F32)
        acc_ref[...] = acc

    tf = wg_ref.shape[1]
    last = pl.num_programs(1) - 1
    if last_cols == tf:
        accumulate(tf)
    else:
        pl.when(j < last)(lambda: accumulate(tf))
        pl.when(j == last)(lambda: accumulate(last_cols))

    @pl.when(j == pl.num_programs(1) - 1)
    def _():
        h = x_ref[...] + 0.5 * acc_ref[...]
        if tail == "emit_norm":
            o_ref[...] = h
            no_ref[...] = _rms(h, fw_ref[...]).astype(BF16)
        else:
            o_ref[...] = _rms(h, fw_ref[...])


def _ffn(x, norm_w, wg, wu, wd, tail_w, *, tail):
    rows = x.shape[0]
    tm = min(512, rows)
    tf = FFN_TF
    steps = pl.cdiv(D_FF, tf)
    last_cols = D_FF - (steps - 1) * tf
    row_block = pl.BlockSpec((tm, D_MODEL), lambda i, j: (i, 0))
    out_specs = [row_block]
    out_shape = [jax.ShapeDtypeStruct((rows, D_MODEL), F32)]
    if tail == "emit_norm":
        out_specs.append(row_block)
        out_shape.append(jax.ShapeDtypeStruct((rows, D_MODEL), BF16))
    est = (2 * 2 * tm * D_MODEL * 4 + tm * D_MODEL * (4 + 2) + 2 * 3 * D_MODEL * tf * 2
           + 2 * tm * D_MODEL * 4 + 2 * tm * D_MODEL * 2)
    return pl.pallas_call(
        functools.partial(_ffn_kernel, tail=tail, sub=256, last_cols=last_cols),
        grid=(rows // tm, steps),
        in_specs=[
            row_block,
            pl.BlockSpec((1, D_MODEL), lambda i, j: (0, 0)),
            pl.BlockSpec((D_MODEL, tf), lambda i, j: (0, j)),
            pl.BlockSpec((D_MODEL, tf), lambda i, j: (0, j)),
            pl.BlockSpec((tf, D_MODEL), lambda i, j: (j, 0)),
            pl.BlockSpec((1, D_MODEL), lambda i, j: (0, 0)),
        ],
        out_specs=out_specs,
        out_shape=out_shape,
        scratch_shapes=[pltpu.VMEM((tm, D_MODEL), BF16), pltpu.VMEM((tm, D_MODEL), F32)],
        compiler_params=pltpu.CompilerParams(
            dimension_semantics=("parallel", "arbitrary"), vmem_limit_bytes=_vmem_limit(est)),
        name="ffn" if tail == "emit_norm" else "ffn_final",
    )(x, norm_w, wg, wu, wd, tail_w)


def _rope_kernel(inv_ref, cos_ref, sin_ref, *, seq, pos0):
    tr = cos_ref.shape[0]
    row = pl.program_id(0) * tr + lax.broadcasted_iota(jnp.int32, (tr, 1), 0)
    ang = ((row & (seq - 1)) + pos0).astype(F32) * inv_ref[...]
    cos_ref[...] = jnp.cos(ang)
    sin_ref[...] = jnp.sin(ang)


def _rope_tables(inv_freq, *, length, seq, pos0):
    tr = min(512, length)
    spec = pl.BlockSpec((tr, ROPE_HALF), lambda i: (i, 0))
    return pl.pallas_call(
        functools.partial(_rope_kernel, seq=seq, pos0=pos0),
        grid=(length // tr,),
        in_specs=[pl.BlockSpec((1, ROPE_HALF), lambda i: (0, 0))],
        out_specs=[spec, spec],
        out_shape=[jax.ShapeDtypeStruct((length, ROPE_HALF), F32)] * 2,
        name="rope",
    )(inv_freq)


def _proj_kernel(n_ref, w_ref, cos_ref, sin_ref, o_ref, *, tn):
    j = pl.program_id(1)
    n_q = RET_QK // tn

    @pl.when(j < 2 * n_q)
    def _():
        acc = jnp.dot(n_ref[...], w_ref[...], preferred_element_type=F32)
        cos = cos_ref[...]
        sin = sin_ref[...]
        scale = jnp.where(j < n_q, 1.0, RET_DK ** -0.5).astype(F32)
        for lo in range(0, tn, RET_DK):
            x1 = acc[:, lo:lo + ROPE_HALF]
            x2 = acc[:, lo + ROPE_HALF:lo + RET_DK]
            o_ref[:, lo:lo + ROPE_HALF] = ((x1 * cos - x2 * sin) * scale).astype(BF16)
            o_ref[:, lo + ROPE_HALF:lo + RET_DK] = ((x1 * sin + x2 * cos) * scale).astype(BF16)

    @pl.when(j >= 2 * n_q)
    def _():
        o_ref[...] = jnp.dot(n_ref[...], w_ref[...], preferred_element_type=F32).astype(BF16)


def _proj(n, w_all, inv_freq, *, seq, pos0):
    rows = n.shape[0]
    tm = min(1024, rows)
    tn = 2048
    assert seq & (seq - 1) == 0 and RET_QK % tn == 0 and PROJ_W % tn == 0
    table_len = max(seq, tm)
    cos, sin = _rope_tables(inv_freq, length=table_len, seq=seq, pos0=pos0)
    n_tab = table_len // tm
    est = (2 * tm * D_MODEL * 2 + 2 * D_MODEL * tn * 2 + 2 * tm * tn * 2 + 2 * tm * tn * 4
           + 4 * tm * LANES * 4)
    return pl.pallas_call(
        functools.partial(_proj_kernel, tn=tn),
        grid=(rows // tm, PROJ_W // tn),
        in_specs=[
            pl.BlockSpec((tm, D_MODEL), lambda i, j: (i, 0)),
            pl.BlockSpec((D_MODEL, tn), lambda i, j: (0, j)),
            pl.BlockSpec((tm, ROPE_HALF), lambda i, j: (i % n_tab, 0)),
            pl.BlockSpec((tm, ROPE_HALF), lambda i, j: (i % n_tab, 0)),
        ],
        out_specs=pl.BlockSpec((tm, tn), lambda i, j: (i, j)),
        out_shape=jax.ShapeDtypeStruct((rows, PROJ_W), BF16),
        compiler_params=pltpu.CompilerParams(
            dimension_semantics=("parallel", "arbitrary"), vmem_limit_bytes=_vmem_limit(est)),
        name="proj",
    )(n, w_all, cos, sin)


def _gates_kernel(n_ref, w_ref, wdt_ref, wdtt_ref, o_ref, dt_ref, dtt_ref, *, batched_dtt):
    @pl.when(pl.program_id(1) == 0)
    def _():
        n = n_ref[...]
        dt_ref[...] = jnp.dot(n, wdt_ref[...], preferred_element_type=F32)
        dtt = _dot_nt(wdtt_ref[...], n)
        if batched_dtt:
            dtt_ref[0] = dtt
        else:
            dtt_ref[...] = dtt

    o_ref[...] = jnp.dot(n_ref[...], w_ref[...], preferred_element_type=F32).astype(BF16)


def _gates(n, w_gates, w_dt, w_dtt, *, batch, seq):
    rows = n.shape[0]
    tm = min(1024, rows)
    tn = 2048
    batched_dtt = tm <= seq
    if batched_dtt:
        per_seq = seq // tm
        dtt_spec = pl.BlockSpec((1, DT_PAD, tm), lambda i, j: (i // per_seq, 0, i % per_seq))
        dtt_shape = jax.ShapeDtypeStruct((batch, DT_PAD, seq), F32)
    else:
        dtt_spec = pl.BlockSpec((DT_PAD, tm), lambda i, j: (0, i))
        dtt_shape = jax.ShapeDtypeStruct((DT_PAD, rows), F32)
    est = (2 * tm * D_MODEL * 2 + 2 * D_MODEL * tn * 2 + 2 * tm * tn * 2 + 2 * tm * tn * 4
           + 4 * D_MODEL * DT_PAD * 2 + 6 * tm * DT_PAD * 4)
    gates, dt, dtt = pl.pallas_call(
        functools.partial(_gates_kernel, batched_dtt=batched_dtt),
        grid=(rows // tm, 2 * D_MODEL // tn),
        in_specs=[
            pl.BlockSpec((tm, D_MODEL), lambda i, j: (i, 0)),
            pl.BlockSpec((D_MODEL, tn), lambda i, j: (0, j)),
            pl.BlockSpec((D_MODEL, DT_PAD), lambda i, j: (0, 0)),
            pl.BlockSpec((DT_PAD, D_MODEL), lambda i, j: (0, 0)),
        ],
        out_specs=[
            pl.BlockSpec((tm, tn), lambda i, j: (i, j)),
            pl.BlockSpec((tm, DT_PAD), lambda i, j: (i, 0)),
            dtt_spec,
        ],
        out_shape=[
            jax.ShapeDtypeStruct((rows, 2 * D_MODEL), BF16),
            jax.ShapeDtypeStruct((rows, DT_PAD), F32),
            dtt_shape,
        ],
        compiler_params=pltpu.CompilerParams(
            dimension_semantics=("parallel", "arbitrary"), vmem_limit_bytes=_vmem_limit(est)),
        name="gates",
    )(n, w_gates, w_dt, w_dtt)
    if not batched_dtt:
        dtt = dtt.reshape(DT_PAD, batch, seq).transpose(1, 0, 2)
    return gates, dt, dtt


def _ret_kernel(*refs, chunk, has_init):
    if has_init:
        q_ref, k_ref, v_ref, g_ref, nw_ref, s0_ref, o_ref, s_ref, mask_ref = refs
    else:
        q_ref, k_ref, v_ref, g_ref, nw_ref, o_ref, s_ref, mask_ref = refs
    c = pl.program_id(1)
    log_decay = [math.log1p(-(2.0 ** (-5.0 - hd))) for hd in range(RET_HEADS)]

    @pl.when((pl.program_id(0) == 0) & (c == 0))
    def _():
        ii = lax.broadcasted_iota(jnp.int32, (chunk, chunk), 0)
        jj = lax.broadcasted_iota(jnp.int32, (chunk, chunk), 1)
        causal = ii >= jj
        diff = jnp.where(causal, ii - jj, 0).astype(F32)
        for hd in range(RET_HEADS):
            mask_ref[hd] = jnp.where(causal, jnp.exp(diff * log_decay[hd]), 0.0)

    @pl.when(c == 0)
    def _():
        if has_init:
            s_ref[...] = s0_ref[...]
        else:
            s_ref[...] = jnp.zeros_like(s_ref)

    idx = lax.broadcasted_iota(jnp.int32, (chunk, 1), 0).astype(F32)
    for hd in range(RET_HEADS):
        lg = log_decay[hd]
        decay_mask = mask_ref[hd]
        inner_decay = jnp.exp((idx + 1.0) * lg)
        state_decay = jnp.exp((chunk - 1.0 - idx) * lg)
        chunk_decay = math.exp(chunk * lg)
        q = q_ref[0, :, hd * RET_DK:(hd + 1) * RET_DK]
        k = k_ref[0, :, hd * RET_DK:(hd + 1) * RET_DK]
        v = v_ref[0, :, hd * RET_DV:(hd + 1) * RET_DV]
        s = s_ref[0, hd]
        scores = _dot_nt(q, k) * decay_mask
        o = _dot(scores, v) + _dot(q, s) * inner_decay
        s_ref[0, hd] = s * chunk_decay + _dot_tn(k.astype(F32) * state_decay, v)
        mu = jnp.mean(o, axis=-1, keepdims=True)
        var = jnp.mean(jnp.square(o - mu), axis=-1, keepdims=True)
        on = (o - mu) * lax.rsqrt(var + EPS) * nw_ref[:, hd * RET_DV:(hd + 1) * RET_DV]
        gate = g_ref[0, :, hd * RET_DV:(hd + 1) * RET_DV].astype(F32)
        o_ref[0, :, hd * RET_DV:(hd + 1) * RET_DV] = (_silu(gate) * on).astype(BF16)


def _retention(proj3, norm_w, s0, *, chunk):
    b, seq, _ = proj3.shape
    has_init = s0 is not None
    state_block = pl.BlockSpec((1, RET_HEADS, RET_DK, RET_DV), lambda bi, c: (bi, 0, 0, 0))
    in_specs = [
        pl.BlockSpec((1, chunk, RET_QK), lambda bi, c: (bi, c, OFF_Q // RET_QK)),
        pl.BlockSpec((1, chunk, RET_QK), lambda bi, c: (bi, c, OFF_K // RET_QK)),
        pl.BlockSpec((1, chunk, RET_V), lambda bi, c: (bi, c, OFF_V // RET_V)),
        pl.BlockSpec((1, chunk, RET_V), lambda bi, c: (bi, c, OFF_G // RET_V)),
        pl.BlockSpec((1, RET_V), lambda bi, c: (0, 0)),
    ]
    args = [proj3, proj3, proj3, proj3, norm_w]
    if has_init:
        in_specs.append(state_block)
        args.append(s0)
    state_bytes = RET_HEADS * RET_DK * RET_DV * 4
    est = (2 * chunk * (2 * RET_QK + 3 * RET_V) * 2 + (4 if has_init else 2) * state_bytes
           + 8 * chunk * max(chunk, RET_DV) * 4 + RET_HEADS * chunk * chunk * 4)
    return pl.pallas_call(
        functools.partial(_ret_kernel, chunk=chunk, has_init=has_init),
        grid=(b, seq // chunk),
        in_specs=in_specs,
        out_specs=[pl.BlockSpec((1, chunk, RET_V), lambda bi, c: (bi, c, 0)), state_block],
        out_shape=[jax.ShapeDtypeStruct((b, seq, RET_V), BF16),
                   jax.ShapeDtypeStruct((b, RET_HEADS, RET_DK, RET_DV), F32)],
        scratch_shapes=[pltpu.VMEM((RET_HEADS, chunk, chunk), F32)],
        compiler_params=pltpu.CompilerParams(
            dimension_semantics=("arbitrary", "arbitrary"), vmem_limit_bytes=_vmem_limit(est)),
        name="retention",
    )(*args)


def _expand_table():
    e = np.zeros((SSM_GROUPS, 2 * DT_PAD, SSM_GW), np.float32)
    for g in range(SSM_GROUPS):
        for col in range(SSM_GW):
            head = g * SSM_HPG + col // SSM_HEAD_DIM
            for piece in range(3):
                e[g, piece * SSM_HEADS + head, col] = 1.0
    return jnp.asarray(e, BF16)


def _ssd_kernel(*refs, chunk, has_init):
    if has_init:
        (z_ref, x_ref, b_ref, c_ref, dt_ref, dtt_ref, cw_ref, cb_ref, dtb_ref, dtbt_ref, al_ref,
         alt_ref, dsk_ref, nw_ref, ex_ref, h0_ref, cs0_ref, y_ref, hout_ref, csout_ref,
         ht_ref, xc_ref, fix_ref, ext_ref) = refs
    else:
        (z_ref, x_ref, b_ref, c_ref, dt_ref, dtt_ref, cw_ref, cb_ref, dtb_ref, dtbt_ref, al_ref,
         alt_ref, dsk_ref, nw_ref, ex_ref, y_ref, hout_ref, csout_ref,
         ht_ref, xc_ref) = refs
    c = pl.program_id(1)
    taps = SSM_CONV - 1
    kk = CONV_TAIL + chunk

    @pl.when(c == 0)
    def _():
        for g in range(SSM_GROUPS):
            if has_init:
                blk = h0_ref[0, g * SSM_HPG:(g + 1) * SSM_HPG].reshape(SSM_GW, SSM_STATE)
                ht_ref[g] = blk.T
            else:
                ht_ref[g] = jnp.zeros((SSM_STATE, SSM_GW), F32)
        xc_ref[0:CONV_TAIL, :] = jnp.zeros((CONV_TAIL, SSM_XBC), BF16)
        if has_init:
            ext_ref[...] = jnp.zeros_like(ext_ref)
            ext_ref[SUBLANES - taps:SUBLANES, :] = cs0_ref[0]
            fix = jnp.zeros((SUBLANES, SSM_XBC), F32)
            for i in range(taps):
                fix = fix + ext_ref[SUBLANES - taps + i:2 * SUBLANES - taps + i, :] * cw_ref[i:i + 1, :]
            fix_ref[...] = fix

    if has_init:
        @pl.when(c == 1)
        def _():
            fix_ref[...] = jnp.zeros_like(fix_ref)

    xc_ref[CONV_TAIL:kk, 0:SSM_INNER] = x_ref[0]
    xc_ref[CONV_TAIL:kk, SSM_INNER:SSM_INNER + SSM_BC] = b_ref[0]
    xc_ref[CONV_TAIL:kk, SSM_INNER + SSM_BC:SSM_XBC] = c_ref[0]

    si = lax.broadcasted_iota(jnp.int32, (taps * chunk, kk), 0)
    sj = lax.broadcasted_iota(jnp.int32, (taps * chunk, kk), 1)
    tap = si >> (chunk.bit_length() - 1)
    shift = jnp.where(sj == (si & (chunk - 1)) + tap + (CONV_TAIL - taps), 1.0, 0.0).astype(BF16)

    def conv(cur, lo, width):
        prev = jnp.dot(shift, xc_ref[:, lo:lo + width], preferred_element_type=F32)
        acc = cb_ref[:, lo:lo + width] + cur.astype(F32) * cw_ref[taps:taps + 1, lo:lo + width]
        for i in range(taps):
            acc = acc + prev[i * chunk:(i + 1) * chunk] * cw_ref[i:i + 1, lo:lo + width]
        if has_init:
            acc = jnp.concatenate([acc[:SUBLANES] + fix_ref[:, lo:lo + width], acc[SUBLANES:]], axis=0)
        return _silu(acc)

    a_row = -jnp.exp(al_ref[...]) * LOG2E
    a_col = -jnp.exp(alt_ref[...]) * LOG2E
    dt = _softplus(dt_ref[0] + dtb_ref[...])
    dtt = _softplus(dtt_ref[0] + dtbt_ref[...])
    ii = lax.broadcasted_iota(jnp.int32, (chunk, chunk), 0)
    jj = lax.broadcasted_iota(jnp.int32, (chunk, chunk), 1)
    tri = ii >= jj
    lower = tri.astype(BF16)
    upper = (ii <= jj).astype(BF16)
    cum = sum(jnp.dot(lower, p, preferred_element_type=F32) for p in _split3(dt * a_row))
    cumt = sum(jnp.dot(p, upper, preferred_element_type=F32) for p in _split3(dtt * a_col))

    first_half = lax.broadcasted_iota(jnp.int32, (chunk, LANES), 1) < SSM_HEADS

    def pieces(v):
        hi, mid, lo = _split3(v)
        return jnp.concatenate([jnp.where(first_half, hi, mid), lo], axis=1)

    dt_p = pieces(dt)
    cum_p = pieces(cum)
    lane_head = lax.broadcasted_iota(jnp.int32, (chunk, LANES), 1) < SSM_HEAD_DIM

    def stage_a(g):
        xs = conv(x_ref[0, :, g * SSM_GW:(g + 1) * SSM_GW], g * SSM_GW, SSM_GW)
        bm = conv(b_ref[0, :, g * SSM_STATE:(g + 1) * SSM_STATE], SSM_INNER + g * SSM_STATE, SSM_STATE)
        cm = conv(c_ref[0, :, g * SSM_STATE:(g + 1) * SSM_STATE], SSM_INNER + SSM_BC + g * SSM_STATE, SSM_STATE)
        cbm = jnp.where(tri, _dot_nt(cm, bm), 0.0)
        dt_e = jnp.dot(dt_p, ex_ref[g], preferred_element_type=F32)
        cum_e = jnp.dot(cum_p, ex_ref[g], preferred_element_type=F32)
        e_all = jnp.exp2(cum_e)
        xdt = xs * dt_e
        xd = xdt * jnp.exp2(cum_e[chunk - 1:chunk, :] - cum_e)
        return xs, bm, cm, cbm, e_all, xd, xdt.astype(BF16)

    nxt = stage_a(0)
    for g in range(SSM_GROUPS):
        xs, bm, cm, cbm, e_all, xd, xdt16 = nxt
        if g + 1 < SSM_GROUPS:
            nxt = stage_a(g + 1)
        y_parts = []
        for s in range(SSM_GW // LANES):
            rhs = xdt16[:, s * LANES:(s + 1) * LANES]
            prod = []
            for j in (g * SSM_HPG + 2 * s, g * SSM_HPG + 2 * s + 1):
                seg = (jnp.broadcast_to(cum[:, j:j + 1], (chunk, chunk))
                       - jnp.broadcast_to(cumt[j:j + 1, :], (chunk, chunk)))
                m = cbm * jnp.exp2(jnp.minimum(seg, 0.0))
                prod.append(jnp.dot(m.astype(BF16), rhs, preferred_element_type=F32))
            y_parts.append(jnp.where(lane_head, prod[0], prod[1]))
        htg = ht_ref[g]
        y = (jnp.concatenate(y_parts, axis=1) + _dot(cm, htg) * e_all
             + xs * dsk_ref[:, g * SSM_GW:(g + 1) * SSM_GW])
        ht_ref[g] = htg * e_all[chunk - 1:chunk, :] + _dot_tn(bm, xd)
        y = y * _silu(z_ref[0, :, g * SSM_GW:(g + 1) * SSM_GW].astype(F32))
        y = _rms(y, nw_ref[:, g * SSM_GW:(g + 1) * SSM_GW])
        y_ref[0, :, g * SSM_GW:(g + 1) * SSM_GW] = y.astype(BF16)

    xc_ref[0:CONV_TAIL, :] = xc_ref[chunk:kk, :]

    @pl.when(c == pl.num_programs(1) - 1)
    def _():
        csout_ref[0] = xc_ref[chunk:kk, :].astype(F32)[CONV_TAIL - taps:CONV_TAIL, :]
        for g in range(SSM_GROUPS):
            hout_ref[0, g * SSM_HPG:(g + 1) * SSM_HPG] = ht_ref[g].T.reshape(SSM_HPG, SSM_HEAD_DIM, SSM_STATE)


def _ssd(proj3, dt3, dtt3, conv_w, conv_b, dt_bias, a_log, d_skip, norm_w, h0, cs0, *, chunk):
    b, seq, _ = proj3.shape
    has_init = h0 is not None
    assert chunk % SUBLANES == 0 and chunk >= SUBLANES
    dtb = jnp.tile(dt_bias.astype(F32), 2)
    al = jnp.tile(a_log.astype(F32), 2)
    dsk = jnp.repeat(d_skip.astype(F32), SSM_HEAD_DIM).reshape(1, SSM_INNER)
    const = lambda *shape: pl.BlockSpec(shape, lambda bi, c: (0,) * len(shape))
    in_specs = [
        pl.BlockSpec((1, chunk, SSM_INNER), lambda bi, c: (bi, c, OFF_Z // SSM_INNER)),
        pl.BlockSpec((1, chunk, SSM_INNER), lambda bi, c: (bi, c, OFF_X // SSM_INNER)),
        pl.BlockSpec((1, chunk, SSM_BC), lambda bi, c: (bi, c, OFF_B // SSM_BC)),
        pl.BlockSpec((1, chunk, SSM_BC), lambda bi, c: (bi, c, OFF_C // SSM_BC)),
        pl.BlockSpec((1, chunk, DT_PAD), lambda bi, c: (bi, c, 0)),
        pl.BlockSpec((1, DT_PAD, chunk), lambda bi, c: (bi, 0, c)),
        const(SSM_CONV, SSM_XBC), const(1, SSM_XBC), const(1, DT_PAD), const(DT_PAD, 1),
        const(1, DT_PAD), const(DT_PAD, 1), const(1, SSM_INNER), const(1, SSM_INNER),
        const(SSM_GROUPS, 2 * DT_PAD, SSM_GW),
    ]
    args = [proj3, proj3, proj3, proj3, dt3, dtt3, conv_w.astype(F32), conv_b.astype(F32).reshape(1, SSM_XBC),
            dtb.reshape(1, DT_PAD), dtb.reshape(DT_PAD, 1), al.reshape(1, DT_PAD), al.reshape(DT_PAD, 1),
            dsk, norm_w.astype(F32).reshape(1, SSM_INNER), _expand_table()]
    h_block = pl.BlockSpec((1, SSM_HEADS, SSM_HEAD_DIM, SSM_STATE), lambda bi, c: (bi, 0, 0, 0))
    cs_block = pl.BlockSpec((1, SSM_CONV - 1, SSM_XBC), lambda bi, c: (bi, 0, 0))
    scratch = [pltpu.VMEM((SSM_GROUPS, SSM_STATE, SSM_GW), F32),
               pltpu.VMEM((CONV_TAIL + chunk, SSM_XBC), BF16)]
    if has_init:
        in_specs += [h_block, cs_block]
        args += [h0, cs0]
        scratch += [pltpu.VMEM((SUBLANES, SSM_XBC), F32), pltpu.VMEM((2 * SUBLANES, SSM_XBC), F32)]
    state_bytes = SSM_HEADS * SSM_HEAD_DIM * SSM_STATE * 4
    est = (2 * chunk * (3 * SSM_INNER + 2 * SSM_BC) * 2 + (5 if has_init else 3) * state_bytes
           + (chunk + CONV_TAIL) * SSM_XBC * 2 + 32 * chunk * max(chunk, SSM_GW) * 4 + 8 * 1024 * 1024)
    return pl.pallas_call(
        functools.partial(_ssd_kernel, chunk=chunk, has_init=has_init),
        grid=(b, seq // chunk),
        in_specs=in_specs,
        out_specs=[pl.BlockSpec((1, chunk, SSM_INNER), lambda bi, c: (bi, c, 0)), h_block, cs_block],
        out_shape=[jax.ShapeDtypeStruct((b, seq, SSM_INNER), BF16),
                   jax.ShapeDtypeStruct((b, SSM_HEADS, SSM_HEAD_DIM, SSM_STATE), F32),
                   jax.ShapeDtypeStruct((b, SSM_CONV - 1, SSM_XBC), F32)],
        scratch_shapes=scratch,
        compiler_params=pltpu.CompilerParams(
            dimension_semantics=("parallel", "arbitrary"), vmem_limit_bytes=_vmem_limit(est)),
        name="ssd",
    )(*args)


def _merge_kernel(ar_ref, as_ref, wr_ref, ws_ref, ga_ref, gb_ref, ba_ref, bb_ref, o_ref, *, sub):
    for lo in range(0, o_ref.shape[0], sub):
        rows = slice(lo, lo + sub)
        yr = jnp.dot(ar_ref[rows, :], wr_ref[...], preferred_element_type=F32)
        ys = jnp.dot(as_ref[rows, :], ws_ref[...], preferred_element_type=F32)
        ga = jax.nn.sigmoid(ga_ref[rows, :].astype(F32) + ba_ref[...])
        gb = jax.nn.sigmoid(gb_ref[rows, :].astype(F32) + bb_ref[...])
        o_ref[rows, :] = (ga * yr + gb * ys).astype(BF16)


def _merge(a_ret, a_ssm, w_ret, w_ssm, gates, b_gate):
    rows = a_ret.shape[0]
    tm = min(1024, rows)
    tn = 512
    nb = D_MODEL // tn
    est = 2 * 2 * tm * RET_V * 2 + 2 * 2 * RET_V * tn * 2 + 8 * tm * tn * 4
    return pl.pallas_call(
        functools.partial(_merge_kernel, sub=min(512, tm)),
        grid=(rows // tm, nb),
        in_specs=[
            pl.BlockSpec((tm, RET_V), lambda i, j: (i, 0)),
            pl.BlockSpec((tm, SSM_INNER), lambda i, j: (i, 0)),
            pl.BlockSpec((RET_V, tn), lambda i, j: (0, j)),
            pl.BlockSpec((SSM_INNER, tn), lambda i, j: (0, j)),
            pl.BlockSpec((tm, tn), lambda i, j: (i, j)),
            pl.BlockSpec((tm, tn), lambda i, j: (i, nb + j)),
            pl.BlockSpec((1, tn), lambda i, j: (0, j)),
            pl.BlockSpec((1, tn), lambda i, j: (0, nb + j)),
        ],
        out_specs=pl.BlockSpec((tm, tn), lambda i, j: (i, j)),
        out_shape=jax.ShapeDtypeStruct((rows, D_MODEL), BF16),
        compiler_params=pltpu.CompilerParams(
            dimension_semantics=("parallel", "arbitrary"), vmem_limit_bytes=_vmem_limit(est)),
        name="merge",
    )(a_ret, a_ssm, w_ret, w_ssm, gates, gates, b_gate, b_gate)


def _outproj_kernel(m_ref, w_ref, h_ref, o_ref):
    o_ref[...] = h_ref[...] + jnp.dot(m_ref[...], w_ref[...], preferred_element_type=F32)


def _outproj(m, w_out, h):
    rows = m.shape[0]
    tm = min(1024, rows)
    tn = 1024
    est = 2 * tm * D_MODEL * 2 + 2 * D_MODEL * tn * 2 + 5 * tm * tn * 4
    return pl.pallas_call(
        _outproj_kernel,
        grid=(rows // tm, D_MODEL // tn),
        in_specs=[
            pl.BlockSpec((tm, D_MODEL), lambda i, j: (i, 0)),
            pl.BlockSpec((D_MODEL, tn), lambda i, j: (0, j)),
            pl.BlockSpec((tm, tn), lambda i, j: (i, j)),
        ],
        out_specs=pl.BlockSpec((tm, tn), lambda i, j: (i, j)),
        out_shape=jax.ShapeDtypeStruct((rows, D_MODEL), F32),
        compiler_params=pltpu.CompilerParams(
            dimension_semantics=("parallel", "arbitrary"), vmem_limit_bytes=_vmem_limit(est)),
        name="outproj",
    )(m, w_out, h)


def _layer(x, s_ret, s_ssm, s_conv, pos0, p, *, ret_chunk, ssd_chunk):
    b, seq, _ = x.shape
    rows = b * seq
    h, n = _ffn(x.reshape(rows, D_MODEL), p["norm_ffn1"], p["ffn1_w_gate"], p["ffn1_w_up"], p["ffn1_w_down"],
                p["norm_mix"], tail="emit_norm")
    proj = _proj(n, p["w_all"], p["inv_freq"], seq=seq, pos0=pos0)
    gates, dt, dtt3 = _gates(n, p["w_gates"], p["w_dt"], p["w_dtt"], batch=b, seq=seq)
    proj3 = proj.reshape(b, seq, PROJ_W)
    dt3 = dt.reshape(b, seq, DT_PAD)
    a_ret, ret_new = _retention(proj3, p["ret_norm_w"], s_ret, chunk=min(ret_chunk, seq))
    a_ssm, ssm_new, conv_new = _ssd(proj3, dt3, dtt3, p["conv_w"], p["conv_b"], p["dt_bias"], p["a_log"],
                                    p["d_skip"], p["ssm_norm_w"], s_ssm, s_conv, chunk=min(ssd_chunk, seq))
    m = _merge(a_ret.reshape(rows, RET_V), a_ssm.reshape(rows, SSM_INNER), p["w_out_ret"], p["w_out_ssm"],
               gates, p["b_gate"])
    h2 = _outproj(m, p["w_out"], h)
    y, = _ffn(h2, p["norm_ffn2"], p["ffn2_w_gate"], p["ffn2_w_up"], p["ffn2_w_down"], p["norm_final"],
              tail="final_norm")
    return y.reshape(b, seq, D_MODEL), ret_new, ssm_new, conv_new


def kernel(x_prompt, x_sample, state_ret, state_ssm, state_conv, norm_ffn1, ffn1_w_gate, ffn1_w_up, ffn1_w_down, norm_mix, w_in, b_gate, ret_norm_w, w_out_ret, conv_w, conv_b, dt_bias, a_log, d_skip, ssm_norm_w, w_out_ssm, w_out, norm_ffn2, ffn2_w_gate, ffn2_w_up, ffn2_w_down, norm_final):
    row = lambda v: v.astype(F32).reshape(1, -1)
    dt_lo = 2 * RET_QK + 2 * RET_V + SSM_INNER + SSM_XBC
    w_dt = jnp.tile(w_in[:, dt_lo:dt_lo + SSM_HEADS], (1, 2)).astype(BF16)
    p = dict(
        norm_ffn1=row(norm_ffn1), norm_mix=row(norm_mix), norm_ffn2=row(norm_ffn2), norm_final=row(norm_final),
        ffn1_w_gate=ffn1_w_gate.astype(BF16), ffn1_w_up=ffn1_w_up.astype(BF16), ffn1_w_down=ffn1_w_down.astype(BF16),
        ffn2_w_gate=ffn2_w_gate.astype(BF16), ffn2_w_up=ffn2_w_up.astype(BF16), ffn2_w_down=ffn2_w_down.astype(BF16),
        w_all=w_in.astype(BF16), w_gates=w_in[:, dt_lo + SSM_HEADS:].astype(BF16),
        w_dt=w_dt, w_dtt=w_dt.T,
        inv_freq=(ROPE_BASE ** (-jnp.arange(ROPE_HALF, dtype=F32) / ROPE_HALF)).reshape(1, ROPE_HALF),
        b_gate=row(b_gate), ret_norm_w=row(ret_norm_w),
        w_out_ret=w_out_ret.astype(BF16), w_out_ssm=w_out_ssm.astype(BF16), w_out=w_out.astype(BF16),
        conv_w=conv_w, conv_b=conv_b, dt_bias=dt_bias, a_log=a_log, d_skip=d_skip, ssm_norm_w=ssm_norm_w,
    )
    y_p, ret_p, ssm_p, conv_p = _layer(x_prompt, None, None, None, 0, p, ret_chunk=256, ssd_chunk=128)
    y_s, ret_s, ssm_s, conv_s = _layer(x_sample, state_ret.astype(F32), state_ssm.astype(F32),
                                       state_conv.astype(F32), PAST_LEN, p, ret_chunk=256, ssd_chunk=128)
    return (y_p, y_s, ret_p, ssm_p, conv_p, ret_s, ssm_s, conv_s)
```

```python
import functools
import math

import numpy as np
import jax
import jax.numpy as jnp
from jax import lax
from jax.experimental import pallas as pl
from jax.experimental.pallas import tpu as pltpu

F32 = jnp.float32
BF16 = jnp.bfloat16

D_MODEL = 2048
PAST_LEN = 1024
RET_HEADS = 8
RET_DK = D_MODEL // RET_HEADS
RET_DV = 2 * D_MODEL // RET_HEADS
RET_QK = RET_HEADS * RET_DK
RET_V = RET_HEADS * RET_DV
ROPE_BASE = 10000.0
ROPE_HALF = RET_DK // 2
SSM_INNER = 2 * D_MODEL
SSM_HEAD_DIM = 64
SSM_HEADS = SSM_INNER // SSM_HEAD_DIM
SSM_GROUPS = 8
SSM_HPG = SSM_HEADS // SSM_GROUPS
SSM_STATE = 128
SSM_CONV = 4
SSM_BC = SSM_GROUPS * SSM_STATE
SSM_XBC = SSM_INNER + 2 * SSM_BC
SSM_GW = SSM_INNER // SSM_GROUPS
D_FF = 5632
EPS = 1e-6
LOG2E = math.log2(math.e)

LANES = 128
SUBLANES = 8
BF16_SUBLANES = 16
V7X_VMEM_BYTES = 64 * 1024 * 1024

OFF_Q = 0
OFF_K = OFF_Q + RET_QK
OFF_V = OFF_K + RET_QK
OFF_G = OFF_V + RET_V
OFF_Z = OFF_G + RET_V
OFF_X = OFF_Z + SSM_INNER
OFF_B = OFF_X + SSM_INNER
OFF_C = OFF_B + SSM_BC
PROJ_W = OFF_C + SSM_BC
DT_PAD = LANES
assert DT_PAD == 2 * SSM_HEADS
CONV_TAIL = BF16_SUBLANES
FFN_TILES = {"emit_norm": (512, 512), "final_norm": (1024, 256)}


def _vmem_limit(nbytes):
    return int(min(V7X_VMEM_BYTES - 4 * 1024 * 1024, max(nbytes, 16 * 1024 * 1024)))


def _dot(a, b):
    return jnp.dot(a.astype(BF16), b.astype(BF16), preferred_element_type=F32)


def _dot_nt(a, b):
    return lax.dot_general(a.astype(BF16), b.astype(BF16), (((1,), (1,)), ((), ())),
                           preferred_element_type=F32)


def _dot_tn(a, b):
    t = a.shape[0]
    pad = (-t) % LANES
    a = a.astype(F32)
    b = b.astype(BF16)
    if pad:
        a = jnp.concatenate([a, jnp.zeros((pad, a.shape[1]), F32)], axis=0)
        b = jnp.concatenate([b, jnp.zeros((pad, b.shape[1]), BF16)], axis=0)
    return jnp.dot(a.T.astype(BF16), b, preferred_element_type=F32)


def _rms(x, w):
    return x * lax.rsqrt(jnp.mean(x * x, axis=-1, keepdims=True) + EPS) * w


def _silu(x):
    return x * jax.nn.sigmoid(x)


def _softplus(x):
    return jnp.maximum(x, 0.0) + jnp.log1p(jnp.exp(-jnp.abs(x)))


def _split3(x):
    hi = x.astype(BF16)
    r = x - hi.astype(F32)
    mid = r.astype(BF16)
    lo = (r - mid.astype(F32)).astype(BF16)
    return hi, mid, lo


def _ffn_kernel(x_ref, nw_ref, wg_ref, wu_ref, wd_ref, fw_ref, *refs, tail, sub):
    if tail == "emit_norm":
        o_ref, no_ref, n_ref = refs
    else:
        o_ref, n_ref = refs
    j = pl.program_id(1)

    @pl.when(j == 0)
    def _():
        n_ref[...] = _rms(x_ref[...], nw_ref[...]).astype(BF16)
        o_ref[...] = jnp.zeros_like(o_ref)

    n = n_ref[...]
    acc = o_ref[...]
    for lo in range(0, wg_ref.shape[1], sub):
        g = jnp.dot(n, wg_ref[:, lo:lo + sub], preferred_element_type=F32)
        u = jnp.dot(n, wu_ref[:, lo:lo + sub], preferred_element_type=F32)
        a = (_silu(g) * u).astype(BF16)
        acc = acc + jnp.dot(a, wd_ref[lo:lo + sub, :], preferred_element_type=F32)
    o_ref[...] = acc

    @pl.when(j == pl.num_programs(1) - 1)
    def _():
        h = x_ref[...] + 0.5 * o_ref[...]
        if tail == "emit_norm":
            o_ref[...] = h
            no_ref[...] = _rms(h, fw_ref[...]).astype(BF16)
        else:
            o_ref[...] = _rms(h, fw_ref[...])


def _ffn(x, norm_w, wg, wu, wd, tail_w, *, tail):
    rows = x.shape[0]
    tm, tf = FFN_TILES[tail]
    tm = min(tm, rows)
    assert D_FF % tf == 0
    row_block = pl.BlockSpec((tm, D_MODEL), lambda i, j: (i, 0))
    out_specs = [row_block]
    out_shape = [jax.ShapeDtypeStruct((rows, D_MODEL), F32)]
    if tail == "emit_norm":
        out_specs.append(row_block)
        out_shape.append(jax.ShapeDtypeStruct((rows, D_MODEL), BF16))
    sub = min(256, tf)
    est = (2 * 2 * tm * D_MODEL * 4 + tm * D_MODEL * 2 + 2 * 3 * D_MODEL * tf * 2
           + 2 * tm * D_MODEL * 4 + 3 * tm * sub * 4 + (2 * tm * D_MODEL * 2 if tail == "emit_norm" else 0))
    return pl.pallas_call(
        functools.partial(_ffn_kernel, tail=tail, sub=sub),
        grid=(rows // tm, D_FF // tf),
        in_specs=[
            row_block,
            pl.BlockSpec((1, D_MODEL), lambda i, j: (0, 0)),
            pl.BlockSpec((D_MODEL, tf), lambda i, j: (0, j)),
            pl.BlockSpec((D_MODEL, tf), lambda i, j: (0, j)),
            pl.BlockSpec((tf, D_MODEL), lambda i, j: (j, 0)),
            pl.BlockSpec((1, D_MODEL), lambda i, j: (0, 0)),
        ],
        out_specs=out_specs,
        out_shape=out_shape,
        scratch_shapes=[pltpu.VMEM((tm, D_MODEL), BF16)],
        compiler_params=pltpu.CompilerParams(
            dimension_semantics=("parallel", "arbitrary"), vmem_limit_bytes=_vmem_limit(est)),
        name="ffn" if tail == "emit_norm" else "ffn_final",
    )(x, norm_w, wg, wu, wd, tail_w)


def _rope_kernel(inv_ref, cos_ref, sin_ref, *, seq, pos0):
    tr = cos_ref.shape[0]
    row = pl.program_id(0) * tr + lax.broadcasted_iota(jnp.int32, (tr, 1), 0)
    ang = ((row & (seq - 1)) + pos0).astype(F32) * inv_ref[...]
    cos_ref[...] = jnp.cos(ang)
    sin_ref[...] = jnp.sin(ang)


def _rope_tables(inv_freq, *, length, seq, pos0):
    tr = min(512, length)
    spec = pl.BlockSpec((tr, ROPE_HALF), lambda i: (i, 0))
    return pl.pallas_call(
        functools.partial(_rope_kernel, seq=seq, pos0=pos0),
        grid=(length // tr,),
        in_specs=[pl.BlockSpec((1, ROPE_HALF), lambda i: (0, 0))],
        out_specs=[spec, spec],
        out_shape=[jax.ShapeDtypeStruct((length, ROPE_HALF), F32)] * 2,
        name="rope",
    )(inv_freq)


def _proj_kernel(n_ref, w_ref, cos_ref, sin_ref, o_ref, *, tn):
    j = pl.program_id(1)
    n_q = RET_QK // tn

    @pl.when(j < 2 * n_q)
    def _():
        acc = jnp.dot(n_ref[...], w_ref[...], preferred_element_type=F32)
        cos = cos_ref[...]
        sin = sin_ref[...]
        scale = jnp.where(j < n_q, 1.0, RET_DK ** -0.5).astype(F32)
        for lo in range(0, tn, RET_DK):
            x1 = acc[:, lo:lo + ROPE_HALF]
            x2 = acc[:, lo + ROPE_HALF:lo + RET_DK]
            o_ref[:, lo:lo + ROPE_HALF] = ((x1 * cos - x2 * sin) * scale).astype(BF16)
            o_ref[:, lo + ROPE_HALF:lo + RET_DK] = ((x1 * sin + x2 * cos) * scale).astype(BF16)

    @pl.when(j >= 2 * n_q)
    def _():
        o_ref[...] = jnp.dot(n_ref[...], w_ref[...], preferred_element_type=F32).astype(BF16)


def _proj(n, w_all, inv_freq, *, seq, pos0):
    rows = n.shape[0]
    tm = min(1024, rows)
    tn = 2048
    assert seq & (seq - 1) == 0 and RET_QK % tn == 0 and PROJ_W % tn == 0
    table_len = max(seq, tm)
    cos, sin = _rope_tables(inv_freq, length=table_len, seq=seq, pos0=pos0)
    n_tab = table_len // tm
    est = (2 * tm * D_MODEL * 2 + 2 * D_MODEL * tn * 2 + 2 * tm * tn * 2 + 2 * tm * tn * 4
           + 4 * tm * LANES * 4)
    return pl.pallas_call(
        functools.partial(_proj_kernel, tn=tn),
        grid=(rows // tm, PROJ_W // tn),
        in_specs=[
            pl.BlockSpec((tm, D_MODEL), lambda i, j: (i, 0)),
            pl.BlockSpec((D_MODEL, tn), lambda i, j: (0, j)),
            pl.BlockSpec((tm, ROPE_HALF), lambda i, j: (i % n_tab, 0)),
            pl.BlockSpec((tm, ROPE_HALF), lambda i, j: (i % n_tab, 0)),
        ],
        out_specs=pl.BlockSpec((tm, tn), lambda i, j: (i, j)),
        out_shape=jax.ShapeDtypeStruct((rows, PROJ_W), BF16),
        compiler_params=pltpu.CompilerParams(
            dimension_semantics=("parallel", "arbitrary"), vmem_limit_bytes=_vmem_limit(est)),
        name="proj",
    )(n, w_all, cos, sin)


def _gates_kernel(n_ref, w_ref, wdt_ref, wdtt_ref, o_ref, dt_ref, dtt_ref, *, batched_dtt):
    @pl.when(pl.program_id(1) == 0)
    def _():
        n = n_ref[...]
        dt_ref[...] = jnp.dot(n, wdt_ref[...], preferred_element_type=F32)
        dtt = _dot_nt(wdtt_ref[...], n)
        if batched_dtt:
            dtt_ref[0] = dtt
        else:
            dtt_ref[...] = dtt

    o_ref[...] = jnp.dot(n_ref[...], w_ref[...], preferred_element_type=F32).astype(BF16)


def _gates(n, w_gates, w_dt, w_dtt, *, batch, seq):
    rows = n.shape[0]
    tm = min(1024, rows)
    tn = 2048
    batched_dtt = tm <= seq
    if batched_dtt:
        per_seq = seq // tm
        dtt_spec = pl.BlockSpec((1, DT_PAD, tm), lambda i, j: (i // per_seq, 0, i % per_seq))
        dtt_shape = jax.ShapeDtypeStruct((batch, DT_PAD, seq), F32)
    else:
        dtt_spec = pl.BlockSpec((DT_PAD, tm), lambda i, j: (0, i))
        dtt_shape = jax.ShapeDtypeStruct((DT_PAD, rows), F32)
    est = (2 * tm * D_MODEL * 2 + 2 * D_MODEL * tn * 2 + 2 * tm * tn * 2 + 2 * tm * tn * 4
           + 4 * D_MODEL * DT_PAD * 2 + 6 * tm * DT_PAD * 4)
    gates, dt, dtt = pl.pallas_call(
        functools.partial(_gates_kernel, batched_dtt=batched_dtt),
        grid=(rows // tm, 2 * D_MODEL // tn),
        in_specs=[
            pl.BlockSpec((tm, D_MODEL), lambda i, j: (i, 0)),
            pl.BlockSpec((D_MODEL, tn), lambda i, j: (0, j)),
            pl.BlockSpec((D_MODEL, DT_PAD), lambda i, j: (0, 0)),
            pl.BlockSpec((DT_PAD, D_MODEL), lambda i, j: (0, 0)),
        ],
        out_specs=[
            pl.BlockSpec((tm, tn), lambda i, j: (i, j)),
            pl.BlockSpec((tm, DT_PAD), lambda i, j: (i, 0)),
            dtt_spec,
        ],
        out_shape=[
            jax.ShapeDtypeStruct((rows, 2 * D_MODEL), BF16),
            jax.ShapeDtypeStruct((rows, DT_PAD), F32),
            dtt_shape,
        ],
        compiler_params=pltpu.CompilerParams(
            dimension_semantics=("parallel", "arbitrary"), vmem_limit_bytes=_vmem_limit(est)),
        name="gates",
    )(n, w_gates, w_dt, w_dtt)
    if not batched_dtt:
        dtt = dtt.reshape(DT_PAD, batch, seq).transpose(1, 0, 2)
    return gates, dt, dtt


def _ret_kernel(*refs, chunk, has_init):
    if has_init:
        q_ref, k_ref, v_ref, g_ref, nw_ref, s0_ref, o_ref, s_ref, mask_ref = refs
    else:
        q_ref, k_ref, v_ref, g_ref, nw_ref, o_ref, s_ref, mask_ref = refs
    c = pl.program_id(1)
    log_decay = [math.log1p(-(2.0 ** (-5.0 - hd))) for hd in range(RET_HEADS)]

    @pl.when((pl.program_id(0) == 0) & (c == 0))
    def _():
        ii = lax.broadcasted_iota(jnp.int32, (chunk, chunk), 0)
        jj = lax.broadcasted_iota(jnp.int32, (chunk, chunk), 1)
        causal = ii >= jj
        diff = jnp.where(causal, ii - jj, 0).astype(F32)
        for hd in range(RET_HEADS):
            mask_ref[hd] = jnp.where(causal, jnp.exp(diff * log_decay[hd]), 0.0)

    @pl.when(c == 0)
    def _():
        if has_init:
            s_ref[...] = s0_ref[...]
        else:
            s_ref[...] = jnp.zeros_like(s_ref)

    idx = lax.broadcasted_iota(jnp.int32, (chunk, 1), 0).astype(F32)
    for hd in range(RET_HEADS):
        lg = log_decay[hd]
        decay_mask = mask_ref[hd]
        inner_decay = jnp.exp((idx + 1.0) * lg)
        state_decay = jnp.exp((chunk - 1.0 - idx) * lg)
        chunk_decay = math.exp(chunk * lg)
        q = q_ref[0, :, hd * RET_DK:(hd + 1) * RET_DK]
        k = k_ref[0, :, hd * RET_DK:(hd + 1) * RET_DK]
        v = v_ref[0, :, hd * RET_DV:(hd + 1) * RET_DV]
        s = s_ref[0, hd]
        scores = _dot_nt(q, k) * decay_mask
        o = _dot(scores, v) + _dot(q, s) * inner_decay
        s_ref[0, hd] = s * chunk_decay + _dot_tn(k.astype(F32) * state_decay, v)
        mu = jnp.mean(o, axis=-1, keepdims=True)
        var = jnp.mean(jnp.square(o - mu), axis=-1, keepdims=True)
        on = (o - mu) * lax.rsqrt(var + EPS) * nw_ref[:, hd * RET_DV:(hd + 1) * RET_DV]
        gate = g_ref[0, :, hd * RET_DV:(hd + 1) * RET_DV].astype(F32)
        o_ref[0, :, hd * RET_DV:(hd + 1) * RET_DV] = (_silu(gate) * on).astype(BF16)


def _retention(proj3, norm_w, s0, *, chunk):
    b, seq, _ = proj3.shape
    has_init = s0 is not None
    state_block = pl.BlockSpec((1, RET_HEADS, RET_DK, RET_DV), lambda bi, c: (bi, 0, 0, 0))
    in_specs = [
        pl.BlockSpec((1, chunk, RET_QK), lambda bi, c: (bi, c, OFF_Q // RET_QK)),
        pl.BlockSpec((1, chunk, RET_QK), lambda bi, c: (bi, c, OFF_K // RET_QK)),
        pl.BlockSpec((1, chunk, RET_V), lambda bi, c: (bi, c, OFF_V // RET_V)),
        pl.BlockSpec((1, chunk, RET_V), lambda bi, c: (bi, c, OFF_G // RET_V)),
        pl.BlockSpec((1, RET_V), lambda bi, c: (0, 0)),
    ]
    args = [proj3, proj3, proj3, proj3, norm_w]
    if has_init:
        in_specs.append(state_block)
        args.append(s0)
    state_bytes = RET_HEADS * RET_DK * RET_DV * 4
    est = (2 * chunk * (2 * RET_QK + 3 * RET_V) * 2 + (4 if has_init else 2) * state_bytes
           + 8 * chunk * max(chunk, RET_DV) * 4 + RET_HEADS * chunk * chunk * 4)
    return pl.pallas_call(
        functools.partial(_ret_kernel, chunk=chunk, has_init=has_init),
        grid=(b, seq // chunk),
        in_specs=in_specs,
        out_specs=[pl.BlockSpec((1, chunk, RET_V), lambda bi, c: (bi, c, 0)), state_block],
        out_shape=[jax.ShapeDtypeStruct((b, seq, RET_V), BF16),
                   jax.ShapeDtypeStruct((b, RET_HEADS, RET_DK, RET_DV), F32)],
        scratch_shapes=[pltpu.VMEM((RET_HEADS, chunk, chunk), F32)],
        compiler_params=pltpu.CompilerParams(
            dimension_semantics=("arbitrary", "arbitrary"), vmem_limit_bytes=_vmem_limit(est)),
        name="retention",
    )(*args)


def _expand_table():
    e = np.zeros((SSM_GROUPS, 2 * DT_PAD, SSM_GW), np.float32)
    for g in range(SSM_GROUPS):
        for col in range(SSM_GW):
            head = g * SSM_HPG + col // SSM_HEAD_DIM
            for piece in range(3):
                e[g, piece * SSM_HEADS + head, col] = 1.0
    return jnp.asarray(e, BF16)


def _ssd_kernel(*refs, chunk, has_init):
    if has_init:
        (z_ref, x_ref, b_ref, c_ref, dt_ref, dtt_ref, cw_ref, cb_ref, dtb_ref, dtbt_ref, al_ref,
         alt_ref, dsk_ref, nw_ref, ex_ref, h0_ref, cs0_ref, y_ref, hout_ref, csout_ref,
         ht_ref, xc_ref, fix_ref, ext_ref) = refs
    else:
        (z_ref, x_ref, b_ref, c_ref, dt_ref, dtt_ref, cw_ref, cb_ref, dtb_ref, dtbt_ref, al_ref,
         alt_ref, dsk_ref, nw_ref, ex_ref, y_ref, hout_ref, csout_ref,
         ht_ref, xc_ref) = refs
    c = pl.program_id(1)
    taps = SSM_CONV - 1
    kk = CONV_TAIL + chunk

    @pl.when(c == 0)
    def _():
        for g in range(SSM_GROUPS):
            if has_init:
                blk = h0_ref[0, g * SSM_HPG:(g + 1) * SSM_HPG].reshape(SSM_GW, SSM_STATE)
                ht_ref[g] = blk.T
            else:
                ht_ref[g] = jnp.zeros((SSM_STATE, SSM_GW), F32)
        xc_ref[0:CONV_TAIL, :] = jnp.zeros((CONV_TAIL, SSM_XBC), BF16)
        if has_init:
            ext_ref[...] = jnp.zeros_like(ext_ref)
            ext_ref[SUBLANES - taps:SUBLANES, :] = cs0_ref[0]
            fix = jnp.zeros((SUBLANES, SSM_XBC), F32)
            for i in range(taps):
                fix = fix + ext_ref[SUBLANES - taps + i:2 * SUBLANES - taps + i, :] * cw_ref[i:i + 1, :]
            fix_ref[...] = fix

    if has_init:
        @pl.when(c == 1)
        def _():
            fix_ref[...] = jnp.zeros_like(fix_ref)

    xc_ref[CONV_TAIL:kk, 0:SSM_INNER] = x_ref[0]
    xc_ref[CONV_TAIL:kk, SSM_INNER:SSM_INNER + SSM_BC] = b_ref[0]
    xc_ref[CONV_TAIL:kk, SSM_INNER + SSM_BC:SSM_XBC] = c_ref[0]

    si = lax.broadcasted_iota(jnp.int32, (taps * chunk, kk), 0)
    sj = lax.broadcasted_iota(jnp.int32, (taps * chunk, kk), 1)
    tap = si >> (chunk.bit_length() - 1)
    shift = jnp.where(sj == (si & (chunk - 1)) + tap + (CONV_TAIL - taps), 1.0, 0.0).astype(BF16)

    def conv(cur, lo, width):
        prev = jnp.dot(shift, xc_ref[:, lo:lo + width], preferred_element_type=F32)
        acc = cb_ref[:, lo:lo + width] + cur.astype(F32) * cw_ref[taps:taps + 1, lo:lo + width]
        for i in range(taps):
            acc = acc + prev[i * chunk:(i + 1) * chunk] * cw_ref[i:i + 1, lo:lo + width]
        if has_init:
            acc = jnp.concatenate([acc[:SUBLANES] + fix_ref[:, lo:lo + width], acc[SUBLANES:]], axis=0)
        return _silu(acc)

    a_row = -jnp.exp(al_ref[...]) * LOG2E
    a_col = -jnp.exp(alt_ref[...]) * LOG2E
    dt = _softplus(dt_ref[0] + dtb_ref[...])
    dtt = _softplus(dtt_ref[0] + dtbt_ref[...])
    ii = lax.broadcasted_iota(jnp.int32, (chunk, chunk), 0)
    jj = lax.broadcasted_iota(jnp.int32, (chunk, chunk), 1)
    tri = ii >= jj
    lower = tri.astype(BF16)
    upper = (ii <= jj).astype(BF16)
    cum = sum(jnp.dot(lower, p, preferred_element_type=F32) for p in _split3(dt * a_row))
    cumt = sum(jnp.dot(p, upper, preferred_element_type=F32) for p in _split3(dtt * a_col))

    first_half = lax.broadcasted_iota(jnp.int32, (chunk, LANES), 1) < SSM_HEADS

    def pieces(v):
        hi, mid, lo = _split3(v)
        return jnp.concatenate([jnp.where(first_half, hi, mid), lo], axis=1)

    dt_p = pieces(dt)
    cum_p = pieces(cum)
    lane_head = lax.broadcasted_iota(jnp.int32, (chunk, LANES), 1) < SSM_HEAD_DIM

    def stage_a(g):
        xs = conv(x_ref[0, :, g * SSM_GW:(g + 1) * SSM_GW], g * SSM_GW, SSM_GW)
        bm = conv(b_ref[0, :, g * SSM_STATE:(g + 1) * SSM_STATE], SSM_INNER + g * SSM_STATE, SSM_STATE)
        cm = conv(c_ref[0, :, g * SSM_STATE:(g + 1) * SSM_STATE], SSM_INNER + SSM_BC + g * SSM_STATE, SSM_STATE)
        cbm = jnp.where(tri, _dot_nt(cm, bm), 0.0)
        dt_e = jnp.dot(dt_p, ex_ref[g], preferred_element_type=F32)
        cum_e = jnp.dot(cum_p, ex_ref[g], preferred_element_type=F32)
        e_all = jnp.exp2(cum_e)
        xdt = xs * dt_e
        xd = xdt * jnp.exp2(cum_e[chunk - 1:chunk, :] - cum_e)
        return xs, bm, cm, cbm, e_all, xd, xdt.astype(BF16)

    nxt = stage_a(0)
    for g in range(SSM_GROUPS):
        xs, bm, cm, cbm, e_all, xd, xdt16 = nxt
        if g + 1 < SSM_GROUPS:
            nxt = stage_a(g + 1)
        y_parts = []
        for s in range(SSM_GW // LANES):
            rhs = xdt16[:, s * LANES:(s + 1) * LANES]
            prod = []
            for j in (g * SSM_HPG + 2 * s, g * SSM_HPG + 2 * s + 1):
                seg = (jnp.broadcast_to(cum[:, j:j + 1], (chunk, chunk))
                       - jnp.broadcast_to(cumt[j:j + 1, :], (chunk, chunk)))
                m = cbm * jnp.exp2(jnp.minimum(seg, 0.0))
                prod.append(jnp.dot(m.astype(BF16), rhs, preferred_element_type=F32))
            y_parts.append(jnp.where(lane_head, prod[0], prod[1]))
        htg = ht_ref[g]
        y = (jnp.concatenate(y_parts, axis=1) + _dot(cm, htg) * e_all
             + xs * dsk_ref[:, g * SSM_GW:(g + 1) * SSM_GW])
        ht_ref[g] = htg * e_all[chunk - 1:chunk, :] + _dot_tn(bm, xd)
        y = y * _silu(z_ref[0, :, g * SSM_GW:(g + 1) * SSM_GW].astype(F32))
        y = _rms(y, nw_ref[:, g * SSM_GW:(g + 1) * SSM_GW])
        y_ref[0, :, g * SSM_GW:(g + 1) * SSM_GW] = y.astype(BF16)

    xc_ref[0:CONV_TAIL, :] = xc_ref[chunk:kk, :]

    @pl.when(c == pl.num_programs(1) - 1)
    def _():
        csout_ref[0] = xc_ref[chunk:kk, :].astype(F32)[CONV_TAIL - taps:CONV_TAIL, :]
        for g in range(SSM_GROUPS):
            hout_ref[0, g * SSM_HPG:(g + 1) * SSM_HPG] = ht_ref[g].T.reshape(SSM_HPG, SSM_HEAD_DIM, SSM_STATE)


def _ssd(proj3, dt3, dtt3, conv_w, conv_b, dt_bias, a_log, d_skip, norm_w, h0, cs0, *, chunk):
    b, seq, _ = proj3.shape
    has_init = h0 is not None
    assert chunk % SUBLANES == 0 and chunk >= SUBLANES
    dtb = jnp.tile(dt_bias.astype(F32), 2)
    al = jnp.tile(a_log.astype(F32), 2)
    dsk = jnp.repeat(d_skip.astype(F32), SSM_HEAD_DIM).reshape(1, SSM_INNER)
    const = lambda *shape: pl.BlockSpec(shape, lambda bi, c: (0,) * len(shape))
    in_specs = [
        pl.BlockSpec((1, chunk, SSM_INNER), lambda bi, c: (bi, c, OFF_Z // SSM_INNER)),
        pl.BlockSpec((1, chunk, SSM_INNER), lambda bi, c: (bi, c, OFF_X // SSM_INNER)),
        pl.BlockSpec((1, chunk, SSM_BC), lambda bi, c: (bi, c, OFF_B // SSM_BC)),
        pl.BlockSpec((1, chunk, SSM_BC), lambda bi, c: (bi, c, OFF_C // SSM_BC)),
        pl.BlockSpec((1, chunk, DT_PAD), lambda bi, c: (bi, c, 0)),
        pl.BlockSpec((1, DT_PAD, chunk), lambda bi, c: (bi, 0, c)),
        const(SSM_CONV, SSM_XBC), const(1, SSM_XBC), const(1, DT_PAD), const(DT_PAD, 1),
        const(1, DT_PAD), const(DT_PAD, 1), const(1, SSM_INNER), const(1, SSM_INNER),
        const(SSM_GROUPS, 2 * DT_PAD, SSM_GW),
    ]
    args = [proj3, proj3, proj3, proj3, dt3, dtt3, conv_w.astype(F32), conv_b.astype(F32).reshape(1, SSM_XBC),
            dtb.reshape(1, DT_PAD), dtb.reshape(DT_PAD, 1), al.reshape(1, DT_PAD), al.reshape(DT_PAD, 1),
            dsk, norm_w.astype(F32).reshape(1, SSM_INNER), _expand_table()]
    h_block = pl.BlockSpec((1, SSM_HEADS, SSM_HEAD_DIM, SSM_STATE), lambda bi, c: (bi, 0, 0, 0))
    cs_block = pl.BlockSpec((1, SSM_CONV - 1, SSM_XBC), lambda bi, c: (bi, 0, 0))
    scratch = [pltpu.VMEM((SSM_GROUPS, SSM_STATE, SSM_GW), F32),
               pltpu.VMEM((CONV_TAIL + chunk, SSM_XBC), BF16)]
    if has_init:
        in_specs += [h_block, cs_block]
        args += [h0, cs0]
        scratch += [pltpu.VMEM((SUBLANES, SSM_XBC), F32), pltpu.VMEM((2 * SUBLANES, SSM_XBC), F32)]
    state_bytes = SSM_HEADS * SSM_HEAD_DIM * SSM_STATE * 4
    est = (2 * chunk * (3 * SSM_INNER + 2 * SSM_BC) * 2 + (5 if has_init else 3) * state_bytes
           + (chunk + CONV_TAIL) * SSM_XBC * 2 + 32 * chunk * max(chunk, SSM_GW) * 4 + 8 * 1024 * 1024)
    return pl.pallas_call(
        functools.partial(_ssd_kernel, chunk=chunk, has_init=has_init),
        grid=(b, seq // chunk),
        in_specs=in_specs,
        out_specs=[pl.BlockSpec((1, chunk, SSM_INNER), lambda bi, c: (bi, c, 0)), h_block, cs_block],
        out_shape=[jax.ShapeDtypeStruct((b, seq, SSM_INNER), BF16),
                   jax.ShapeDtypeStruct((b, SSM_HEADS, SSM_HEAD_DIM, SSM_STATE), F32),
                   jax.ShapeDtypeStruct((b, SSM_CONV - 1, SSM_XBC), F32)],
        scratch_shapes=scratch,
        compiler_params=pltpu.CompilerParams(
            dimension_semantics=("parallel", "arbitrary"), vmem_limit_bytes=_vmem_limit(est)),
        name="ssd",
    )(*args)


def _merge_kernel(ar_ref, as_ref, wr_ref, ws_ref, ga_ref, gb_ref, ba_ref, bb_ref, o_ref, *, sub):
    for lo in range(0, o_ref.shape[0], sub):
        rows = slice(lo, lo + sub)
        yr = jnp.dot(ar_ref[rows, :], wr_ref[...], preferred_element_type=F32)
        ys = jnp.dot(as_ref[rows, :], ws_ref[...], preferred_element_type=F32)
        ga = jax.nn.sigmoid(ga_ref[rows, :].astype(F32) + ba_ref[...])
        gb = jax.nn.sigmoid(gb_ref[rows, :].astype(F32) + bb_ref[...])
        o_ref[rows, :] = (ga * yr + gb * ys).astype(BF16)


def _merge(a_ret, a_ssm, w_ret, w_ssm, gates, b_gate):
    rows = a_ret.shape[0]
    tm = min(1024, rows)
    tn = 512
    nb = D_MODEL // tn
    est = 2 * 2 * tm * RET_V * 2 + 2 * 2 * RET_V * tn * 2 + 8 * tm * tn * 4
    return pl.pallas_call(
        functools.partial(_merge_kernel, sub=min(512, tm)),
        grid=(rows // tm, nb),
        in_specs=[
            pl.BlockSpec((tm, RET_V), lambda i, j: (i, 0)),
            pl.BlockSpec((tm, SSM_INNER), lambda i, j: (i, 0)),
            pl.BlockSpec((RET_V, tn), lambda i, j: (0, j)),
            pl.BlockSpec((SSM_INNER, tn), lambda i, j: (0, j)),
            pl.BlockSpec((tm, tn), lambda i, j: (i, j)),
            pl.BlockSpec((tm, tn), lambda i, j: (i, nb + j)),
            pl.BlockSpec((1, tn), lambda i, j: (0, j)),
            pl.BlockSpec((1, tn), lambda i, j: (0, nb + j)),
        ],
        out_specs=pl.BlockSpec((tm, tn), lambda i, j: (i, j)),
        out_shape=jax.ShapeDtypeStruct((rows, D_MODEL), BF16),
        compiler_params=pltpu.CompilerParams(
            dimension_semantics=("parallel", "arbitrary"), vmem_limit_bytes=_vmem_limit(est)),
        name="merge",
    )(a_ret, a_ssm, w_ret, w_ssm, gates, gates, b_gate, b_gate)


def _outproj_kernel(m_ref, w_ref, h_ref, o_ref):
    o_ref[...] = h_ref[...] + jnp.dot(m_ref[...], w_ref[...], preferred_element_type=F32)


def _outproj(m, w_out, h):
    rows = m.shape[0]
    tm = min(1024, rows)
    tn = 1024
    est = 2 * tm * D_MODEL * 2 + 2 * D_MODEL * tn * 2 + 5 * tm * tn * 4
    return pl.pallas_call(
        _outproj_kernel,
        grid=(rows // tm, D_MODEL // tn),
        in_specs=[
            pl.BlockSpec((tm, D_MODEL), lambda i, j: (i, 0)),
            pl.BlockSpec((D_MODEL, tn), lambda i, j: (0, j)),
            pl.BlockSpec((tm, tn), lambda i, j: (i, j)),
        ],
        out_specs=pl.BlockSpec((tm, tn), lambda i, j: (i, j)),
        out_shape=jax.ShapeDtypeStruct((rows, D_MODEL), F32),
        compiler_params=pltpu.CompilerParams(
            dimension_semantics=("parallel", "arbitrary"), vmem_limit_bytes=_vmem_limit(est)),
        name="outproj",
    )(m, w_out, h)


def _layer(x, s_ret, s_ssm, s_conv, pos0, p, *, ret_chunk, ssd_chunk):
    b, seq, _ = x.shape
    rows = b * seq
    h, n = _ffn(x.reshape(rows, D_MODEL), p["norm_ffn1"], p["ffn1_w_gate"], p["ffn1_w_up"], p["ffn1_w_down"],
                p["norm_mix"], tail="emit_norm")
    proj = _proj(n, p["w_all"], p["inv_freq"], seq=seq, pos0=pos0)
    gates, dt, dtt3 = _gates(n, p["w_gates"], p["w_dt"], p["w_dtt"], batch=b, seq=seq)
    proj3 = proj.reshape(b, seq, PROJ_W)
    dt3 = dt.reshape(b, seq, DT_PAD)
    a_ret, ret_new = _retention(proj3, p["ret_norm_w"], s_ret, chunk=min(ret_chunk, seq))
    a_ssm, ssm_new, conv_new = _ssd(proj3, dt3, dtt3, p["conv_w"], p["conv_b"], p["dt_bias"], p["a_log"],
                                    p["d_skip"], p["ssm_norm_w"], s_ssm, s_conv, chunk=min(ssd_chunk, seq))
    m = _merge(a_ret.reshape(rows, RET_V), a_ssm.reshape(rows, SSM_INNER), p["w_out_ret"], p["w_out_ssm"],
               gates, p["b_gate"])
    h2 = _outproj(m, p["w_out"], h)
    y, = _ffn(h2, p["norm_ffn2"], p["ffn2_w_gate"], p["ffn2_w_up"], p["ffn2_w_down"], p["norm_final"],
              tail="final_norm")
    return y.reshape(b, seq, D_MODEL), ret_new, ssm_new, conv_new


def kernel(x_prompt, x_sample, state_ret, state_ssm, state_conv, norm_ffn1, ffn1_w_gate, ffn1_w_up, ffn1_w_down, norm_mix, w_in, b_gate, ret_norm_w, w_out_ret, conv_w, conv_b, dt_bias, a_log, d_skip, ssm_norm_w, w_out_ssm, w_out, norm_ffn2, ffn2_w_gate, ffn2_w_up, ffn2_w_down, norm_final):
    row = lambda v: v.astype(F32).reshape(1, -1)
    dt_lo = 2 * RET_QK + 2 * RET_V + SSM_INNER + SSM_XBC
    w_dt = jnp.tile(w_in[:, dt_lo:dt_lo + SSM_HEADS], (1, 2)).astype(BF16)
    p = dict(
        norm_ffn1=row(norm_ffn1), norm_mix=row(norm_mix), norm_ffn2=row(norm_ffn2), norm_final=row(norm_final),
        ffn1_w_gate=ffn1_w_gate.astype(BF16), ffn1_w_up=ffn1_w_up.astype(BF16), ffn1_w_down=ffn1_w_down.astype(BF16),
        ffn2_w_gate=ffn2_w_gate.astype(BF16), ffn2_w_up=ffn2_w_up.astype(BF16), ffn2_w_down=ffn2_w_down.astype(BF16),
        w_all=w_in.astype(BF16), w_gates=w_in[:, dt_lo + SSM_HEADS:].astype(BF16),
        w_dt=w_dt, w_dtt=w_dt.T,
        inv_freq=(ROPE_BASE ** (-jnp.arange(ROPE_HALF, dtype=F32) / ROPE_HALF)).reshape(1, ROPE_HALF),
        b_gate=row(b_gate), ret_norm_w=row(ret_norm_w),
        w_out_ret=w_out_ret.astype(BF16), w_out_ssm=w_out_ssm.astype(BF16), w_out=w_out.astype(BF16),
        conv_w=conv_w, conv_b=conv_b, dt_bias=dt_bias, a_log=a_log, d_skip=d_skip, ssm_norm_w=ssm_norm_w,
    )
    y_p, ret_p, ssm_p, conv_p = _layer(x_prompt, None, None, None, 0, p, ret_chunk=256, ssd_chunk=128)
    y_s, ret_s, ssm_s, conv_s = _layer(x_sample, state_ret.astype(F32), state_ssm.astype(F32),
                                       state_conv.astype(F32), PAST_LEN, p, ret_chunk=256, ssd_chunk=128)
    return (y_p, y_s, ret_p, ssm_p, conv_p, ret_s, ssm_s, conv_s)
```

```python
import functools
import math

import numpy as np
import jax
import jax.numpy as jnp
from jax import lax
from jax.experimental import pallas as pl
from jax.experimental.pallas import tpu as pltpu

F32 = jnp.float32
BF16 = jnp.bfloat16

D_MODEL = 2048
PAST_LEN = 1024
RET_HEADS = 8
RET_DK = D_MODEL // RET_HEADS
RET_DV = 2 * D_MODEL // RET_HEADS
RET_QK = RET_HEADS * RET_DK
RET_V = RET_HEADS * RET_DV
ROPE_BASE = 10000.0
ROPE_HALF = RET_DK // 2
SSM_INNER = 2 * D_MODEL
SSM_HEAD_DIM = 64
SSM_HEADS = SSM_INNER // SSM_HEAD_DIM
SSM_GROUPS = 8
SSM_HPG = SSM_HEADS // SSM_GROUPS
SSM_STATE = 128
SSM_CONV = 4
SSM_BC = SSM_GROUPS * SSM_STATE
SSM_XBC = SSM_INNER + 2 * SSM_BC
SSM_GW = SSM_INNER // SSM_GROUPS
D_FF = 5632
EPS = 1e-6
LOG2E = math.log2(math.e)

LANES = 128
SUBLANES = 8
BF16_SUBLANES = 16
V7X_VMEM_BYTES = 64 * 1024 * 1024

OFF_Q = 0
OFF_K = OFF_Q + RET_QK
OFF_V = OFF_K + RET_QK
OFF_G = OFF_V + RET_V
OFF_Z = OFF_G + RET_V
OFF_X = OFF_Z + SSM_INNER
OFF_B = OFF_X + SSM_INNER
OFF_C = OFF_B + SSM_BC
PROJ_W = OFF_C + SSM_BC
DT_PAD = LANES
assert DT_PAD == 2 * SSM_HEADS
CONV_TAIL = BF16_SUBLANES
FFN_TILES = {"emit_norm": (512, 512), "final_norm": (512, 512)}


def _vmem_limit(nbytes):
    return int(min(V7X_VMEM_BYTES - 4 * 1024 * 1024, max(nbytes, 16 * 1024 * 1024)))


def _dot(a, b):
    return jnp.dot(a.astype(BF16), b.astype(BF16), preferred_element_type=F32)


def _dot_nt(a, b):
    return lax.dot_general(a.astype(BF16), b.astype(BF16), (((1,), (1,)), ((), ())),
                           preferred_element_type=F32)


def _dot_tn(a, b):
    t = a.shape[0]
    pad = (-t) % LANES
    a = a.astype(F32)
    b = b.astype(BF16)
    if pad:
        a = jnp.concatenate([a, jnp.zeros((pad, a.shape[1]), F32)], axis=0)
        b = jnp.concatenate([b, jnp.zeros((pad, b.shape[1]), BF16)], axis=0)
    return jnp.dot(a.T.astype(BF16), b, preferred_element_type=F32)


def _rms(x, w):
    return x * lax.rsqrt(jnp.mean(x * x, axis=-1, keepdims=True) + EPS) * w


def _sigmoid(x):
    return 0.5 * jnp.tanh(0.5 * x) + 0.5


def _silu(x):
    h = 0.5 * x
    return h * jnp.tanh(h) + h


def _softplus(x):
    return jnp.maximum(x, 0.0) + jnp.log1p(jnp.exp(-jnp.abs(x)))


def _split3(x):
    hi = x.astype(BF16)
    r = x - hi.astype(F32)
    mid = r.astype(BF16)
    lo = (r - mid.astype(F32)).astype(BF16)
    return hi, mid, lo


def _ffn_kernel(x_ref, nw_ref, wg_ref, wu_ref, wd_ref, fw_ref, *refs, tail, sub):
    if tail == "emit_norm":
        o_ref, no_ref, n_ref = refs
    else:
        o_ref, n_ref = refs
    j = pl.program_id(1)

    @pl.when(j == 0)
    def _():
        n_ref[...] = _rms(x_ref[...], nw_ref[...]).astype(BF16)
        o_ref[...] = jnp.zeros_like(o_ref)

    n = n_ref[...]
    acc = o_ref[...]
    for lo in range(0, wg_ref.shape[1], sub):
        g = jnp.dot(n, wg_ref[:, lo:lo + sub], preferred_element_type=F32)
        u = jnp.dot(n, wu_ref[:, lo:lo + sub], preferred_element_type=F32)
        a = (_silu(g) * u).astype(BF16)
        acc = acc + jnp.dot(a, wd_ref[lo:lo + sub, :], preferred_element_type=F32)
    o_ref[...] = acc

    @pl.when(j == pl.num_programs(1) - 1)
    def _():
        h = x_ref[...] + 0.5 * o_ref[...]
        if tail == "emit_norm":
            o_ref[...] = h
            no_ref[...] = _rms(h, fw_ref[...]).astype(BF16)
        else:
            o_ref[...] = _rms(h, fw_ref[...])


def _ffn(x, norm_w, wg, wu, wd, tail_w, *, tail):
    rows = x.shape[0]
    tm, tf = FFN_TILES[tail]
    tm = min(tm, rows)
    assert D_FF % tf == 0
    row_block = pl.BlockSpec((tm, D_MODEL), lambda i, j: (i, 0))
    out_specs = [row_block]
    out_shape = [jax.ShapeDtypeStruct((rows, D_MODEL), F32)]
    if tail == "emit_norm":
        out_specs.append(row_block)
        out_shape.append(jax.ShapeDtypeStruct((rows, D_MODEL), BF16))
    sub = min(256, tf)
    est = (2 * 2 * tm * D_MODEL * 4 + tm * D_MODEL * 2 + 2 * 3 * D_MODEL * tf * 2
           + 2 * tm * D_MODEL * 4 + 3 * tm * sub * 4 + (2 * tm * D_MODEL * 2 if tail == "emit_norm" else 0))
    return pl.pallas_call(
        functools.partial(_ffn_kernel, tail=tail, sub=sub),
        grid=(rows // tm, D_FF // tf),
        in_specs=[
            row_block,
            pl.BlockSpec((1, D_MODEL), lambda i, j: (0, 0)),
            pl.BlockSpec((D_MODEL, tf), lambda i, j: (0, j)),
            pl.BlockSpec((D_MODEL, tf), lambda i, j: (0, j)),
            pl.BlockSpec((tf, D_MODEL), lambda i, j: (j, 0)),
            pl.BlockSpec((1, D_MODEL), lambda i, j: (0, 0)),
        ],
        out_specs=out_specs,
        out_shape=out_shape,
        scratch_shapes=[pltpu.VMEM((tm, D_MODEL), BF16)],
        compiler_params=pltpu.CompilerParams(
            dimension_semantics=("parallel", "arbitrary"), vmem_limit_bytes=_vmem_limit(est)),
        name="ffn" if tail == "emit_norm" else "ffn_final",
    )(x, norm_w, wg, wu, wd, tail_w)


def _rope_kernel(inv_ref, cos_ref, sin_ref, *, seq, pos0):
    tr = cos_ref.shape[0]
    row = pl.program_id(0) * tr + lax.broadcasted_iota(jnp.int32, (tr, 1), 0)
    ang = ((row & (seq - 1)) + pos0).astype(F32) * inv_ref[...]
    cos_ref[...] = jnp.cos(ang)
    sin_ref[...] = jnp.sin(ang)


def _rope_tables(inv_freq, *, length, seq, pos0):
    tr = min(512, length)
    spec = pl.BlockSpec((tr, ROPE_HALF), lambda i: (i, 0))
    return pl.pallas_call(
        functools.partial(_rope_kernel, seq=seq, pos0=pos0),
        grid=(length // tr,),
        in_specs=[pl.BlockSpec((1, ROPE_HALF), lambda i: (0, 0))],
        out_specs=[spec, spec],
        out_shape=[jax.ShapeDtypeStruct((length, ROPE_HALF), F32)] * 2,
        name="rope",
    )(inv_freq)


def _proj_kernel(n_ref, w_ref, cos_ref, sin_ref, o_ref, *, tn):
    j = pl.program_id(1)
    n_q = RET_QK // tn

    @pl.when(j < 2 * n_q)
    def _():
        acc = jnp.dot(n_ref[...], w_ref[...], preferred_element_type=F32)
        cos = cos_ref[...]
        sin = sin_ref[...]
        scale = jnp.where(j < n_q, 1.0, RET_DK ** -0.5).astype(F32)
        for lo in range(0, tn, RET_DK):
            x1 = acc[:, lo:lo + ROPE_HALF]
            x2 = acc[:, lo + ROPE_HALF:lo + RET_DK]
            o_ref[:, lo:lo + ROPE_HALF] = ((x1 * cos - x2 * sin) * scale).astype(BF16)
            o_ref[:, lo + ROPE_HALF:lo + RET_DK] = ((x1 * sin + x2 * cos) * scale).astype(BF16)

    @pl.when(j >= 2 * n_q)
    def _():
        o_ref[...] = jnp.dot(n_ref[...], w_ref[...], preferred_element_type=F32).astype(BF16)


def _proj(n, w_all, inv_freq, *, seq, pos0):
    rows = n.shape[0]
    tm = min(1024, rows)
    tn = 2048
    assert seq & (seq - 1) == 0 and RET_QK % tn == 0 and PROJ_W % tn == 0
    table_len = max(seq, tm)
    cos, sin = _rope_tables(inv_freq, length=table_len, seq=seq, pos0=pos0)
    n_tab = table_len // tm
    est = (2 * tm * D_MODEL * 2 + 2 * D_MODEL * tn * 2 + 2 * tm * tn * 2 + 2 * tm * tn * 4
           + 4 * tm * LANES * 4)
    return pl.pallas_call(
        functools.partial(_proj_kernel, tn=tn),
        grid=(rows // tm, PROJ_W // tn),
        in_specs=[
            pl.BlockSpec((tm, D_MODEL), lambda i, j: (i, 0)),
            pl.BlockSpec((D_MODEL, tn), lambda i, j: (0, j)),
            pl.BlockSpec((tm, ROPE_HALF), lambda i, j: (i % n_tab, 0)),
            pl.BlockSpec((tm, ROPE_HALF), lambda i, j: (i % n_tab, 0)),
        ],
        out_specs=pl.BlockSpec((tm, tn), lambda i, j: (i, j)),
        out_shape=jax.ShapeDtypeStruct((rows, PROJ_W), BF16),
        compiler_params=pltpu.CompilerParams(
            dimension_semantics=("parallel", "arbitrary"), vmem_limit_bytes=_vmem_limit(est)),
        name="proj",
    )(n, w_all, cos, sin)


def _gates_kernel(n_ref, w_ref, wdt_ref, wdtt_ref, o_ref, dt_ref, dtt_ref, *, batched_dtt):
    @pl.when(pl.program_id(1) == 0)
    def _():
        n = n_ref[...]
        dt_ref[...] = jnp.dot(n, wdt_ref[...], preferred_element_type=F32)
        dtt = _dot_nt(wdtt_ref[...], n)
        if batched_dtt:
            dtt_ref[0] = dtt
        else:
            dtt_ref[...] = dtt

    o_ref[...] = jnp.dot(n_ref[...], w_ref[...], preferred_element_type=F32).astype(BF16)


def _gates(n, w_gates, w_dt, w_dtt, *, batch, seq):
    rows = n.shape[0]
    tm = min(1024, rows)
    tn = 2048
    batched_dtt = tm <= seq
    if batched_dtt:
        per_seq = seq // tm
        dtt_spec = pl.BlockSpec((1, DT_PAD, tm), lambda i, j: (i // per_seq, 0, i % per_seq))
        dtt_shape = jax.ShapeDtypeStruct((batch, DT_PAD, seq), F32)
    else:
        dtt_spec = pl.BlockSpec((DT_PAD, tm), lambda i, j: (0, i))
        dtt_shape = jax.ShapeDtypeStruct((DT_PAD, rows), F32)
    est = (2 * tm * D_MODEL * 2 + 2 * D_MODEL * tn * 2 + 2 * tm * tn * 2 + 2 * tm * tn * 4
           + 4 * D_MODEL * DT_PAD * 2 + 6 * tm * DT_PAD * 4)
    gates, dt, dtt = pl.pallas_call(
        functools.partial(_gates_kernel, batched_dtt=batched_dtt),
        grid=(rows // tm, 2 * D_MODEL // tn),
        in_specs=[
            pl.BlockSpec((tm, D_MODEL), lambda i, j: (i, 0)),
            pl.BlockSpec((D_MODEL, tn), lambda i, j: (0, j)),
            pl.BlockSpec((D_MODEL, DT_PAD), lambda i, j: (0, 0)),
            pl.BlockSpec((DT_PAD, D_MODEL), lambda i, j: (0, 0)),
        ],
        out_specs=[
            pl.BlockSpec((tm, tn), lambda i, j: (i, j)),
            pl.BlockSpec((tm, DT_PAD), lambda i, j: (i, 0)),
            dtt_spec,
        ],
        out_shape=[
            jax.ShapeDtypeStruct((rows, 2 * D_MODEL), BF16),
            jax.ShapeDtypeStruct((rows, DT_PAD), F32),
            dtt_shape,
        ],
        compiler_params=pltpu.CompilerParams(
            dimension_semantics=("parallel", "arbitrary"), vmem_limit_bytes=_vmem_limit(est)),
        name="gates",
    )(n, w_gates, w_dt, w_dtt)
    if not batched_dtt:
        dtt = dtt.reshape(DT_PAD, batch, seq).transpose(1, 0, 2)
    return gates, dt, dtt


def _ret_kernel(*refs, chunk, has_init):
    if has_init:
        q_ref, k_ref, v_ref, g_ref, nw_ref, s0_ref, o_ref, s_ref, mask_ref = refs
    else:
        q_ref, k_ref, v_ref, g_ref, nw_ref, o_ref, s_ref, mask_ref = refs
    c = pl.program_id(1)
    log_decay = [math.log1p(-(2.0 ** (-5.0 - hd))) for hd in range(RET_HEADS)]

    @pl.when((pl.program_id(0) == 0) & (c == 0))
    def _():
        ii = lax.broadcasted_iota(jnp.int32, (chunk, chunk), 0)
        jj = lax.broadcasted_iota(jnp.int32, (chunk, chunk), 1)
        causal = ii >= jj
        diff = jnp.where(causal, ii - jj, 0).astype(F32)
        for hd in range(RET_HEADS):
            mask_ref[hd] = jnp.where(causal, jnp.exp(diff * log_decay[hd]), 0.0)

    @pl.when(c == 0)
    def _():
        if has_init:
            s_ref[...] = s0_ref[...]
        else:
            s_ref[...] = jnp.zeros_like(s_ref)

    idx = lax.broadcasted_iota(jnp.int32, (chunk, 1), 0).astype(F32)
    for hd in range(RET_HEADS):
        lg = log_decay[hd]
        decay_mask = mask_ref[hd]
        inner_decay = jnp.exp((idx + 1.0) * lg)
        state_decay = jnp.exp((chunk - 1.0 - idx) * lg)
        chunk_decay = math.exp(chunk * lg)
        q = q_ref[0, :, hd * RET_DK:(hd + 1) * RET_DK]
        k = k_ref[0, :, hd * RET_DK:(hd + 1) * RET_DK]
        v = v_ref[0, :, hd * RET_DV:(hd + 1) * RET_DV]
        s = s_ref[0, hd]
        scores = _dot_nt(q, k) * decay_mask
        o = _dot(scores, v) + _dot(q, s) * inner_decay
        s_ref[0, hd] = s * chunk_decay + _dot_tn(k.astype(F32) * state_decay, v)
        mu = jnp.mean(o, axis=-1, keepdims=True)
        var = jnp.mean(jnp.square(o - mu), axis=-1, keepdims=True)
        on = (o - mu) * lax.rsqrt(var + EPS) * nw_ref[:, hd * RET_DV:(hd + 1) * RET_DV]
        gate = g_ref[0, :, hd * RET_DV:(hd + 1) * RET_DV].astype(F32)
        o_ref[0, :, hd * RET_DV:(hd + 1) * RET_DV] = (_silu(gate) * on).astype(BF16)


def _retention(proj3, norm_w, s0, *, chunk):
    b, seq, _ = proj3.shape
    has_init = s0 is not None
    state_block = pl.BlockSpec((1, RET_HEADS, RET_DK, RET_DV), lambda bi, c: (bi, 0, 0, 0))
    in_specs = [
        pl.BlockSpec((1, chunk, RET_QK), lambda bi, c: (bi, c, OFF_Q // RET_QK)),
        pl.BlockSpec((1, chunk, RET_QK), lambda bi, c: (bi, c, OFF_K // RET_QK)),
        pl.BlockSpec((1, chunk, RET_V), lambda bi, c: (bi, c, OFF_V // RET_V)),
        pl.BlockSpec((1, chunk, RET_V), lambda bi, c: (bi, c, OFF_G // RET_V)),
        pl.BlockSpec((1, RET_V), lambda bi, c: (0, 0)),
    ]
    args = [proj3, proj3, proj3, proj3, norm_w]
    if has_init:
        in_specs.append(state_block)
        args.append(s0)
    state_bytes = RET_HEADS * RET_DK * RET_DV * 4
    est = (2 * chunk * (2 * RET_QK + 3 * RET_V) * 2 + (4 if has_init else 2) * state_bytes
           + 8 * chunk * max(chunk, RET_DV) * 4 + RET_HEADS * chunk * chunk * 4)
    return pl.pallas_call(
        functools.partial(_ret_kernel, chunk=chunk, has_init=has_init),
        grid=(b, seq // chunk),
        in_specs=in_specs,
        out_specs=[pl.BlockSpec((1, chunk, RET_V), lambda bi, c: (bi, c, 0)), state_block],
        out_shape=[jax.ShapeDtypeStruct((b, seq, RET_V), BF16),
                   jax.ShapeDtypeStruct((b, RET_HEADS, RET_DK, RET_DV), F32)],
        scratch_shapes=[pltpu.VMEM((RET_HEADS, chunk, chunk), F32)],
        compiler_params=pltpu.CompilerParams(
            dimension_semantics=("arbitrary", "arbitrary"), vmem_limit_bytes=_vmem_limit(est)),
        name="retention",
    )(*args)


def _expand_table():
    e = np.zeros((SSM_GROUPS, 2 * DT_PAD, SSM_GW), np.float32)
    for g in range(SSM_GROUPS):
        for col in range(SSM_GW):
            head = g * SSM_HPG + col // SSM_HEAD_DIM
            for piece in range(3):
                e[g, piece * SSM_HEADS + head, col] = 1.0
    return jnp.asarray(e, BF16)


def _ssd_kernel(*refs, chunk, has_init):
    if has_init:
        (z_ref, x_ref, b_ref, c_ref, dt_ref, dtt_ref, cw_ref, cb_ref, dtb_ref, dtbt_ref, al_ref,
         alt_ref, dsk_ref, nw_ref, ex_ref, h0_ref, cs0_ref, y_ref, hout_ref, csout_ref,
         ht_ref, xc_ref, fix_ref, ext_ref) = refs
    else:
        (z_ref, x_ref, b_ref, c_ref, dt_ref, dtt_ref, cw_ref, cb_ref, dtb_ref, dtbt_ref, al_ref,
         alt_ref, dsk_ref, nw_ref, ex_ref, y_ref, hout_ref, csout_ref,
         ht_ref, xc_ref) = refs
    c = pl.program_id(1)
    taps = SSM_CONV - 1
    kk = CONV_TAIL + chunk

    @pl.when(c == 0)
    def _():
        for g in range(SSM_GROUPS):
            if has_init:
                blk = h0_ref[0, g * SSM_HPG:(g + 1) * SSM_HPG].reshape(SSM_GW, SSM_STATE)
                ht_ref[g] = blk.T
            else:
                ht_ref[g] = jnp.zeros((SSM_STATE, SSM_GW), F32)
        xc_ref[0:CONV_TAIL, :] = jnp.zeros((CONV_TAIL, SSM_XBC), BF16)
        if has_init:
            ext_ref[...] = jnp.zeros_like(ext_ref)
            ext_ref[SUBLANES - taps:SUBLANES, :] = cs0_ref[0]
            fix = jnp.zeros((SUBLANES, SSM_XBC), F32)
            for i in range(taps):
                fix = fix + ext_ref[SUBLANES - taps + i:2 * SUBLANES - taps + i, :] * cw_ref[i:i + 1, :]
            fix_ref[...] = fix

    if has_init:
        @pl.when(c == 1)
        def _():
            fix_ref[...] = jnp.zeros_like(fix_ref)

    xc_ref[CONV_TAIL:kk, 0:SSM_INNER] = x_ref[0]
    xc_ref[CONV_TAIL:kk, SSM_INNER:SSM_INNER + SSM_BC] = b_ref[0]
    xc_ref[CONV_TAIL:kk, SSM_INNER + SSM_BC:SSM_XBC] = c_ref[0]

    si = lax.broadcasted_iota(jnp.int32, (taps * chunk, kk), 0)
    sj = lax.broadcasted_iota(jnp.int32, (taps * chunk, kk), 1)
    tap = si >> (chunk.bit_length() - 1)
    shift = jnp.where(sj == (si & (chunk - 1)) + tap + (CONV_TAIL - taps), 1.0, 0.0).astype(BF16)

    def conv(cur, lo, width):
        prev = jnp.dot(shift, xc_ref[:, lo:lo + width], preferred_element_type=F32)
        half = cb_ref[:, lo:lo + width] + cur.astype(F32) * cw_ref[taps:taps + 1, lo:lo + width]
        for i in range(taps):
            half = half + prev[i * chunk:(i + 1) * chunk] * cw_ref[i:i + 1, lo:lo + width]
        if has_init:
            half = jnp.concatenate([half[:SUBLANES] + fix_ref[:, lo:lo + width], half[SUBLANES:]], axis=0)
        return half * jnp.tanh(half) + half

    a_row = -jnp.exp(al_ref[...]) * LOG2E
    a_col = -jnp.exp(alt_ref[...]) * LOG2E
    dt = _softplus(dt_ref[0] + dtb_ref[...])
    dtt = _softplus(dtt_ref[0] + dtbt_ref[...])
    ii = lax.broadcasted_iota(jnp.int32, (chunk, chunk), 0)
    jj = lax.broadcasted_iota(jnp.int32, (chunk, chunk), 1)
    tri = ii >= jj
    lower = tri.astype(BF16)
    upper = (ii <= jj).astype(BF16)
    cum = sum(jnp.dot(lower, p, preferred_element_type=F32) for p in _split3(dt * a_row))
    cumt = sum(jnp.dot(p, upper, preferred_element_type=F32) for p in _split3(dtt * a_col))

    first_half = lax.broadcasted_iota(jnp.int32, (chunk, LANES), 1) < SSM_HEADS

    def pieces(v):
        hi, mid, lo = _split3(v)
        return jnp.concatenate([jnp.where(first_half, hi, mid), lo], axis=1)

    dt_p = pieces(dt)
    cum_p = pieces(cum)
    lane_head = lax.broadcasted_iota(jnp.int32, (chunk, LANES), 1) < SSM_HEAD_DIM
    head_a = jnp.where(lane_head, 1.0, 0.0).astype(BF16)
    head_b = jnp.where(lane_head, 0.0, 1.0).astype(BF16)

    def stage_a(g):
        xs = conv(x_ref[0, :, g * SSM_GW:(g + 1) * SSM_GW], g * SSM_GW, SSM_GW)
        bm = conv(b_ref[0, :, g * SSM_STATE:(g + 1) * SSM_STATE], SSM_INNER + g * SSM_STATE, SSM_STATE)
        cm = conv(c_ref[0, :, g * SSM_STATE:(g + 1) * SSM_STATE], SSM_INNER + SSM_BC + g * SSM_STATE, SSM_STATE)
        cbm = jnp.where(tri, _dot_nt(cm, bm), 0.0)
        dt_e = jnp.dot(dt_p, ex_ref[g], preferred_element_type=F32)
        cum_e = jnp.dot(cum_p, ex_ref[g], preferred_element_type=F32)
        e_all = jnp.exp2(cum_e)
        xdt = xs * dt_e
        xd = xdt * jnp.exp2(cum_e[chunk - 1:chunk, :] - cum_e)
        return xs, bm, cm, cbm, e_all, xd, xdt.astype(BF16)

    nxt = stage_a(0)
    for g in range(SSM_GROUPS):
        xs, bm, cm, cbm, e_all, xd, xdt16 = nxt
        if g + 1 < SSM_GROUPS:
            nxt = stage_a(g + 1)
        y_parts = []
        for s in range(SSM_GW // LANES):
            rhs = xdt16[:, s * LANES:(s + 1) * LANES]
            ms = []
            for j in (g * SSM_HPG + 2 * s, g * SSM_HPG + 2 * s + 1):
                seg = (jnp.broadcast_to(cum[:, j:j + 1], (chunk, chunk))
                       - jnp.broadcast_to(cumt[j:j + 1, :], (chunk, chunk)))
                ms.append((cbm * jnp.exp2(jnp.minimum(seg, 0.0))).astype(BF16))
            if chunk % LANES == 0:
                rhs2 = jnp.concatenate([rhs * head_a, rhs * head_b], axis=0)
                y_parts.append(jnp.dot(jnp.concatenate(ms, axis=1), rhs2, preferred_element_type=F32))
            else:
                y_parts.append(jnp.dot(ms[0], rhs * head_a, preferred_element_type=F32)
                               + jnp.dot(ms[1], rhs * head_b, preferred_element_type=F32))
        htg = ht_ref[g]
        y = (jnp.concatenate(y_parts, axis=1) + _dot(cm, htg) * e_all
             + xs * dsk_ref[:, g * SSM_GW:(g + 1) * SSM_GW])
        ht_ref[g] = htg * e_all[chunk - 1:chunk, :] + _dot_tn(bm, xd)
        y = y * _silu(z_ref[0, :, g * SSM_GW:(g + 1) * SSM_GW].astype(F32))
        y = _rms(y, nw_ref[:, g * SSM_GW:(g + 1) * SSM_GW])
        y_ref[0, :, g * SSM_GW:(g + 1) * SSM_GW] = y.astype(BF16)

    xc_ref[0:CONV_TAIL, :] = xc_ref[chunk:kk, :]

    @pl.when(c == pl.num_programs(1) - 1)
    def _():
        csout_ref[0] = xc_ref[chunk:kk, :].astype(F32)[CONV_TAIL - taps:CONV_TAIL, :]
        for g in range(SSM_GROUPS):
            hout_ref[0, g * SSM_HPG:(g + 1) * SSM_HPG] = ht_ref[g].T.reshape(SSM_HPG, SSM_HEAD_DIM, SSM_STATE)


def _ssd(proj3, dt3, dtt3, conv_w, conv_b, dt_bias, a_log, d_skip, norm_w, h0, cs0, *, chunk):
    b, seq, _ = proj3.shape
    has_init = h0 is not None
    assert chunk % SUBLANES == 0 and chunk >= SUBLANES
    dtb = jnp.tile(dt_bias.astype(F32), 2)
    al = jnp.tile(a_log.astype(F32), 2)
    dsk = jnp.repeat(d_skip.astype(F32), SSM_HEAD_DIM).reshape(1, SSM_INNER)
    const = lambda *shape: pl.BlockSpec(shape, lambda bi, c: (0,) * len(shape))
    in_specs = [
        pl.BlockSpec((1, chunk, SSM_INNER), lambda bi, c: (bi, c, OFF_Z // SSM_INNER)),
        pl.BlockSpec((1, chunk, SSM_INNER), lambda bi, c: (bi, c, OFF_X // SSM_INNER)),
        pl.BlockSpec((1, chunk, SSM_BC), lambda bi, c: (bi, c, OFF_B // SSM_BC)),
        pl.BlockSpec((1, chunk, SSM_BC), lambda bi, c: (bi, c, OFF_C // SSM_BC)),
        pl.BlockSpec((1, chunk, DT_PAD), lambda bi, c: (bi, c, 0)),
        pl.BlockSpec((1, DT_PAD, chunk), lambda bi, c: (bi, 0, c)),
        const(SSM_CONV, SSM_XBC), const(1, SSM_XBC), const(1, DT_PAD), const(DT_PAD, 1),
        const(1, DT_PAD), const(DT_PAD, 1), const(1, SSM_INNER), const(1, SSM_INNER),
        const(SSM_GROUPS, 2 * DT_PAD, SSM_GW),
    ]
    args = [proj3, proj3, proj3, proj3, dt3, dtt3, 0.5 * conv_w.astype(F32),
            0.5 * conv_b.astype(F32).reshape(1, SSM_XBC),
            dtb.reshape(1, DT_PAD), dtb.reshape(DT_PAD, 1), al.reshape(1, DT_PAD), al.reshape(DT_PAD, 1),
            dsk, norm_w.astype(F32).reshape(1, SSM_INNER), _expand_table()]
    h_block = pl.BlockSpec((1, SSM_HEADS, SSM_HEAD_DIM, SSM_STATE), lambda bi, c: (bi, 0, 0, 0))
    cs_block = pl.BlockSpec((1, SSM_CONV - 1, SSM_XBC), lambda bi, c: (bi, 0, 0))
    scratch = [pltpu.VMEM((SSM_GROUPS, SSM_STATE, SSM_GW), F32),
               pltpu.VMEM((CONV_TAIL + chunk, SSM_XBC), BF16)]
    if has_init:
        in_specs += [h_block, cs_block]
        args += [h0, cs0]
        scratch += [pltpu.VMEM((SUBLANES, SSM_XBC), F32), pltpu.VMEM((2 * SUBLANES, SSM_XBC), F32)]
    state_bytes = SSM_HEADS * SSM_HEAD_DIM * SSM_STATE * 4
    est = (2 * chunk * (3 * SSM_INNER + 2 * SSM_BC) * 2 + (5 if has_init else 3) * state_bytes
           + (chunk + CONV_TAIL) * SSM_XBC * 2 + 32 * chunk * max(chunk, SSM_GW) * 4 + 8 * 1024 * 1024)
    return pl.pallas_call(
        functools.partial(_ssd_kernel, chunk=chunk, has_init=has_init),
        grid=(b, seq // chunk),
        in_specs=in_specs,
        out_specs=[pl.BlockSpec((1, chunk, SSM_INNER), lambda bi, c: (bi, c, 0)), h_block, cs_block],
        out_shape=[jax.ShapeDtypeStruct((b, seq, SSM_INNER), BF16),
                   jax.ShapeDtypeStruct((b, SSM_HEADS, SSM_HEAD_DIM, SSM_STATE), F32),
                   jax.ShapeDtypeStruct((b, SSM_CONV - 1, SSM_XBC), F32)],
        scratch_shapes=scratch,
        compiler_params=pltpu.CompilerParams(
            dimension_semantics=("parallel", "arbitrary"), vmem_limit_bytes=_vmem_limit(est)),
        name="ssd",
    )(*args)


def _merge_kernel(ar_ref, as_ref, wr_ref, ws_ref, ga_ref, gb_ref, ba_ref, bb_ref, o_ref, *, sub):
    for lo in range(0, o_ref.shape[0], sub):
        rows = slice(lo, lo + sub)
        yr = jnp.dot(ar_ref[rows, :], wr_ref[...], preferred_element_type=F32)
        ys = jnp.dot(as_ref[rows, :], ws_ref[...], preferred_element_type=F32)
        ga = _sigmoid(ga_ref[rows, :].astype(F32) + ba_ref[...])
        gb = _sigmoid(gb_ref[rows, :].astype(F32) + bb_ref[...])
        o_ref[rows, :] = (ga * yr + gb * ys).astype(BF16)


def _merge(a_ret, a_ssm, w_ret, w_ssm, gates, b_gate):
    rows = a_ret.shape[0]
    tm = min(1024, rows)
    tn = 512
    nb = D_MODEL // tn
    est = 2 * 2 * tm * RET_V * 2 + 2 * 2 * RET_V * tn * 2 + 8 * tm * tn * 4
    return pl.pallas_call(
        functools.partial(_merge_kernel, sub=min(512, tm)),
        grid=(rows // tm, nb),
        in_specs=[
            pl.BlockSpec((tm, RET_V), lambda i, j: (i, 0)),
            pl.BlockSpec((tm, SSM_INNER), lambda i, j: (i, 0)),
            pl.BlockSpec((RET_V, tn), lambda i, j: (0, j)),
            pl.BlockSpec((SSM_INNER, tn), lambda i, j: (0, j)),
            pl.BlockSpec((tm, tn), lambda i, j: (i, j)),
            pl.BlockSpec((tm, tn), lambda i, j: (i, nb + j)),
            pl.BlockSpec((1, tn), lambda i, j: (0, j)),
            pl.BlockSpec((1, tn), lambda i, j: (0, nb + j)),
        ],
        out_specs=pl.BlockSpec((tm, tn), lambda i, j: (i, j)),
        out_shape=jax.ShapeDtypeStruct((rows, D_MODEL), BF16),
        compiler_params=pltpu.CompilerParams(
            dimension_semantics=("parallel", "arbitrary"), vmem_limit_bytes=_vmem_limit(est)),
        name="merge",
    )(a_ret, a_ssm, w_ret, w_ssm, gates, gates, b_gate, b_gate)


def _outproj_kernel(m_ref, w_ref, h_ref, o_ref):
    o_ref[...] = h_ref[...] + jnp.dot(m_ref[...], w_ref[...], preferred_element_type=F32)


def _outproj(m, w_out, h):
    rows = m.shape[0]
    tm = min(1024, rows)
    tn = 1024
    est = 2 * tm * D_MODEL * 2 + 2 * D_MODEL * tn * 2 + 5 * tm * tn * 4
    return pl.pallas_call(
        _outproj_kernel,
        grid=(rows // tm, D_MODEL // tn),
        in_specs=[
            pl.BlockSpec((tm, D_MODEL), lambda i, j: (i, 0)),
            pl.BlockSpec((D_MODEL, tn), lambda i, j: (0, j)),
            pl.BlockSpec((tm, tn), lambda i, j: (i, j)),
        ],
        out_specs=pl.BlockSpec((tm, tn), lambda i, j: (i, j)),
        out_shape=jax.ShapeDtypeStruct((rows, D_MODEL), F32),
        compiler_params=pltpu.CompilerParams(
            dimension_semantics=("parallel", "arbitrary"), vmem_limit_bytes=_vmem_limit(est)),
        name="outproj",
    )(m, w_out, h)


def _layer(x, s_ret, s_ssm, s_conv, pos0, p, *, ret_chunk, ssd_chunk):
    b, seq, _ = x.shape
    rows = b * seq
    h, n = _ffn(x.reshape(rows, D_MODEL), p["norm_ffn1"], p["ffn1_w_gate"], p["ffn1_w_up"], p["ffn1_w_down"],
                p["norm_mix"], tail="emit_norm")
    proj = _proj(n, p["w_all"], p["inv_freq"], seq=seq, pos0=pos0)
    gates, dt, dtt3 = _gates(n, p["w_gates"], p["w_dt"], p["w_dtt"], batch=b, seq=seq)
    proj3 = proj.reshape(b, seq, PROJ_W)
    dt3 = dt.reshape(b, seq, DT_PAD)
    a_ret, ret_new = _retention(proj3, p["ret_norm_w"], s_ret, chunk=min(ret_chunk, seq))
    a_ssm, ssm_new, conv_new = _ssd(proj3, dt3, dtt3, p["conv_w"], p["conv_b"], p["dt_bias"], p["a_log"],
                                    p["d_skip"], p["ssm_norm_w"], s_ssm, s_conv, chunk=min(ssd_chunk, seq))
    m = _merge(a_ret.reshape(rows, RET_V), a_ssm.reshape(rows, SSM_INNER), p["w_out_ret"], p["w_out_ssm"],
               gates, p["b_gate"])
    h2 = _outproj(m, p["w_out"], h)
    y, = _ffn(h2, p["norm_ffn2"], p["ffn2_w_gate"], p["ffn2_w_up"], p["ffn2_w_down"], p["norm_final"],
              tail="final_norm")
    return y.reshape(b, seq, D_MODEL), ret_new, ssm_new, conv_new


def kernel(x_prompt, x_sample, state_ret, state_ssm, state_conv, norm_ffn1, ffn1_w_gate, ffn1_w_up, ffn1_w_down, norm_mix, w_in, b_gate, ret_norm_w, w_out_ret, conv_w, conv_b, dt_bias, a_log, d_skip, ssm_norm_w, w_out_ssm, w_out, norm_ffn2, ffn2_w_gate, ffn2_w_up, ffn2_w_down, norm_final):
    row = lambda v: v.astype(F32).reshape(1, -1)
    dt_lo = 2 * RET_QK + 2 * RET_V + SSM_INNER + SSM_XBC
    w_dt = jnp.tile(w_in[:, dt_lo:dt_lo + SSM_HEADS], (1, 2)).astype(BF16)
    p = dict(
        norm_ffn1=row(norm_ffn1), norm_mix=row(norm_mix), norm_ffn2=row(norm_ffn2), norm_final=row(norm_final),
        ffn1_w_gate=ffn1_w_gate.astype(BF16), ffn1_w_up=ffn1_w_up.astype(BF16), ffn1_w_down=ffn1_w_down.astype(BF16),
        ffn2_w_gate=ffn2_w_gate.astype(BF16), ffn2_w_up=ffn2_w_up.astype(BF16), ffn2_w_down=ffn2_w_down.astype(BF16),
        w_all=w_in[:, :PROJ_W].astype(BF16), w_gates=w_in[:, dt_lo + SSM_HEADS:].astype(BF16),
        w_dt=w_dt, w_dtt=w_dt.T,
        inv_freq=(ROPE_BASE ** (-jnp.arange(ROPE_HALF, dtype=F32) / ROPE_HALF)).reshape(1, ROPE_HALF),
        b_gate=row(b_gate), ret_norm_w=row(ret_norm_w),
        w_out_ret=w_out_ret.astype(BF16), w_out_ssm=w_out_ssm.astype(BF16), w_out=w_out.astype(BF16),
        conv_w=conv_w, conv_b=conv_b, dt_bias=dt_bias, a_log=a_log, d_skip=d_skip, ssm_norm_w=ssm_norm_w,
    )
    y_p, ret_p, ssm_p, conv_p = _layer(x_prompt, None, None, None, 0, p, ret_chunk=256, ssd_chunk=128)
    y_s, ret_s, ssm_s, conv_s = _layer(x_sample, state_ret.astype(F32), state_ssm.astype(F32),
                                       state_conv.astype(F32), PAST_LEN, p, ret_chunk=256, ssd_chunk=128)
    return (y_p, y_s, ret_p, ssm_p, conv_p, ret_s, ssm_s, conv_s)
```

```python
import functools
import math

import numpy as np
import jax
import jax.numpy as jnp
from jax import lax
from jax.experimental import pallas as pl
from jax.experimental.pallas import tpu as pltpu

F32 = jnp.float32
BF16 = jnp.bfloat16

D_MODEL = 2048
PAST_LEN = 1024
RET_HEADS = 8
RET_DK = D_MODEL // RET_HEADS
RET_DV = 2 * D_MODEL // RET_HEADS
RET_QK = RET_HEADS * RET_DK
RET_V = RET_HEADS * RET_DV
ROPE_BASE = 10000.0
ROPE_HALF = RET_DK // 2
SSM_INNER = 2 * D_MODEL
SSM_HEAD_DIM = 64
SSM_HEADS = SSM_INNER // SSM_HEAD_DIM
SSM_GROUPS = 8
SSM_HPG = SSM_HEADS // SSM_GROUPS
SSM_STATE = 128
SSM_CONV = 4
SSM_BC = SSM_GROUPS * SSM_STATE
SSM_XBC = SSM_INNER + 2 * SSM_BC
SSM_GW = SSM_INNER // SSM_GROUPS
D_FF = 5632
EPS = 1e-6
LOG2E = math.log2(math.e)

LANES = 128
SUBLANES = 8
BF16_SUBLANES = 16
V7X_VMEM_BYTES = 64 * 1024 * 1024

OFF_Q = 0
OFF_K = OFF_Q + RET_QK
OFF_V = OFF_K + RET_QK
OFF_G = OFF_V + RET_V
OFF_Z = OFF_G + RET_V
OFF_X = OFF_Z + SSM_INNER
OFF_B = OFF_X + SSM_INNER
OFF_C = OFF_B + SSM_BC
PROJ_W = OFF_C + SSM_BC
DT_PAD = LANES
assert DT_PAD == 2 * SSM_HEADS
CONV_TAIL = BF16_SUBLANES
FFN_TM, FFN_TF = 512, 1024
FFN_PAD = -(-D_FF // FFN_TF) * FFN_TF
SSD_CHUNKS_PER_STEP = 2
RET_CHUNKS_PER_STEP = 2


def _vmem_limit(nbytes):
    return int(min(V7X_VMEM_BYTES - 4 * 1024 * 1024, max(nbytes, 16 * 1024 * 1024)))


def _dot(a, b):
    return jnp.dot(a.astype(BF16), b.astype(BF16), preferred_element_type=F32)


def _dot_nt(a, b):
    return lax.dot_general(a.astype(BF16), b.astype(BF16), (((1,), (1,)), ((), ())),
                           preferred_element_type=F32)


def _dot_tn(a, b):
    t = a.shape[0]
    pad = (-t) % LANES
    a = a.astype(F32)
    b = b.astype(BF16)
    if pad:
        a = jnp.concatenate([a, jnp.zeros((pad, a.shape[1]), F32)], axis=0)
        b = jnp.concatenate([b, jnp.zeros((pad, b.shape[1]), BF16)], axis=0)
    return jnp.dot(a.T.astype(BF16), b, preferred_element_type=F32)


def _rms(x, w):
    return x * lax.rsqrt(jnp.mean(x * x, axis=-1, keepdims=True) + EPS) * w


def _sigmoid(x):
    return 0.5 * jnp.tanh(0.5 * x) + 0.5


def _silu(x):
    h = 0.5 * x
    return h * jnp.tanh(h) + h


def _softplus(x):
    return jnp.maximum(x, 0.0) + jnp.log1p(jnp.exp(-jnp.abs(x)))


def _split3(x):
    hi = x.astype(BF16)
    r = x - hi.astype(F32)
    mid = r.astype(BF16)
    lo = (r - mid.astype(F32)).astype(BF16)
    return hi, mid, lo


def _ffn_kernel(x_ref, nw_ref, wg_ref, wu_ref, wd_ref, fw_ref, *refs, tail, sub, last_cols):
    if tail == "emit_norm":
        o_ref, no_ref, n_ref = refs
    else:
        o_ref, n_ref = refs
    j = pl.program_id(1)

    @pl.when(j == 0)
    def _():
        n_ref[...] = _rms(x_ref[...], nw_ref[...]).astype(BF16)
        o_ref[...] = jnp.zeros_like(o_ref)

    def accumulate(cols):
        n = n_ref[...]
        acc = o_ref[...]
        for lo in range(0, cols, sub):
            g = jnp.dot(n, wg_ref[:, lo:lo + sub], preferred_element_type=F32)
            u = jnp.dot(n, wu_ref[:, lo:lo + sub], preferred_element_type=F32)
            a = (_silu(g) * u).astype(BF16)
            acc = acc + jnp.dot(a, wd_ref[lo:lo + sub, :], preferred_element_type=F32)
        o_ref[...] = acc

    tf = wg_ref.shape[1]
    if last_cols == tf:
        accumulate(tf)
    else:
        pl.when(j < pl.num_programs(1) - 1)(lambda: accumulate(tf))
        pl.when(j == pl.num_programs(1) - 1)(lambda: accumulate(last_cols))

    @pl.when(j == pl.num_programs(1) - 1)
    def _():
        h = x_ref[...] + 0.5 * o_ref[...]
        if tail == "emit_norm":
            o_ref[...] = h
            no_ref[...] = _rms(h, fw_ref[...]).astype(BF16)
        else:
            o_ref[...] = _rms(h, fw_ref[...])


def _ffn(x, norm_w, wg, wu, wd, tail_w, *, tail):
    rows = x.shape[0]
    tm, tf = min(FFN_TM, rows), FFN_TF
    steps = FFN_PAD // tf
    last_cols = D_FF - (steps - 1) * tf
    row_block = pl.BlockSpec((tm, D_MODEL), lambda i, j: (i, 0))
    out_specs = [row_block]
    out_shape = [jax.ShapeDtypeStruct((rows, D_MODEL), F32)]
    if tail == "emit_norm":
        out_specs.append(row_block)
        out_shape.append(jax.ShapeDtypeStruct((rows, D_MODEL), BF16))
    sub = min(256, tf)
    est = (2 * 2 * tm * D_MODEL * 4 + tm * D_MODEL * 2 + 2 * 3 * D_MODEL * tf * 2
           + 2 * tm * D_MODEL * 4 + 3 * tm * sub * 4 + (2 * tm * D_MODEL * 2 if tail == "emit_norm" else 0))
    return pl.pallas_call(
        functools.partial(_ffn_kernel, tail=tail, sub=sub, last_cols=last_cols),
        grid=(rows // tm, steps),
        in_specs=[
            row_block,
            pl.BlockSpec((1, D_MODEL), lambda i, j: (0, 0)),
            pl.BlockSpec((D_MODEL, tf), lambda i, j: (0, j)),
            pl.BlockSpec((D_MODEL, tf), lambda i, j: (0, j)),
            pl.BlockSpec((tf, D_MODEL), lambda i, j: (j, 0)),
            pl.BlockSpec((1, D_MODEL), lambda i, j: (0, 0)),
        ],
        out_specs=out_specs,
        out_shape=out_shape,
        scratch_shapes=[pltpu.VMEM((tm, D_MODEL), BF16)],
        compiler_params=pltpu.CompilerParams(
            dimension_semantics=("parallel", "arbitrary"), vmem_limit_bytes=_vmem_limit(est)),
        name="ffn" if tail == "emit_norm" else "ffn_final",
    )(x, norm_w, wg, wu, wd, tail_w)


def _rope_kernel(inv_ref, cos_ref, sin_ref, *, seq, pos0):
    tr = cos_ref.shape[0]
    row = pl.program_id(0) * tr + lax.broadcasted_iota(jnp.int32, (tr, 1), 0)
    ang = ((row & (seq - 1)) + pos0).astype(F32) * inv_ref[...]
    cos_ref[...] = jnp.cos(ang)
    sin_ref[...] = jnp.sin(ang)


def _rope_tables(inv_freq, *, length, seq, pos0):
    tr = min(512, length)
    spec = pl.BlockSpec((tr, ROPE_HALF), lambda i: (i, 0))
    return pl.pallas_call(
        functools.partial(_rope_kernel, seq=seq, pos0=pos0),
        grid=(length // tr,),
        in_specs=[pl.BlockSpec((1, ROPE_HALF), lambda i: (0, 0))],
        out_specs=[spec, spec],
        out_shape=[jax.ShapeDtypeStruct((length, ROPE_HALF), F32)] * 2,
        name="rope",
    )(inv_freq)


def _proj_kernel(n_ref, w_ref, cos_ref, sin_ref, o_ref, *, tn):
    j = pl.program_id(1)
    n_q = RET_QK // tn

    @pl.when(j < 2 * n_q)
    def _():
        acc = jnp.dot(n_ref[...], w_ref[...], preferred_element_type=F32)
        cos = cos_ref[...]
        sin = sin_ref[...]
        scale = jnp.where(j < n_q, 1.0, RET_DK ** -0.5).astype(F32)
        for lo in range(0, tn, RET_DK):
            x1 = acc[:, lo:lo + ROPE_HALF]
            x2 = acc[:, lo + ROPE_HALF:lo + RET_DK]
            o_ref[:, lo:lo + ROPE_HALF] = ((x1 * cos - x2 * sin) * scale).astype(BF16)
            o_ref[:, lo + ROPE_HALF:lo + RET_DK] = ((x1 * sin + x2 * cos) * scale).astype(BF16)

    @pl.when(j >= 2 * n_q)
    def _():
        o_ref[...] = jnp.dot(n_ref[...], w_ref[...], preferred_element_type=F32).astype(BF16)


def _proj(n, w_all, inv_freq, *, seq, pos0):
    rows = n.shape[0]
    tm = min(1024, rows)
    tn = 2048
    assert seq & (seq - 1) == 0 and RET_QK % tn == 0 and PROJ_W % tn == 0
    table_len = max(seq, tm)
    cos, sin = _rope_tables(inv_freq, length=table_len, seq=seq, pos0=pos0)
    n_tab = table_len // tm
    est = (2 * tm * D_MODEL * 2 + 2 * D_MODEL * tn * 2 + 2 * tm * tn * 2 + 2 * tm * tn * 4
           + 4 * tm * LANES * 4)
    return pl.pallas_call(
        functools.partial(_proj_kernel, tn=tn),
        grid=(rows // tm, PROJ_W // tn),
        in_specs=[
            pl.BlockSpec((tm, D_MODEL), lambda i, j: (i, 0)),
            pl.BlockSpec((D_MODEL, tn), lambda i, j: (0, j)),
            pl.BlockSpec((tm, ROPE_HALF), lambda i, j: (i % n_tab, 0)),
            pl.BlockSpec((tm, ROPE_HALF), lambda i, j: (i % n_tab, 0)),
        ],
        out_specs=pl.BlockSpec((tm, tn), lambda i, j: (i, j)),
        out_shape=jax.ShapeDtypeStruct((rows, PROJ_W), BF16),
        compiler_params=pltpu.CompilerParams(
            dimension_semantics=("parallel", "arbitrary"), vmem_limit_bytes=_vmem_limit(est)),
        name="proj",
    )(n, w_all, cos, sin)


def _gates_kernel(n_ref, w_ref, wdtt_ref, o_ref, dtt_ref, *, batched_dtt):
    @pl.when(pl.program_id(1) == 0)
    def _():
        dtt = _dot_nt(wdtt_ref[...], n_ref[...])
        if batched_dtt:
            dtt_ref[0] = dtt
        else:
            dtt_ref[...] = dtt

    o_ref[...] = jnp.dot(n_ref[...], w_ref[...], preferred_element_type=F32).astype(BF16)


def _gates(n, w_gates, w_dtt, *, batch, seq):
    rows = n.shape[0]
    tm = min(1024, rows)
    tn = 2048
    batched_dtt = tm <= seq
    if batched_dtt:
        per_seq = seq // tm
        dtt_spec = pl.BlockSpec((1, DT_PAD, tm), lambda i, j: (i // per_seq, 0, i % per_seq))
        dtt_shape = jax.ShapeDtypeStruct((batch, DT_PAD, seq), F32)
    else:
        dtt_spec = pl.BlockSpec((DT_PAD, tm), lambda i, j: (0, i))
        dtt_shape = jax.ShapeDtypeStruct((DT_PAD, rows), F32)
    est = (2 * tm * D_MODEL * 2 + 2 * D_MODEL * tn * 2 + 2 * tm * tn * 2 + 2 * tm * tn * 4
           + 2 * D_MODEL * DT_PAD * 2 + 4 * tm * DT_PAD * 4)
    gates, dtt = pl.pallas_call(
        functools.partial(_gates_kernel, batched_dtt=batched_dtt),
        grid=(rows // tm, 2 * D_MODEL // tn),
        in_specs=[
            pl.BlockSpec((tm, D_MODEL), lambda i, j: (i, 0)),
            pl.BlockSpec((D_MODEL, tn), lambda i, j: (0, j)),
            pl.BlockSpec((DT_PAD, D_MODEL), lambda i, j: (0, 0)),
        ],
        out_specs=[pl.BlockSpec((tm, tn), lambda i, j: (i, j)), dtt_spec],
        out_shape=[jax.ShapeDtypeStruct((rows, 2 * D_MODEL), BF16), dtt_shape],
        compiler_params=pltpu.CompilerParams(
            dimension_semantics=("parallel", "arbitrary"), vmem_limit_bytes=_vmem_limit(est)),
        name="gates",
    )(n, w_gates, w_dtt)
    if not batched_dtt:
        dtt = dtt.reshape(DT_PAD, batch, seq).transpose(1, 0, 2)
    return gates, dtt


def _ret_kernel(*refs, chunk, has_init):
    if has_init:
        q_ref, k_ref, v_ref, g_ref, nw_ref, s0_ref, o_ref, s_ref, mask_ref = refs
    else:
        q_ref, k_ref, v_ref, g_ref, nw_ref, o_ref, s_ref, mask_ref = refs
    c = pl.program_id(1)
    log_decay = [math.log1p(-(2.0 ** (-5.0 - hd))) for hd in range(RET_HEADS)]

    @pl.when((pl.program_id(0) == 0) & (c == 0))
    def _():
        ii = lax.broadcasted_iota(jnp.int32, (chunk, chunk), 0)
        jj = lax.broadcasted_iota(jnp.int32, (chunk, chunk), 1)
        causal = ii >= jj
        diff = jnp.where(causal, ii - jj, 0).astype(F32)
        for hd in range(RET_HEADS):
            mask_ref[hd] = jnp.where(causal, jnp.exp(diff * log_decay[hd]), 0.0)

    @pl.when(c == 0)
    def _():
        if has_init:
            s_ref[...] = s0_ref[...]
        else:
            s_ref[...] = jnp.zeros_like(s_ref)

    idx = lax.broadcasted_iota(jnp.int32, (chunk, 1), 0).astype(F32)
    for r0 in range(0, q_ref.shape[1], chunk):
        rows = slice(r0, r0 + chunk)
        for hd in range(RET_HEADS):
            lg = log_decay[hd]
            qk_cols = slice(hd * RET_DK, (hd + 1) * RET_DK)
            v_cols = slice(hd * RET_DV, (hd + 1) * RET_DV)
            decay_mask = mask_ref[hd]
            inner_decay = jnp.exp((idx + 1.0) * lg)
            state_decay = jnp.exp((chunk - 1.0 - idx) * lg)
            chunk_decay = math.exp(chunk * lg)
            q = q_ref[0, rows, qk_cols]
            k = k_ref[0, rows, qk_cols]
            v = v_ref[0, rows, v_cols]
            s = s_ref[0, hd]
            scores = _dot_nt(q, k) * decay_mask
            o = _dot(scores, v) + _dot(q, s) * inner_decay
            s_ref[0, hd] = s * chunk_decay + _dot_tn(k.astype(F32) * state_decay, v)
            mu = jnp.mean(o, axis=-1, keepdims=True)
            var = jnp.mean(jnp.square(o - mu), axis=-1, keepdims=True)
            on = (o - mu) * lax.rsqrt(var + EPS) * nw_ref[:, v_cols]
            gate = g_ref[0, rows, v_cols].astype(F32)
            o_ref[0, rows, v_cols] = (_silu(gate) * on).astype(BF16)


def _retention(proj3, norm_w, s0, *, chunk, block):
    b, seq, _ = proj3.shape
    has_init = s0 is not None
    assert block % chunk == 0 and seq % block == 0
    state_block = pl.BlockSpec((1, RET_HEADS, RET_DK, RET_DV), lambda bi, c: (bi, 0, 0, 0))
    in_specs = [
        pl.BlockSpec((1, block, RET_QK), lambda bi, c: (bi, c, OFF_Q // RET_QK)),
        pl.BlockSpec((1, block, RET_QK), lambda bi, c: (bi, c, OFF_K // RET_QK)),
        pl.BlockSpec((1, block, RET_V), lambda bi, c: (bi, c, OFF_V // RET_V)),
        pl.BlockSpec((1, block, RET_V), lambda bi, c: (bi, c, OFF_G // RET_V)),
        pl.BlockSpec((1, RET_V), lambda bi, c: (0, 0)),
    ]
    args = [proj3, proj3, proj3, proj3, norm_w]
    if has_init:
        in_specs.append(state_block)
        args.append(s0)
    state_bytes = RET_HEADS * RET_DK * RET_DV * 4
    est = (2 * block * (2 * RET_QK + 3 * RET_V) * 2 + (4 if has_init else 2) * state_bytes
           + 12 * chunk * max(chunk, RET_DV) * 4 + RET_HEADS * chunk * chunk * 4)
    return pl.pallas_call(
        functools.partial(_ret_kernel, chunk=chunk, has_init=has_init),
        grid=(b, seq // block),
        in_specs=in_specs,
        out_specs=[pl.BlockSpec((1, block, RET_V), lambda bi, c: (bi, c, 0)), state_block],
        out_shape=[jax.ShapeDtypeStruct((b, seq, RET_V), BF16),
                   jax.ShapeDtypeStruct((b, RET_HEADS, RET_DK, RET_DV), F32)],
        scratch_shapes=[pltpu.VMEM((RET_HEADS, chunk, chunk), F32)],
        compiler_params=pltpu.CompilerParams(
            dimension_semantics=("arbitrary", "arbitrary"), vmem_limit_bytes=_vmem_limit(est)),
        name="retention",
    )(*args)


def _expand_table():
    e = np.zeros((SSM_GROUPS, 2 * DT_PAD, SSM_GW), np.float32)
    for g in range(SSM_GROUPS):
        for col in range(SSM_GW):
            head = g * SSM_HPG + col // SSM_HEAD_DIM
            for piece in range(3):
                e[g, piece * SSM_HEADS + head, col] = 1.0
    return jnp.asarray(e, BF16)


def _ssd_kernel(*refs, chunk, has_init, has_dt):
    refs = iter(refs)
    take = lambda k: [next(refs) for _ in range(k)]
    z_ref, x_ref, b_ref, c_ref = take(4)
    dt_ref, = take(1) if has_dt else (None,)
    dtt_ref, cw_ref, cb_ref, dtb_ref, dtbt_ref, al_ref, alt_ref, dsk_ref, nw_ref, ex_ref = take(10)
    h0_ref, cs0_ref = take(2) if has_init else (None, None)
    y_ref, hout_ref, csout_ref, ht_ref, xc_ref = take(5)
    fix_ref, ext_ref = take(2) if has_init else (None, None)
    c = pl.program_id(1)
    taps = SSM_CONV - 1
    kk = CONV_TAIL + chunk

    @pl.when(c == 0)
    def _():
        for g in range(SSM_GROUPS):
            if has_init:
                blk = h0_ref[0, g * SSM_HPG:(g + 1) * SSM_HPG].reshape(SSM_GW, SSM_STATE)
                ht_ref[g] = blk.T
            else:
                ht_ref[g] = jnp.zeros((SSM_STATE, SSM_GW), F32)
        xc_ref[0:CONV_TAIL, :] = jnp.zeros((CONV_TAIL, SSM_XBC), BF16)
        if has_init:
            ext_ref[...] = jnp.zeros_like(ext_ref)
            ext_ref[SUBLANES - taps:SUBLANES, :] = cs0_ref[0]
            fix = jnp.zeros((SUBLANES, SSM_XBC), F32)
            for i in range(taps):
                fix = fix + ext_ref[SUBLANES - taps + i:2 * SUBLANES - taps + i, :] * cw_ref[i:i + 1, :]
            fix_ref[...] = fix

    if has_init:
        @pl.when(c == 1)
        def _():
            fix_ref[...] = jnp.zeros_like(fix_ref)

    block = x_ref.shape[1]
    xc_ref[CONV_TAIL:CONV_TAIL + block, 0:SSM_INNER] = x_ref[0]
    xc_ref[CONV_TAIL:CONV_TAIL + block, SSM_INNER:SSM_INNER + SSM_BC] = b_ref[0]
    xc_ref[CONV_TAIL:CONV_TAIL + block, SSM_INNER + SSM_BC:SSM_XBC] = c_ref[0]

    si = lax.broadcasted_iota(jnp.int32, (taps * chunk, kk), 0)
    sj = lax.broadcasted_iota(jnp.int32, (taps * chunk, kk), 1)
    tap = si >> (chunk.bit_length() - 1)
    shift = jnp.where(sj == (si & (chunk - 1)) + tap + (CONV_TAIL - taps), 1.0, 0.0).astype(BF16)

    def conv(src_ref, r0, lo, col0, width):
        cur = src_ref[0, r0:r0 + chunk, lo:lo + width]
        lo = col0 + lo
        prev = jnp.dot(shift, xc_ref[r0:r0 + kk, lo:lo + width], preferred_element_type=F32)
        half = cb_ref[:, lo:lo + width] + cur.astype(F32) * cw_ref[taps:taps + 1, lo:lo + width]
        for i in range(taps):
            half = half + prev[i * chunk:(i + 1) * chunk] * cw_ref[i:i + 1, lo:lo + width]
        if has_init and r0 == 0:
            half = jnp.concatenate([half[:SUBLANES] + fix_ref[:, lo:lo + width], half[SUBLANES:]], axis=0)
        return half * jnp.tanh(half) + half

    a_row = -jnp.exp(al_ref[...]) * LOG2E
    a_col = -jnp.exp(alt_ref[...]) * LOG2E
    ii = lax.broadcasted_iota(jnp.int32, (chunk, chunk), 0)
    jj = lax.broadcasted_iota(jnp.int32, (chunk, chunk), 1)
    tri = ii >= jj
    lower = tri.astype(BF16)
    upper = (ii <= jj).astype(BF16)
    first_half = lax.broadcasted_iota(jnp.int32, (chunk, LANES), 1) < SSM_HEADS
    lane_head = lax.broadcasted_iota(jnp.int32, (chunk, LANES), 1) < SSM_HEAD_DIM
    head_a = jnp.where(lane_head, 1.0, 0.0).astype(BF16)
    head_b = jnp.where(lane_head, 0.0, 1.0).astype(BF16)

    def pieces(v):
        hi, mid, lo = _split3(v)
        return jnp.concatenate([jnp.where(first_half, hi, mid), lo], axis=1)

    def one_chunk(r0):
        rows = slice(r0, r0 + chunk)
        dtt_raw = dtt_ref[0, :, rows]
        dt_raw = dt_ref[0, rows, :] if has_dt else dtt_raw.T
        dt = _softplus(dt_raw + dtb_ref[...])
        dtt = _softplus(dtt_raw + dtbt_ref[...])
        cum = sum(jnp.dot(lower, p, preferred_element_type=F32) for p in _split3(dt * a_row))
        cumt = sum(jnp.dot(p, upper, preferred_element_type=F32) for p in _split3(dtt * a_col))
        dt_p = pieces(dt)
        cum_p = pieces(cum)

        def stage_a(g):
            xs = conv(x_ref, r0, g * SSM_GW, 0, SSM_GW)
            bm = conv(b_ref, r0, g * SSM_STATE, SSM_INNER, SSM_STATE)
            cm = conv(c_ref, r0, g * SSM_STATE, SSM_INNER + SSM_BC, SSM_STATE)
            cbm = jnp.where(tri, _dot_nt(cm, bm), 0.0)
            dt_e = jnp.dot(dt_p, ex_ref[g], preferred_element_type=F32)
            cum_e = jnp.dot(cum_p, ex_ref[g], preferred_element_type=F32)
            e_all = jnp.exp2(cum_e)
            xdt = xs * dt_e
            xd = xdt * jnp.exp2(cum_e[chunk - 1:chunk, :] - cum_e)
            return xs, bm, cm, cbm, e_all, xd, xdt.astype(BF16)

        nxt = stage_a(0)
        for g in range(SSM_GROUPS):
            xs, bm, cm, cbm, e_all, xd, xdt16 = nxt
            if g + 1 < SSM_GROUPS:
                nxt = stage_a(g + 1)
            cols = slice(g * SSM_GW, (g + 1) * SSM_GW)
            y_parts = []
            for s in range(SSM_GW // LANES):
                rhs = xdt16[:, s * LANES:(s + 1) * LANES]
                ms = []
                for j in (g * SSM_HPG + 2 * s, g * SSM_HPG + 2 * s + 1):
                    seg = (jnp.broadcast_to(cum[:, j:j + 1], (chunk, chunk))
                           - jnp.broadcast_to(cumt[j:j + 1, :], (chunk, chunk)))
                    ms.append((cbm * jnp.exp2(jnp.minimum(seg, 0.0))).astype(BF16))
                if chunk % LANES == 0:
                    rhs2 = jnp.concatenate([rhs * head_a, rhs * head_b], axis=0)
                    y_parts.append(jnp.dot(jnp.concatenate(ms, axis=1), rhs2, preferred_element_type=F32))
                else:
                    y_parts.append(jnp.dot(ms[0], rhs * head_a, preferred_element_type=F32)
                                   + jnp.dot(ms[1], rhs * head_b, preferred_element_type=F32))
            htg = ht_ref[g]
            y = jnp.concatenate(y_parts, axis=1) + _dot(cm, htg) * e_all + xs * dsk_ref[:, cols]
            ht_ref[g] = htg * e_all[chunk - 1:chunk, :] + _dot_tn(bm, xd)
            y = y * _silu(z_ref[0, rows, cols].astype(F32))
            y_ref[0, rows, cols] = _rms(y, nw_ref[:, cols]).astype(BF16)

    for r0 in range(0, block, chunk):
        one_chunk(r0)

    xc_ref[0:CONV_TAIL, :] = xc_ref[block:block + CONV_TAIL, :]

    @pl.when(c == pl.num_programs(1) - 1)
    def _():
        csout_ref[0] = xc_ref[block:block + CONV_TAIL, :].astype(F32)[CONV_TAIL - taps:CONV_TAIL, :]
        for g in range(SSM_GROUPS):
            hout_ref[0, g * SSM_HPG:(g + 1) * SSM_HPG] = ht_ref[g].T.reshape(SSM_HPG, SSM_HEAD_DIM, SSM_STATE)


def _ssd(proj3, dtt3, conv_w, conv_b, dt_bias, a_log, d_skip, norm_w, h0, cs0, *, chunk, block):
    b, seq, _ = proj3.shape
    has_init = h0 is not None
    has_dt = chunk % LANES != 0
    assert chunk % SUBLANES == 0 and chunk >= SUBLANES and block % chunk == 0 and seq % block == 0
    assert chunk & (chunk - 1) == 0
    dtb = jnp.tile(dt_bias.astype(F32), 2)
    al = jnp.tile(a_log.astype(F32), 2)
    dsk = jnp.repeat(d_skip.astype(F32), SSM_HEAD_DIM).reshape(1, SSM_INNER)
    const = lambda *shape: pl.BlockSpec(shape, lambda bi, c: (0,) * len(shape))
    in_specs = [
        pl.BlockSpec((1, block, SSM_INNER), lambda bi, c: (bi, c, OFF_Z // SSM_INNER)),
        pl.BlockSpec((1, block, SSM_INNER), lambda bi, c: (bi, c, OFF_X // SSM_INNER)),
        pl.BlockSpec((1, block, SSM_BC), lambda bi, c: (bi, c, OFF_B // SSM_BC)),
        pl.BlockSpec((1, block, SSM_BC), lambda bi, c: (bi, c, OFF_C // SSM_BC)),
    ]
    args = [proj3, proj3, proj3, proj3]
    if has_dt:
        in_specs.append(pl.BlockSpec((1, block, DT_PAD), lambda bi, c: (bi, c, 0)))
        args.append(dtt3.transpose(0, 2, 1))
    in_specs += [
        pl.BlockSpec((1, DT_PAD, block), lambda bi, c: (bi, 0, c)),
        const(SSM_CONV, SSM_XBC), const(1, SSM_XBC), const(1, DT_PAD), const(DT_PAD, 1),
        const(1, DT_PAD), const(DT_PAD, 1), const(1, SSM_INNER), const(1, SSM_INNER),
        const(SSM_GROUPS, 2 * DT_PAD, SSM_GW),
    ]
    args += [dtt3, 0.5 * conv_w.astype(F32), 0.5 * conv_b.astype(F32).reshape(1, SSM_XBC),
             dtb.reshape(1, DT_PAD), dtb.reshape(DT_PAD, 1), al.reshape(1, DT_PAD), al.reshape(DT_PAD, 1),
             dsk, norm_w.astype(F32).reshape(1, SSM_INNER), _expand_table()]
    h_block = pl.BlockSpec((1, SSM_HEADS, SSM_HEAD_DIM, SSM_STATE), lambda bi, c: (bi, 0, 0, 0))
    cs_block = pl.BlockSpec((1, SSM_CONV - 1, SSM_XBC), lambda bi, c: (bi, 0, 0))
    scratch = [pltpu.VMEM((SSM_GROUPS, SSM_STATE, SSM_GW), F32),
               pltpu.VMEM((CONV_TAIL + block, SSM_XBC), BF16)]
    if has_init:
        in_specs += [h_block, cs_block]
        args += [h0, cs0]
        scratch += [pltpu.VMEM((SUBLANES, SSM_XBC), F32), pltpu.VMEM((2 * SUBLANES, SSM_XBC), F32)]
    state_bytes = SSM_HEADS * SSM_HEAD_DIM * SSM_STATE * 4
    est = (2 * block * (3 * SSM_INNER + 2 * SSM_BC) * 2 + (5 if has_init else 3) * state_bytes
           + (block + CONV_TAIL) * SSM_XBC * 2 + 32 * chunk * max(chunk, SSM_GW) * 4 + 8 * 1024 * 1024)
    return pl.pallas_call(
        functools.partial(_ssd_kernel, chunk=chunk, has_init=has_init, has_dt=has_dt),
        grid=(b, seq // block),
        in_specs=in_specs,
        out_specs=[pl.BlockSpec((1, block, SSM_INNER), lambda bi, c: (bi, c, 0)), h_block, cs_block],
        out_shape=[jax.ShapeDtypeStruct((b, seq, SSM_INNER), BF16),
                   jax.ShapeDtypeStruct((b, SSM_HEADS, SSM_HEAD_DIM, SSM_STATE), F32),
                   jax.ShapeDtypeStruct((b, SSM_CONV - 1, SSM_XBC), F32)],
        scratch_shapes=scratch,
        compiler_params=pltpu.CompilerParams(
            dimension_semantics=("parallel", "arbitrary"), vmem_limit_bytes=_vmem_limit(est)),
        name="ssd",
    )(*args)


def _merge_kernel(ar_ref, as_ref, wr_ref, ws_ref, ga_ref, gb_ref, ba_ref, bb_ref, o_ref, *, sub):
    for lo in range(0, o_ref.shape[0], sub):
        rows = slice(lo, lo + sub)
        yr = jnp.dot(ar_ref[rows, :], wr_ref[...], preferred_element_type=F32)
        ys = jnp.dot(as_ref[rows, :], ws_ref[...], preferred_element_type=F32)
        ga = _sigmoid(ga_ref[rows, :].astype(F32) + ba_ref[...])
        gb = _sigmoid(gb_ref[rows, :].astype(F32) + bb_ref[...])
        o_ref[rows, :] = (ga * yr + gb * ys).astype(BF16)


def _merge(a_ret, a_ssm, w_ret, w_ssm, gates, b_gate):
    rows = a_ret.shape[0]
    tm = min(1024, rows)
    tn = 512
    nb = D_MODEL // tn
    est = 2 * 2 * tm * RET_V * 2 + 2 * 2 * RET_V * tn * 2 + 8 * tm * tn * 4
    return pl.pallas_call(
        functools.partial(_merge_kernel, sub=min(512, tm)),
        grid=(rows // tm, nb),
        in_specs=[
            pl.BlockSpec((tm, RET_V), lambda i, j: (i, 0)),
            pl.BlockSpec((tm, SSM_INNER), lambda i, j: (i, 0)),
            pl.BlockSpec((RET_V, tn), lambda i, j: (0, j)),
            pl.BlockSpec((SSM_INNER, tn), lambda i, j: (0, j)),
            pl.BlockSpec((tm, tn), lambda i, j: (i, j)),
            pl.BlockSpec((tm, tn), lambda i, j: (i, nb + j)),
            pl.BlockSpec((1, tn), lambda i, j: (0, j)),
            pl.BlockSpec((1, tn), lambda i, j: (0, nb + j)),
        ],
        out_specs=pl.BlockSpec((tm, tn), lambda i, j: (i, j)),
        out_shape=jax.ShapeDtypeStruct((rows, D_MODEL), BF16),
        compiler_params=pltpu.CompilerParams(
            dimension_semantics=("parallel", "arbitrary"), vmem_limit_bytes=_vmem_limit(est)),
        name="merge",
    )(a_ret, a_ssm, w_ret, w_ssm, gates, gates, b_gate, b_gate)


def _outproj_kernel(m_ref, w_ref, h_ref, o_ref):
    o_ref[...] = h_ref[...] + jnp.dot(m_ref[...], w_ref[...], preferred_element_type=F32)


def _outproj(m, w_out, h):
    rows = m.shape[0]
    tm = min(1024, rows)
    tn = 1024
    est = 2 * tm * D_MODEL * 2 + 2 * D_MODEL * tn * 2 + 5 * tm * tn * 4
    return pl.pallas_call(
        _outproj_kernel,
        grid=(rows // tm, D_MODEL // tn),
        in_specs=[
            pl.BlockSpec((tm, D_MODEL), lambda i, j: (i, 0)),
            pl.BlockSpec((D_MODEL, tn), lambda i, j: (0, j)),
            pl.BlockSpec((tm, tn), lambda i, j: (i, j)),
        ],
        out_specs=pl.BlockSpec((tm, tn), lambda i, j: (i, j)),
        out_shape=jax.ShapeDtypeStruct((rows, D_MODEL), F32),
        compiler_params=pltpu.CompilerParams(
            dimension_semantics=("parallel", "arbitrary"), vmem_limit_bytes=_vmem_limit(est)),
        name="outproj",
    )(m, w_out, h)


def _layer(x, s_ret, s_ssm, s_conv, pos0, p, *, ret_chunk, ssd_chunk):
    b, seq, _ = x.shape
    rows = b * seq
    h, n = _ffn(x.reshape(rows, D_MODEL), p["norm_ffn1"], p["ffn1_w_gate"], p["ffn1_w_up"], p["ffn1_w_down"],
                p["norm_mix"], tail="emit_norm")
    proj = _proj(n, p["w_all"], p["inv_freq"], seq=seq, pos0=pos0)
    gates, dtt3 = _gates(n, p["w_gates"], p["w_dtt"], batch=b, seq=seq)
    proj3 = proj.reshape(b, seq, PROJ_W)
    a_ret, ret_new = _retention(proj3, p["ret_norm_w"], s_ret, chunk=min(ret_chunk, seq),
                                block=min(RET_CHUNKS_PER_STEP * ret_chunk, seq))
    a_ssm, ssm_new, conv_new = _ssd(proj3, dtt3, p["conv_w"], p["conv_b"], p["dt_bias"], p["a_log"],
                                    p["d_skip"], p["ssm_norm_w"], s_ssm, s_conv,
                                    chunk=min(ssd_chunk, seq), block=min(SSD_CHUNKS_PER_STEP * ssd_chunk, seq))
    m = _merge(a_ret.reshape(rows, RET_V), a_ssm.reshape(rows, SSM_INNER), p["w_out_ret"], p["w_out_ssm"],
               gates, p["b_gate"])
    h2 = _outproj(m, p["w_out"], h)
    y, = _ffn(h2, p["norm_ffn2"], p["ffn2_w_gate"], p["ffn2_w_up"], p["ffn2_w_down"], p["norm_final"],
              tail="final_norm")
    return y.reshape(b, seq, D_MODEL), ret_new, ssm_new, conv_new


def kernel(x_prompt, x_sample, state_ret, state_ssm, state_conv, norm_ffn1, ffn1_w_gate, ffn1_w_up, ffn1_w_down, norm_mix, w_in, b_gate, ret_norm_w, w_out_ret, conv_w, conv_b, dt_bias, a_log, d_skip, ssm_norm_w, w_out_ssm, w_out, norm_ffn2, ffn2_w_gate, ffn2_w_up, ffn2_w_down, norm_final):
    row = lambda v: v.astype(F32).reshape(1, -1)
    pad_cols = lambda w: jnp.pad(w.astype(BF16), ((0, 0), (0, FFN_PAD - D_FF)))
    pad_rows = lambda w: jnp.pad(w.astype(BF16), ((0, FFN_PAD - D_FF), (0, 0)))
    dt_lo = 2 * RET_QK + 2 * RET_V + SSM_INNER + SSM_XBC
    p = dict(
        norm_ffn1=row(norm_ffn1), norm_mix=row(norm_mix), norm_ffn2=row(norm_ffn2), norm_final=row(norm_final),
        ffn1_w_gate=pad_cols(ffn1_w_gate), ffn1_w_up=pad_cols(ffn1_w_up), ffn1_w_down=pad_rows(ffn1_w_down),
        ffn2_w_gate=pad_cols(ffn2_w_gate), ffn2_w_up=pad_cols(ffn2_w_up), ffn2_w_down=pad_rows(ffn2_w_down),
        w_all=w_in.astype(BF16), w_gates=w_in[:, dt_lo + SSM_HEADS:].astype(BF16),
        w_dtt=jnp.tile(w_in[:, dt_lo:dt_lo + SSM_HEADS].T, (2, 1)).astype(BF16),
        inv_freq=(ROPE_BASE ** (-jnp.arange(ROPE_HALF, dtype=F32) / ROPE_HALF)).reshape(1, ROPE_HALF),
        b_gate=row(b_gate), ret_norm_w=row(ret_norm_w),
        w_out_ret=w_out_ret.astype(BF16), w_out_ssm=w_out_ssm.astype(BF16), w_out=w_out.astype(BF16),
        conv_w=conv_w, conv_b=conv_b, dt_bias=dt_bias, a_log=a_log, d_skip=d_skip, ssm_norm_w=ssm_norm_w,
    )
    y_p, ret_p, ssm_p, conv_p = _layer(x_prompt, None, None, None, 0, p, ret_chunk=256, ssd_chunk=128)
    y_s, ret_s, ssm_s, conv_s = _layer(x_sample, state_ret.astype(F32), state_ssm.astype(F32),
                                       state_conv.astype(F32), PAST_LEN, p, ret_chunk=256, ssd_chunk=128)
    return (y_p, y_s, ret_p, ssm_p, conv_p, ret_s, ssm_s, conv_s)
```

```python
import functools
import math

import numpy as np
import jax
import jax.numpy as jnp
from jax import lax
from jax.experimental import pallas as pl
from jax.experimental.pallas import tpu as pltpu

F32 = jnp.float32
BF16 = jnp.bfloat16

D_MODEL = 2048
PAST_LEN = 1024
RET_HEADS = 8
RET_DK = D_MODEL // RET_HEADS
RET_DV = 2 * D_MODEL // RET_HEADS
RET_QK = RET_HEADS * RET_DK
RET_V = RET_HEADS * RET_DV
ROPE_BASE = 10000.0
ROPE_HALF = RET_DK // 2
SSM_INNER = 2 * D_MODEL
SSM_HEAD_DIM = 64
SSM_HEADS = SSM_INNER // SSM_HEAD_DIM
SSM_GROUPS = 8
SSM_HPG = SSM_HEADS // SSM_GROUPS
SSM_STATE = 128
SSM_CONV = 4
SSM_BC = SSM_GROUPS * SSM_STATE
SSM_XBC = SSM_INNER + 2 * SSM_BC
SSM_GW = SSM_INNER // SSM_GROUPS
D_FF = 5632
EPS = 1e-6
LOG2E = math.log2(math.e)

LANES = 128
SUBLANES = 8
BF16_SUBLANES = 16
V7X_VMEM_BYTES = 64 * 1024 * 1024

OFF_Q = 0
OFF_K = OFF_Q + RET_QK
OFF_V = OFF_K + RET_QK
OFF_G = OFF_V + RET_V
OFF_Z = OFF_G + RET_V
OFF_X = OFF_Z + SSM_INNER
OFF_B = OFF_X + SSM_INNER
OFF_C = OFF_B + SSM_BC
PROJ_W = OFF_C + SSM_BC
DT_PAD = LANES
assert DT_PAD == 2 * SSM_HEADS
CONV_TAIL = BF16_SUBLANES
FFN_TILES = {"emit_norm": (512, 512), "final_norm": (512, 512)}
SSD_CHUNKS_PER_STEP = 2
RET_CHUNKS_PER_STEP = 2


def _vmem_limit(nbytes):
    return int(min(V7X_VMEM_BYTES - 4 * 1024 * 1024, max(nbytes, 16 * 1024 * 1024)))


def _dot(a, b):
    return jnp.dot(a.astype(BF16), b.astype(BF16), preferred_element_type=F32)


def _dot_nt(a, b):
    return lax.dot_general(a.astype(BF16), b.astype(BF16), (((1,), (1,)), ((), ())),
                           preferred_element_type=F32)


def _dot_tn(a, b):
    t = a.shape[0]
    pad = (-t) % LANES
    a = a.astype(F32)
    b = b.astype(BF16)
    if pad:
        a = jnp.concatenate([a, jnp.zeros((pad, a.shape[1]), F32)], axis=0)
        b = jnp.concatenate([b, jnp.zeros((pad, b.shape[1]), BF16)], axis=0)
    return jnp.dot(a.T.astype(BF16), b, preferred_element_type=F32)


def _rms(x, w):
    return x * lax.rsqrt(jnp.mean(x * x, axis=-1, keepdims=True) + EPS) * w


def _sigmoid(x):
    return 0.5 * jnp.tanh(0.5 * x) + 0.5


def _silu(x):
    h = 0.5 * x
    return h * jnp.tanh(h) + h


def _softplus(x):
    return jnp.maximum(x, 0.0) + jnp.log1p(jnp.exp(-jnp.abs(x)))


def _split3(x):
    hi = x.astype(BF16)
    r = x - hi.astype(F32)
    mid = r.astype(BF16)
    lo = (r - mid.astype(F32)).astype(BF16)
    return hi, mid, lo


def _ffn_kernel(x_ref, nw_ref, wg_ref, wu_ref, wd_ref, fw_ref, *refs, tail, sub):
    if tail == "emit_norm":
        o_ref, no_ref, n_ref = refs
    else:
        o_ref, n_ref = refs
    j = pl.program_id(1)

    @pl.when(j == 0)
    def _():
        n_ref[...] = _rms(x_ref[...], nw_ref[...]).astype(BF16)
        o_ref[...] = jnp.zeros_like(o_ref)

    n = n_ref[...]
    acc = o_ref[...]
    for lo in range(0, wg_ref.shape[1], sub):
        g = jnp.dot(n, wg_ref[:, lo:lo + sub], preferred_element_type=F32)
        u = jnp.dot(n, wu_ref[:, lo:lo + sub], preferred_element_type=F32)
        a = (_silu(g) * u).astype(BF16)
        acc = acc + jnp.dot(a, wd_ref[lo:lo + sub, :], preferred_element_type=F32)
    o_ref[...] = acc

    @pl.when(j == pl.num_programs(1) - 1)
    def _():
        h = x_ref[...] + 0.5 * o_ref[...]
        if tail == "emit_norm":
            o_ref[...] = h
            no_ref[...] = _rms(h, fw_ref[...]).astype(BF16)
        else:
            o_ref[...] = _rms(h, fw_ref[...])


def _ffn(x, norm_w, wg, wu, wd, tail_w, *, tail):
    rows = x.shape[0]
    tm, tf = FFN_TILES[tail]
    tm = min(tm, rows)
    assert D_FF % tf == 0
    row_block = pl.BlockSpec((tm, D_MODEL), lambda i, j: (i, 0))
    out_specs = [row_block]
    out_shape = [jax.ShapeDtypeStruct((rows, D_MODEL), F32)]
    if tail == "emit_norm":
        out_specs.append(row_block)
        out_shape.append(jax.ShapeDtypeStruct((rows, D_MODEL), BF16))
    sub = min(256, tf)
    est = (2 * 2 * tm * D_MODEL * 4 + tm * D_MODEL * 2 + 2 * 3 * D_MODEL * tf * 2
           + 2 * tm * D_MODEL * 4 + 3 * tm * sub * 4 + (2 * tm * D_MODEL * 2 if tail == "emit_norm" else 0))
    return pl.pallas_call(
        functools.partial(_ffn_kernel, tail=tail, sub=sub),
        grid=(rows // tm, D_FF // tf),
        in_specs=[
            row_block,
            pl.BlockSpec((1, D_MODEL), lambda i, j: (0, 0)),
            pl.BlockSpec((D_MODEL, tf), lambda i, j: (0, j)),
            pl.BlockSpec((D_MODEL, tf), lambda i, j: (0, j)),
            pl.BlockSpec((tf, D_MODEL), lambda i, j: (j, 0)),
            pl.BlockSpec((1, D_MODEL), lambda i, j: (0, 0)),
        ],
        out_specs=out_specs,
        out_shape=out_shape,
        scratch_shapes=[pltpu.VMEM((tm, D_MODEL), BF16)],
        compiler_params=pltpu.CompilerParams(
            dimension_semantics=("parallel", "arbitrary"), vmem_limit_bytes=_vmem_limit(est)),
        name="ffn" if tail == "emit_norm" else "ffn_final",
    )(x, norm_w, wg, wu, wd, tail_w)


def _rope_kernel(inv_ref, cos_ref, sin_ref, *, seq, pos0):
    tr = cos_ref.shape[0]
    row = pl.program_id(0) * tr + lax.broadcasted_iota(jnp.int32, (tr, 1), 0)
    ang = ((row & (seq - 1)) + pos0).astype(F32) * inv_ref[...]
    cos_ref[...] = jnp.cos(ang)
    sin_ref[...] = jnp.sin(ang)


def _rope_tables(inv_freq, *, length, seq, pos0):
    tr = min(512, length)
    spec = pl.BlockSpec((tr, ROPE_HALF), lambda i: (i, 0))
    return pl.pallas_call(
        functools.partial(_rope_kernel, seq=seq, pos0=pos0),
        grid=(length // tr,),
        in_specs=[pl.BlockSpec((1, ROPE_HALF), lambda i: (0, 0))],
        out_specs=[spec, spec],
        out_shape=[jax.ShapeDtypeStruct((length, ROPE_HALF), F32)] * 2,
        name="rope",
    )(inv_freq)


def _proj_kernel(n_ref, w_ref, cos_ref, sin_ref, o_ref, *, tn):
    j = pl.program_id(1)
    n_q = RET_QK // tn

    @pl.when(j < 2 * n_q)
    def _():
        acc = jnp.dot(n_ref[...], w_ref[...], preferred_element_type=F32)
        cos = cos_ref[...]
        sin = sin_ref[...]
        scale = jnp.where(j < n_q, 1.0, RET_DK ** -0.5).astype(F32)
        for lo in range(0, tn, RET_DK):
            x1 = acc[:, lo:lo + ROPE_HALF]
            x2 = acc[:, lo + ROPE_HALF:lo + RET_DK]
            o_ref[:, lo:lo + ROPE_HALF] = ((x1 * cos - x2 * sin) * scale).astype(BF16)
            o_ref[:, lo + ROPE_HALF:lo + RET_DK] = ((x1 * sin + x2 * cos) * scale).astype(BF16)

    @pl.when(j >= 2 * n_q)
    def _():
        o_ref[...] = jnp.dot(n_ref[...], w_ref[...], preferred_element_type=F32).astype(BF16)


def _proj(n, w_all, inv_freq, *, seq, pos0):
    rows = n.shape[0]
    tm = min(1024, rows)
    tn = 2048
    assert seq & (seq - 1) == 0 and RET_QK % tn == 0 and PROJ_W % tn == 0
    table_len = max(seq, tm)
    cos, sin = _rope_tables(inv_freq, length=table_len, seq=seq, pos0=pos0)
    n_tab = table_len // tm
    est = (2 * tm * D_MODEL * 2 + 2 * D_MODEL * tn * 2 + 2 * tm * tn * 2 + 2 * tm * tn * 4
           + 4 * tm * LANES * 4)
    return pl.pallas_call(
        functools.partial(_proj_kernel, tn=tn),
        grid=(rows // tm, PROJ_W // tn),
        in_specs=[
            pl.BlockSpec((tm, D_MODEL), lambda i, j: (i, 0)),
            pl.BlockSpec((D_MODEL, tn), lambda i, j: (0, j)),
            pl.BlockSpec((tm, ROPE_HALF), lambda i, j: (i % n_tab, 0)),
            pl.BlockSpec((tm, ROPE_HALF), lambda i, j: (i % n_tab, 0)),
        ],
        out_specs=pl.BlockSpec((tm, tn), lambda i, j: (i, j)),
        out_shape=jax.ShapeDtypeStruct((rows, PROJ_W), BF16),
        compiler_params=pltpu.CompilerParams(
            dimension_semantics=("parallel", "arbitrary"), vmem_limit_bytes=_vmem_limit(est)),
        name="proj",
    )(n, w_all, cos, sin)


def _gates_kernel(n_ref, w_ref, wdtt_ref, o_ref, dtt_ref, *, batched_dtt):
    @pl.when(pl.program_id(1) == 0)
    def _():
        dtt = _dot_nt(wdtt_ref[...], n_ref[...])
        if batched_dtt:
            dtt_ref[0] = dtt
        else:
            dtt_ref[...] = dtt

    o_ref[...] = jnp.dot(n_ref[...], w_ref[...], preferred_element_type=F32).astype(BF16)


def _gates(n, w_gates, w_dtt, *, batch, seq):
    rows = n.shape[0]
    tm = min(1024, rows)
    tn = 2048
    batched_dtt = tm <= seq
    if batched_dtt:
        per_seq = seq // tm
        dtt_spec = pl.BlockSpec((1, DT_PAD, tm), lambda i, j: (i // per_seq, 0, i % per_seq))
        dtt_shape = jax.ShapeDtypeStruct((batch, DT_PAD, seq), F32)
    else:
        dtt_spec = pl.BlockSpec((DT_PAD, tm), lambda i, j: (0, i))
        dtt_shape = jax.ShapeDtypeStruct((DT_PAD, rows), F32)
    est = (2 * tm * D_MODEL * 2 + 2 * D_MODEL * tn * 2 + 2 * tm * tn * 2 + 2 * tm * tn * 4
           + 2 * D_MODEL * DT_PAD * 2 + 4 * tm * DT_PAD * 4)
    gates, dtt = pl.pallas_call(
        functools.partial(_gates_kernel, batched_dtt=batched_dtt),
        grid=(rows // tm, 2 * D_MODEL // tn),
        in_specs=[
            pl.BlockSpec((tm, D_MODEL), lambda i, j: (i, 0)),
            pl.BlockSpec((D_MODEL, tn), lambda i, j: (0, j)),
            pl.BlockSpec((DT_PAD, D_MODEL), lambda i, j: (0, 0)),
        ],
        out_specs=[pl.BlockSpec((tm, tn), lambda i, j: (i, j)), dtt_spec],
        out_shape=[jax.ShapeDtypeStruct((rows, 2 * D_MODEL), BF16), dtt_shape],
        compiler_params=pltpu.CompilerParams(
            dimension_semantics=("parallel", "arbitrary"), vmem_limit_bytes=_vmem_limit(est)),
        name="gates",
    )(n, w_gates, w_dtt)
    if not batched_dtt:
        dtt = dtt.reshape(DT_PAD, batch, seq).transpose(1, 0, 2)
    return gates, dtt


def _ret_kernel(*refs, chunk, has_init):
    if has_init:
        q_ref, k_ref, v_ref, g_ref, nw_ref, s0_ref, o_ref, s_ref, mask_ref = refs
    else:
        q_ref, k_ref, v_ref, g_ref, nw_ref, o_ref, s_ref, mask_ref = refs
    c = pl.program_id(1)
    log_decay = [math.log1p(-(2.0 ** (-5.0 - hd))) for hd in range(RET_HEADS)]

    @pl.when((pl.program_id(0) == 0) & (c == 0))
    def _():
        ii = lax.broadcasted_iota(jnp.int32, (chunk, chunk), 0)
        jj = lax.broadcasted_iota(jnp.int32, (chunk, chunk), 1)
        causal = ii >= jj
        diff = jnp.where(causal, ii - jj, 0).astype(F32)
        for hd in range(RET_HEADS):
            mask_ref[hd] = jnp.where(causal, jnp.exp(diff * log_decay[hd]), 0.0)

    @pl.when(c == 0)
    def _():
        if has_init:
            s_ref[...] = s0_ref[...]
        else:
            s_ref[...] = jnp.zeros_like(s_ref)

    idx = lax.broadcasted_iota(jnp.int32, (chunk, 1), 0).astype(F32)
    for r0 in range(0, q_ref.shape[1], chunk):
        rows = slice(r0, r0 + chunk)
        for hd in range(RET_HEADS):
            lg = log_decay[hd]
            qk_cols = slice(hd * RET_DK, (hd + 1) * RET_DK)
            v_cols = slice(hd * RET_DV, (hd + 1) * RET_DV)
            decay_mask = mask_ref[hd]
            inner_decay = jnp.exp((idx + 1.0) * lg)
            state_decay = jnp.exp((chunk - 1.0 - idx) * lg)
            chunk_decay = math.exp(chunk * lg)
            q = q_ref[0, rows, qk_cols]
            k = k_ref[0, rows, qk_cols]
            v = v_ref[0, rows, v_cols]
            s = s_ref[0, hd]
            scores = _dot_nt(q, k) * decay_mask
            o = _dot(scores, v) + _dot(q, s) * inner_decay
            s_ref[0, hd] = s * chunk_decay + _dot_tn(k.astype(F32) * state_decay, v)
            mu = jnp.mean(o, axis=-1, keepdims=True)
            var = jnp.mean(jnp.square(o - mu), axis=-1, keepdims=True)
            on = (o - mu) * lax.rsqrt(var + EPS) * nw_ref[:, v_cols]
            gate = g_ref[0, rows, v_cols].astype(F32)
            o_ref[0, rows, v_cols] = (_silu(gate) * on).astype(BF16)


def _retention(proj3, norm_w, s0, *, chunk, block):
    b, seq, _ = proj3.shape
    has_init = s0 is not None
    assert block % chunk == 0 and seq % block == 0
    state_block = pl.BlockSpec((1, RET_HEADS, RET_DK, RET_DV), lambda bi, c: (bi, 0, 0, 0))
    in_specs = [
        pl.BlockSpec((1, block, RET_QK), lambda bi, c: (bi, c, OFF_Q // RET_QK)),
        pl.BlockSpec((1, block, RET_QK), lambda bi, c: (bi, c, OFF_K // RET_QK)),
        pl.BlockSpec((1, block, RET_V), lambda bi, c: (bi, c, OFF_V // RET_V)),
        pl.BlockSpec((1, block, RET_V), lambda bi, c: (bi, c, OFF_G // RET_V)),
        pl.BlockSpec((1, RET_V), lambda bi, c: (0, 0)),
    ]
    args = [proj3, proj3, proj3, proj3, norm_w]
    if has_init:
        in_specs.append(state_block)
        args.append(s0)
    state_bytes = RET_HEADS * RET_DK * RET_DV * 4
    est = (2 * block * (2 * RET_QK + 3 * RET_V) * 2 + (4 if has_init else 2) * state_bytes
           + 12 * chunk * max(chunk, RET_DV) * 4 + RET_HEADS * chunk * chunk * 4)
    return pl.pallas_call(
        functools.partial(_ret_kernel, chunk=chunk, has_init=has_init),
        grid=(b, seq // block),
        in_specs=in_specs,
        out_specs=[pl.BlockSpec((1, block, RET_V), lambda bi, c: (bi, c, 0)), state_block],
        out_shape=[jax.ShapeDtypeStruct((b, seq, RET_V), BF16),
                   jax.ShapeDtypeStruct((b, RET_HEADS, RET_DK, RET_DV), F32)],
        scratch_shapes=[pltpu.VMEM((RET_HEADS, chunk, chunk), F32)],
        compiler_params=pltpu.CompilerParams(
            dimension_semantics=("arbitrary", "arbitrary"), vmem_limit_bytes=_vmem_limit(est)),
        name="retention",
    )(*args)


def _expand_table():
    e = np.zeros((SSM_GROUPS, 2 * DT_PAD, SSM_GW), np.float32)
    for g in range(SSM_GROUPS):
        for col in range(SSM_GW):
            head = g * SSM_HPG + col // SSM_HEAD_DIM
            for piece in range(3):
                e[g, piece * SSM_HEADS + head, col] = 1.0
    return jnp.asarray(e, BF16)


def _ssd_kernel(*refs, chunk, has_init, has_dt):
    refs = iter(refs)
    take = lambda k: [next(refs) for _ in range(k)]
    z_ref, x_ref, b_ref, c_ref = take(4)
    dt_ref, = take(1) if has_dt else (None,)
    dtt_ref, cw_ref, cb_ref, dtb_ref, dtbt_ref, al_ref, alt_ref, dsk_ref, nw_ref, ex_ref = take(10)
    h0_ref, cs0_ref = take(2) if has_init else (None, None)
    y_ref, hout_ref, csout_ref, ht_ref, xc_ref = take(5)
    fix_ref, ext_ref = take(2) if has_init else (None, None)
    c = pl.program_id(1)
    taps = SSM_CONV - 1
    kk = CONV_TAIL + chunk

    @pl.when(c == 0)
    def _():
        for g in range(SSM_GROUPS):
            if has_init:
                blk = h0_ref[0, g * SSM_HPG:(g + 1) * SSM_HPG].reshape(SSM_GW, SSM_STATE)
                ht_ref[g] = blk.T
            else:
                ht_ref[g] = jnp.zeros((SSM_STATE, SSM_GW), F32)
        xc_ref[0:CONV_TAIL, :] = jnp.zeros((CONV_TAIL, SSM_XBC), BF16)
        if has_init:
            ext_ref[...] = jnp.zeros_like(ext_ref)
            ext_ref[SUBLANES - taps:SUBLANES, :] = cs0_ref[0]
            fix = jnp.zeros((SUBLANES, SSM_XBC), F32)
            for i in range(taps):
                fix = fix + ext_ref[SUBLANES - taps + i:2 * SUBLANES - taps + i, :] * cw_ref[i:i + 1, :]
            fix_ref[...] = fix

    if has_init:
        @pl.when(c == 1)
        def _():
            fix_ref[...] = jnp.zeros_like(fix_ref)

    block = x_ref.shape[1]
    xc_ref[CONV_TAIL:CONV_TAIL + block, 0:SSM_INNER] = x_ref[0]
    xc_ref[CONV_TAIL:CONV_TAIL + block, SSM_INNER:SSM_INNER + SSM_BC] = b_ref[0]
    xc_ref[CONV_TAIL:CONV_TAIL + block, SSM_INNER + SSM_BC:SSM_XBC] = c_ref[0]

    group = taps * SUBLANES
    si = lax.broadcasted_iota(jnp.int32, (taps * chunk, kk), 0)
    sj = lax.broadcasted_iota(jnp.int32, (taps * chunk, kk), 1)
    tile = (si * 2731) >> 16
    rem = si - group * tile
    src_row = tile * SUBLANES + (rem & (SUBLANES - 1)) + (rem >> 3) + (CONV_TAIL - taps)
    shift = jnp.where(sj == src_row, 1.0, 0.0).astype(BF16)

    def conv(src_ref, r0, lo, col0, width):
        cur = src_ref[0, r0:r0 + chunk, lo:lo + width]
        lo = col0 + lo
        prev = jnp.dot(shift, xc_ref[r0:r0 + kk, lo:lo + width], preferred_element_type=F32)
        prev = prev.reshape(chunk // SUBLANES, taps, SUBLANES, width)
        half = cb_ref[:, lo:lo + width] + cur.astype(F32) * cw_ref[taps:taps + 1, lo:lo + width]
        for i in range(taps):
            half = half + prev[:, i].reshape(chunk, width) * cw_ref[i:i + 1, lo:lo + width]
        if has_init and r0 == 0:
            half = jnp.concatenate([half[:SUBLANES] + fix_ref[:, lo:lo + width], half[SUBLANES:]], axis=0)
        return half * jnp.tanh(half) + half

    a_row = -jnp.exp(al_ref[...]) * LOG2E
    a_col = -jnp.exp(alt_ref[...]) * LOG2E
    ii = lax.broadcasted_iota(jnp.int32, (chunk, chunk), 0)
    jj = lax.broadcasted_iota(jnp.int32, (chunk, chunk), 1)
    tri = ii >= jj
    lower = tri.astype(BF16)
    upper = (ii <= jj).astype(BF16)
    first_half = lax.broadcasted_iota(jnp.int32, (chunk, LANES), 1) < SSM_HEADS
    lane_head = lax.broadcasted_iota(jnp.int32, (chunk, LANES), 1) < SSM_HEAD_DIM
    head_a = jnp.where(lane_head, 1.0, 0.0).astype(BF16)
    head_b = jnp.where(lane_head, 0.0, 1.0).astype(BF16)

    def pieces(v):
        hi, mid, lo = _split3(v)
        return jnp.concatenate([jnp.where(first_half, hi, mid), lo], axis=1)

    def one_chunk(r0):
        rows = slice(r0, r0 + chunk)
        dtt_raw = dtt_ref[0, :, rows]
        dt_raw = dt_ref[0, rows, :] if has_dt else dtt_raw.T
        dt = _softplus(dt_raw + dtb_ref[...])
        dtt = _softplus(dtt_raw + dtbt_ref[...])
        cum = sum(jnp.dot(lower, p, preferred_element_type=F32) for p in _split3(dt * a_row))
        cumt = sum(jnp.dot(p, upper, preferred_element_type=F32) for p in _split3(dtt * a_col))
        dt_p = pieces(dt)
        cum_p = pieces(cum)

        def stage_a(g):
            cols = slice(g * SSM_GW, (g + 1) * SSM_GW)
            xs = conv(x_ref, r0, g * SSM_GW, 0, SSM_GW)
            bm = conv(b_ref, r0, g * SSM_STATE, SSM_INNER, SSM_STATE)
            cm = conv(c_ref, r0, g * SSM_STATE, SSM_INNER + SSM_BC, SSM_STATE)
            cbm = jnp.where(tri, _dot_nt(cm, bm), 0.0)
            dt_e = jnp.dot(dt_p, ex_ref[g], preferred_element_type=F32)
            cum_e = jnp.dot(cum_p, ex_ref[g], preferred_element_type=F32)
            e_all = jnp.exp2(cum_e)
            xdt = xs * dt_e
            xd = xdt * jnp.exp2(cum_e[chunk - 1:chunk, :] - cum_e)
            htg = ht_ref[g]
            y_base = _dot(cm, htg) * e_all + xs * dsk_ref[:, cols]
            ht_ref[g] = htg * e_all[chunk - 1:chunk, :] + _dot_tn(bm, xd)
            return y_base, cbm, xdt.astype(BF16)

        nxt = stage_a(0)
        for g in range(SSM_GROUPS):
            y_base, cbm, xdt16 = nxt
            if g + 1 < SSM_GROUPS:
                nxt = stage_a(g + 1)
            cols = slice(g * SSM_GW, (g + 1) * SSM_GW)
            y_parts = []
            for s in range(SSM_GW // LANES):
                rhs = xdt16[:, s * LANES:(s + 1) * LANES]
                ms = []
                for j in (g * SSM_HPG + 2 * s, g * SSM_HPG + 2 * s + 1):
                    seg = (jnp.broadcast_to(cum[:, j:j + 1], (chunk, chunk))
                           - jnp.broadcast_to(cumt[j:j + 1, :], (chunk, chunk)))
                    ms.append((cbm * jnp.exp2(jnp.minimum(seg, 0.0))).astype(BF16))
                if chunk % LANES == 0:
                    rhs2 = jnp.concatenate([rhs * head_a, rhs * head_b], axis=0)
                    y_parts.append(jnp.dot(jnp.concatenate(ms, axis=1), rhs2, preferred_element_type=F32))
                else:
                    y_parts.append(jnp.dot(ms[0], rhs * head_a, preferred_element_type=F32)
                                   + jnp.dot(ms[1], rhs * head_b, preferred_element_type=F32))
            y = jnp.concatenate(y_parts, axis=1) + y_base
            y = y * _silu(z_ref[0, rows, cols].astype(F32))
            y_ref[0, rows, cols] = _rms(y, nw_ref[:, cols]).astype(BF16)

    for r0 in range(0, block, chunk):
        one_chunk(r0)

    xc_ref[0:CONV_TAIL, :] = xc_ref[block:block + CONV_TAIL, :]

    @pl.when(c == pl.num_programs(1) - 1)
    def _():
        csout_ref[0] = xc_ref[block:block + CONV_TAIL, :].astype(F32)[CONV_TAIL - taps:CONV_TAIL, :]
        for g in range(SSM_GROUPS):
            hout_ref[0, g * SSM_HPG:(g + 1) * SSM_HPG] = ht_ref[g].T.reshape(SSM_HPG, SSM_HEAD_DIM, SSM_STATE)


def _ssd(proj3, dtt3, conv_w, conv_b, dt_bias, a_log, d_skip, norm_w, h0, cs0, *, chunk, block):
    b, seq, _ = proj3.shape
    has_init = h0 is not None
    has_dt = chunk % LANES != 0
    assert chunk % SUBLANES == 0 and chunk >= SUBLANES and block % chunk == 0 and seq % block == 0
    assert chunk & (chunk - 1) == 0
    dtb = jnp.tile(dt_bias.astype(F32), 2)
    al = jnp.tile(a_log.astype(F32), 2)
    dsk = jnp.repeat(d_skip.astype(F32), SSM_HEAD_DIM).reshape(1, SSM_INNER)
    const = lambda *shape: pl.BlockSpec(shape, lambda bi, c: (0,) * len(shape))
    in_specs = [
        pl.BlockSpec((1, block, SSM_INNER), lambda bi, c: (bi, c, OFF_Z // SSM_INNER)),
        pl.BlockSpec((1, block, SSM_INNER), lambda bi, c: (bi, c, OFF_X // SSM_INNER)),
        pl.BlockSpec((1, block, SSM_BC), lambda bi, c: (bi, c, OFF_B // SSM_BC)),
        pl.BlockSpec((1, block, SSM_BC), lambda bi, c: (bi, c, OFF_C // SSM_BC)),
    ]
    args = [proj3, proj3, proj3, proj3]
    if has_dt:
        in_specs.append(pl.BlockSpec((1, block, DT_PAD), lambda bi, c: (bi, c, 0)))
        args.append(dtt3.transpose(0, 2, 1))
    in_specs += [
        pl.BlockSpec((1, DT_PAD, block), lambda bi, c: (bi, 0, c)),
        const(SSM_CONV, SSM_XBC), const(1, SSM_XBC), const(1, DT_PAD), const(DT_PAD, 1),
        const(1, DT_PAD), const(DT_PAD, 1), const(1, SSM_INNER), const(1, SSM_INNER),
        const(SSM_GROUPS, 2 * DT_PAD, SSM_GW),
    ]
    args += [dtt3, 0.5 * conv_w.astype(F32), 0.5 * conv_b.astype(F32).reshape(1, SSM_XBC),
             dtb.reshape(1, DT_PAD), dtb.reshape(DT_PAD, 1), al.reshape(1, DT_PAD), al.reshape(DT_PAD, 1),
             dsk, norm_w.astype(F32).reshape(1, SSM_INNER), _expand_table()]
    h_block = pl.BlockSpec((1, SSM_HEADS, SSM_HEAD_DIM, SSM_STATE), lambda bi, c: (bi, 0, 0, 0))
    cs_block = pl.BlockSpec((1, SSM_CONV - 1, SSM_XBC), lambda bi, c: (bi, 0, 0))
    scratch = [pltpu.VMEM((SSM_GROUPS, SSM_STATE, SSM_GW), F32),
               pltpu.VMEM((CONV_TAIL + block, SSM_XBC), BF16)]
    if has_init:
        in_specs += [h_block, cs_block]
        args += [h0, cs0]
        scratch += [pltpu.VMEM((SUBLANES, SSM_XBC), F32), pltpu.VMEM((2 * SUBLANES, SSM_XBC), F32)]
    state_bytes = SSM_HEADS * SSM_HEAD_DIM * SSM_STATE * 4
    est = (2 * block * (3 * SSM_INNER + 2 * SSM_BC) * 2 + (5 if has_init else 3) * state_bytes
           + (block + CONV_TAIL) * SSM_XBC * 2 + 32 * chunk * max(chunk, SSM_GW) * 4 + 8 * 1024 * 1024)
    return pl.pallas_call(
        functools.partial(_ssd_kernel, chunk=chunk, has_init=has_init, has_dt=has_dt),
        grid=(b, seq // block),
        in_specs=in_specs,
        out_specs=[pl.BlockSpec((1, block, SSM_INNER), lambda bi, c: (bi, c, 0)), h_block, cs_block],
        out_shape=[jax.ShapeDtypeStruct((b, seq, SSM_INNER), BF16),
                   jax.ShapeDtypeStruct((b, SSM_HEADS, SSM_HEAD_DIM, SSM_STATE), F32),
                   jax.ShapeDtypeStruct((b, SSM_CONV - 1, SSM_XBC), F32)],
        scratch_shapes=scratch,
        compiler_params=pltpu.CompilerParams(
            dimension_semantics=("parallel", "arbitrary"), vmem_limit_bytes=_vmem_limit(est)),
        name="ssd",
    )(*args)


def _merge_kernel(ar_ref, as_ref, wr_ref, ws_ref, ga_ref, gb_ref, ba_ref, bb_ref, o_ref, *, sub):
    for lo in range(0, o_ref.shape[0], sub):
        rows = slice(lo, lo + sub)
        yr = jnp.dot(ar_ref[rows, :], wr_ref[...], preferred_element_type=F32)
        ys = jnp.dot(as_ref[rows, :], ws_ref[...], preferred_element_type=F32)
        ga = _sigmoid(ga_ref[rows, :].astype(F32) + ba_ref[...])
        gb = _sigmoid(gb_ref[rows, :].astype(F32) + bb_ref[...])
        o_ref[rows, :] = (ga * yr + gb * ys).astype(BF16)


def _merge(a_ret, a_ssm, w_ret, w_ssm, gates, b_gate):
    rows = a_ret.shape[0]
    tm = min(1024, rows)
    tn = 512
    nb = D_MODEL // tn
    est = 2 * 2 * tm * RET_V * 2 + 2 * 2 * RET_V * tn * 2 + 8 * tm * tn * 4
    return pl.pallas_call(
        functools.partial(_merge_kernel, sub=min(512, tm)),
        grid=(rows // tm, nb),
        in_specs=[
            pl.BlockSpec((tm, RET_V), lambda i, j: (i, 0)),
            pl.BlockSpec((tm, SSM_INNER), lambda i, j: (i, 0)),
            pl.BlockSpec((RET_V, tn), lambda i, j: (0, j)),
            pl.BlockSpec((SSM_INNER, tn), lambda i, j: (0, j)),
            pl.BlockSpec((tm, tn), lambda i, j: (i, j)),
            pl.BlockSpec((tm, tn), lambda i, j: (i, nb + j)),
            pl.BlockSpec((1, tn), lambda i, j: (0, j)),
            pl.BlockSpec((1, tn), lambda i, j: (0, nb + j)),
        ],
        out_specs=pl.BlockSpec((tm, tn), lambda i, j: (i, j)),
        out_shape=jax.ShapeDtypeStruct((rows, D_MODEL), BF16),
        compiler_params=pltpu.CompilerParams(
            dimension_semantics=("parallel", "arbitrary"), vmem_limit_bytes=_vmem_limit(est)),
        name="merge",
    )(a_ret, a_ssm, w_ret, w_ssm, gates, gates, b_gate, b_gate)


def _outproj_kernel(m_ref, w_ref, h_ref, o_ref):
    o_ref[...] = h_ref[...] + jnp.dot(m_ref[...], w_ref[...], preferred_element_type=F32)


def _outproj(m, w_out, h):
    rows = m.shape[0]
    tm = min(1024, rows)
    tn = 1024
    est = 2 * tm * D_MODEL * 2 + 2 * D_MODEL * tn * 2 + 5 * tm * tn * 4
    return pl.pallas_call(
        _outproj_kernel,
        grid=(rows // tm, D_MODEL // tn),
        in_specs=[
            pl.BlockSpec((tm, D_MODEL), lambda i, j: (i, 0)),
            pl.BlockSpec((D_MODEL, tn), lambda i, j: (0, j)),
            pl.BlockSpec((tm, tn), lambda i, j: (i, j)),
        ],
        out_specs=pl.BlockSpec((tm, tn), lambda i, j: (i, j)),
        out_shape=jax.ShapeDtypeStruct((rows, D_MODEL), F32),
        compiler_params=pltpu.CompilerParams(
            dimension_semantics=("parallel", "arbitrary"), vmem_limit_bytes=_vmem_limit(est)),
        name="outproj",
    )(m, w_out, h)


def _layer(x, s_ret, s_ssm, s_conv, pos0, p, *, ret_chunk, ssd_chunk):
    b, seq, _ = x.shape
    rows = b * seq
    h, n = _ffn(x.reshape(rows, D_MODEL), p["norm_ffn1"], p["ffn1_w_gate"], p["ffn1_w_up"], p["ffn1_w_down"],
                p["norm_mix"], tail="emit_norm")
    proj = _proj(n, p["w_all"], p["inv_freq"], seq=seq, pos0=pos0)
    gates, dtt3 = _gates(n, p["w_gates"], p["w_dtt"], batch=b, seq=seq)
    proj3 = proj.reshape(b, seq, PROJ_W)
    a_ret, ret_new = _retention(proj3, p["ret_norm_w"], s_ret, chunk=min(ret_chunk, seq),
                                block=min(RET_CHUNKS_PER_STEP * ret_chunk, seq))
    a_ssm, ssm_new, conv_new = _ssd(proj3, dtt3, p["conv_w"], p["conv_b"], p["dt_bias"], p["a_log"],
                                    p["d_skip"], p["ssm_norm_w"], s_ssm, s_conv,
                                    chunk=min(ssd_chunk, seq), block=min(SSD_CHUNKS_PER_STEP * ssd_chunk, seq))
    m = _merge(a_ret.reshape(rows, RET_V), a_ssm.reshape(rows, SSM_INNER), p["w_out_ret"], p["w_out_ssm"],
               gates, p["b_gate"])
    h2 = _outproj(m, p["w_out"], h)
    y, = _ffn(h2, p["norm_ffn2"], p["ffn2_w_gate"], p["ffn2_w_up"], p["ffn2_w_down"], p["norm_final"],
              tail="final_norm")
    return y.reshape(b, seq, D_MODEL), ret_new, ssm_new, conv_new


def kernel(x_prompt, x_sample, state_ret, state_ssm, state_conv, norm_ffn1, ffn1_w_gate, ffn1_w_up, ffn1_w_down, norm_mix, w_in, b_gate, ret_norm_w, w_out_ret, conv_w, conv_b, dt_bias, a_log, d_skip, ssm_norm_w, w_out_ssm, w_out, norm_ffn2, ffn2_w_gate, ffn2_w_up, ffn2_w_down, norm_final):
    row = lambda v: v.astype(F32).reshape(1, -1)
    dt_lo = 2 * RET_QK + 2 * RET_V + SSM_INNER + SSM_XBC
    p = dict(
        norm_ffn1=row(norm_ffn1), norm_mix=row(norm_mix), norm_ffn2=row(norm_ffn2), norm_final=row(norm_final),
        ffn1_w_gate=ffn1_w_gate.astype(BF16), ffn1_w_up=ffn1_w_up.astype(BF16), ffn1_w_down=ffn1_w_down.astype(BF16),
        ffn2_w_gate=ffn2_w_gate.astype(BF16), ffn2_w_up=ffn2_w_up.astype(BF16), ffn2_w_down=ffn2_w_down.astype(BF16),
        w_all=w_in.astype(BF16), w_gates=w_in[:, dt_lo + SSM_HEADS:].astype(BF16),
        w_dtt=jnp.tile(w_in[:, dt_lo:dt_lo + SSM_HEADS].T, (2, 1)).astype(BF16),
        inv_freq=(ROPE_BASE ** (-jnp.arange(ROPE_HALF, dtype=F32) / ROPE_HALF)).reshape(1, ROPE_HALF),
        b_gate=row(b_gate), ret_norm_w=row(ret_norm_w),
        w_out_ret=w_out_ret.astype(BF16), w_out_ssm=w_out_ssm.astype(BF16), w_out=w_out.astype(BF16),
        conv_w=conv_w, conv_b=conv_b, dt_bias=dt_bias, a_log=a_log, d_skip=d_skip, ssm_norm_w=ssm_norm_w,
    )
    y_p, ret_p, ssm_p, conv_p = _layer(x_prompt, None, None, None, 0, p, ret_chunk=256, ssd_chunk=128)
    y_s, ret_s, ssm_s, conv_s = _layer(x_sample, state_ret.astype(F32), state_ssm.astype(F32),
                                       state_conv.astype(F32), PAST_LEN, p, ret_chunk=256, ssd_chunk=128)
    return (y_p, y_s, ret_p, ssm_p, conv_p, ret_s, ssm_s, conv_s)
```

```python
import functools
import math

import numpy as np
import jax
import jax.numpy as jnp
from jax import lax
from jax.experimental import pallas as pl
from jax.experimental.pallas import tpu as pltpu

F32 = jnp.float32
BF16 = jnp.bfloat16

D_MODEL = 2048
PAST_LEN = 1024
RET_HEADS = 8
RET_DK = D_MODEL // RET_HEADS
RET_DV = 2 * D_MODEL // RET_HEADS
RET_QK = RET_HEADS * RET_DK
RET_V = RET_HEADS * RET_DV
ROPE_BASE = 10000.0
ROPE_HALF = RET_DK // 2
SSM_INNER = 2 * D_MODEL
SSM_HEAD_DIM = 64
SSM_HEADS = SSM_INNER // SSM_HEAD_DIM
SSM_GROUPS = 8
SSM_HPG = SSM_HEADS // SSM_GROUPS
SSM_STATE = 128
SSM_CONV = 4
SSM_BC = SSM_GROUPS * SSM_STATE
SSM_XBC = SSM_INNER + 2 * SSM_BC
SSM_GW = SSM_INNER // SSM_GROUPS
D_FF = 5632
EPS = 1e-6
LOG2E = math.log2(math.e)

LANES = 128
SUBLANES = 8
BF16_SUBLANES = 16
V7X_VMEM_BYTES = 64 * 1024 * 1024

OFF_Q = 0
OFF_K = OFF_Q + RET_QK
OFF_V = OFF_K + RET_QK
OFF_G = OFF_V + RET_V
OFF_Z = OFF_G + RET_V
OFF_X = OFF_Z + SSM_INNER
OFF_B = OFF_X + SSM_INNER
OFF_C = OFF_B + SSM_BC
PROJ_W = OFF_C + SSM_BC
DT_PAD = LANES
assert DT_PAD == 2 * SSM_HEADS
CONV_TAIL = BF16_SUBLANES
FFN_TILES = {"emit_norm": (512, 512), "final_norm": (512, 512)}
SSD_CHUNKS_PER_STEP = 2
RET_CHUNKS_PER_STEP = 2


def _vmem_limit(nbytes):
    return int(min(V7X_VMEM_BYTES - 4 * 1024 * 1024, max(nbytes, 16 * 1024 * 1024)))


def _dot(a, b):
    return jnp.dot(a.astype(BF16), b.astype(BF16), preferred_element_type=F32)


def _dot_nt(a, b):
    return lax.dot_general(a.astype(BF16), b.astype(BF16), (((1,), (1,)), ((), ())),
                           preferred_element_type=F32)


def _dot_tn(a, b):
    t = a.shape[0]
    pad = (-t) % LANES
    a = a.astype(F32)
    b = b.astype(BF16)
    if pad:
        a = jnp.concatenate([a, jnp.zeros((pad, a.shape[1]), F32)], axis=0)
        b = jnp.concatenate([b, jnp.zeros((pad, b.shape[1]), BF16)], axis=0)
    return jnp.dot(a.T.astype(BF16), b, preferred_element_type=F32)


def _rms(x, w):
    return x * lax.rsqrt(jnp.mean(x * x, axis=-1, keepdims=True) + EPS) * w


def _sigmoid(x):
    return 0.5 * jnp.tanh(0.5 * x) + 0.5


def _silu(x):
    h = 0.5 * x
    return h * jnp.tanh(h) + h


def _softplus(x):
    return jnp.maximum(x, 0.0) + jnp.log1p(jnp.exp(-jnp.abs(x)))


def _split3(x):
    hi = x.astype(BF16)
    r = x - hi.astype(F32)
    mid = r.astype(BF16)
    lo = (r - mid.astype(F32)).astype(BF16)
    return hi, mid, lo


def _ffn_kernel(x_ref, nw_ref, wg0_ref, wg1_ref, wu0_ref, wu1_ref, wd0_ref, wd1_ref, fw_ref, *refs, tail, sub):
    if tail == "emit_norm":
        o_ref, no_ref, n_ref = refs
    else:
        o_ref, n_ref = refs
    j = pl.program_id(1)

    @pl.when(j == 0)
    def _():
        n_ref[...] = _rms(x_ref[...], nw_ref[...]).astype(BF16)
        o_ref[...] = jnp.zeros_like(o_ref)

    kh = wg0_ref.shape[0]
    n0 = n_ref[:, :kh]
    n1 = n_ref[:, kh:]
    acc = o_ref[...]
    for wd_ref, base in ((wd0_ref, 0), (wd1_ref, wd0_ref.shape[0])):
        for lo in range(0, wd_ref.shape[0], sub):
            cols = slice(base + lo, base + lo + sub)
            g = (jnp.dot(n0, wg0_ref[:, cols], preferred_element_type=F32)
                 + jnp.dot(n1, wg1_ref[:, cols], preferred_element_type=F32))
            u = (jnp.dot(n0, wu0_ref[:, cols], preferred_element_type=F32)
                 + jnp.dot(n1, wu1_ref[:, cols], preferred_element_type=F32))
            a = (_silu(g) * u).astype(BF16)
            acc = acc + jnp.dot(a, wd_ref[lo:lo + sub, :], preferred_element_type=F32)
    o_ref[...] = acc

    @pl.when(j == pl.num_programs(1) - 1)
    def _():
        h = x_ref[...] + 0.5 * o_ref[...]
        if tail == "emit_norm":
            o_ref[...] = h
            no_ref[...] = _rms(h, fw_ref[...]).astype(BF16)
        else:
            o_ref[...] = _rms(h, fw_ref[...])


def _ffn(x, norm_w, wg, wu, wd, tail_w, *, tail):
    rows = x.shape[0]
    tm, tf = FFN_TILES[tail]
    tm = min(tm, rows)
    assert D_FF % tf == 0
    row_block = pl.BlockSpec((tm, D_MODEL), lambda i, j: (i, 0))
    out_specs = [row_block]
    out_shape = [jax.ShapeDtypeStruct((rows, D_MODEL), F32)]
    if tail == "emit_norm":
        out_specs.append(row_block)
        out_shape.append(jax.ShapeDtypeStruct((rows, D_MODEL), BF16))
    sub = min(256, tf)
    est = (2 * 2 * tm * D_MODEL * 4 + tm * D_MODEL * 2 + 2 * 3 * D_MODEL * tf * 2
           + 2 * tm * D_MODEL * 4 + 3 * tm * sub * 4 + (2 * tm * D_MODEL * 2 if tail == "emit_norm" else 0))
    return pl.pallas_call(
        functools.partial(_ffn_kernel, tail=tail, sub=sub),
        grid=(rows // tm, D_FF // tf),
        in_specs=[
            row_block,
            pl.BlockSpec((1, D_MODEL), lambda i, j: (0, 0)),
            pl.BlockSpec((D_MODEL // 2, tf), lambda i, j: (0, j)),
            pl.BlockSpec((D_MODEL // 2, tf), lambda i, j: (1, j)),
            pl.BlockSpec((D_MODEL // 2, tf), lambda i, j: (0, j)),
            pl.BlockSpec((D_MODEL // 2, tf), lambda i, j: (1, j)),
            pl.BlockSpec((tf // 2, D_MODEL), lambda i, j: (2 * j, 0)),
            pl.BlockSpec((tf // 2, D_MODEL), lambda i, j: (2 * j + 1, 0)),
            pl.BlockSpec((1, D_MODEL), lambda i, j: (0, 0)),
        ],
        out_specs=out_specs,
        out_shape=out_shape,
        scratch_shapes=[pltpu.VMEM((tm, D_MODEL), BF16)],
        compiler_params=pltpu.CompilerParams(
            dimension_semantics=("parallel", "arbitrary"), vmem_limit_bytes=_vmem_limit(est)),
        name="ffn" if tail == "emit_norm" else "ffn_final",
    )(x, norm_w, wg, wg, wu, wu, wd, wd, tail_w)


def _rope_kernel(inv_ref, cos_ref, sin_ref, *, seq, pos0):
    tr = cos_ref.shape[0]
    row = pl.program_id(0) * tr + lax.broadcasted_iota(jnp.int32, (tr, 1), 0)
    ang = ((row & (seq - 1)) + pos0).astype(F32) * inv_ref[...]
    cos_ref[...] = jnp.cos(ang)
    sin_ref[...] = jnp.sin(ang)


def _rope_tables(inv_freq, *, length, seq, pos0):
    tr = min(512, length)
    spec = pl.BlockSpec((tr, ROPE_HALF), lambda i: (i, 0))
    return pl.pallas_call(
        functools.partial(_rope_kernel, seq=seq, pos0=pos0),
        grid=(length // tr,),
        in_specs=[pl.BlockSpec((1, ROPE_HALF), lambda i: (0, 0))],
        out_specs=[spec, spec],
        out_shape=[jax.ShapeDtypeStruct((length, ROPE_HALF), F32)] * 2,
        name="rope",
    )(inv_freq)


def _proj_kernel(n_ref, w_ref, cos_ref, sin_ref, o_ref, *, tn):
    j = pl.program_id(1)
    n_q = RET_QK // tn

    @pl.when(j < 2 * n_q)
    def _():
        acc = jnp.dot(n_ref[...], w_ref[...], preferred_element_type=F32)
        cos = cos_ref[...]
        sin = sin_ref[...]
        scale = jnp.where(j < n_q, 1.0, RET_DK ** -0.5).astype(F32)
        for lo in range(0, tn, RET_DK):
            x1 = acc[:, lo:lo + ROPE_HALF]
            x2 = acc[:, lo + ROPE_HALF:lo + RET_DK]
            o_ref[:, lo:lo + ROPE_HALF] = ((x1 * cos - x2 * sin) * scale).astype(BF16)
            o_ref[:, lo + ROPE_HALF:lo + RET_DK] = ((x1 * sin + x2 * cos) * scale).astype(BF16)

    @pl.when(j >= 2 * n_q)
    def _():
        o_ref[...] = jnp.dot(n_ref[...], w_ref[...], preferred_element_type=F32).astype(BF16)


def _proj(n, w_all, inv_freq, *, seq, pos0):
    rows = n.shape[0]
    tm = min(1024, rows)
    tn = 2048
    assert seq & (seq - 1) == 0 and RET_QK % tn == 0 and PROJ_W % tn == 0
    table_len = max(seq, tm)
    cos, sin = _rope_tables(inv_freq, length=table_len, seq=seq, pos0=pos0)
    n_tab = table_len // tm
    est = (2 * tm * D_MODEL * 2 + 2 * D_MODEL * tn * 2 + 2 * tm * tn * 2 + 2 * tm * tn * 4
           + 4 * tm * LANES * 4)
    return pl.pallas_call(
        functools.partial(_proj_kernel, tn=tn),
        grid=(rows // tm, PROJ_W // tn),
        in_specs=[
            pl.BlockSpec((tm, D_MODEL), lambda i, j: (i, 0)),
            pl.BlockSpec((D_MODEL, tn), lambda i, j: (0, j)),
            pl.BlockSpec((tm, ROPE_HALF), lambda i, j: (i % n_tab, 0)),
            pl.BlockSpec((tm, ROPE_HALF), lambda i, j: (i % n_tab, 0)),
        ],
        out_specs=pl.BlockSpec((tm, tn), lambda i, j: (i, j)),
        out_shape=jax.ShapeDtypeStruct((rows, PROJ_W), BF16),
        compiler_params=pltpu.CompilerParams(
            dimension_semantics=("parallel", "arbitrary"), vmem_limit_bytes=_vmem_limit(est)),
        name="proj",
    )(n, w_all, cos, sin)


def _gates_kernel(n_ref, w_ref, wdtt_ref, o_ref, dtt_ref, *, batched_dtt):
    @pl.when(pl.program_id(1) == 0)
    def _():
        dtt = _dot_nt(wdtt_ref[...], n_ref[...])
        if batched_dtt:
            dtt_ref[0] = dtt
        else:
            dtt_ref[...] = dtt

    o_ref[...] = jnp.dot(n_ref[...], w_ref[...], preferred_element_type=F32).astype(BF16)


def _gates(n, w_gates, w_dtt, *, batch, seq):
    rows = n.shape[0]
    tm = min(1024, rows)
    tn = 2048
    batched_dtt = tm <= seq
    if batched_dtt:
        per_seq = seq // tm
        dtt_spec = pl.BlockSpec((1, DT_PAD, tm), lambda i, j: (i // per_seq, 0, i % per_seq))
        dtt_shape = jax.ShapeDtypeStruct((batch, DT_PAD, seq), F32)
    else:
        dtt_spec = pl.BlockSpec((DT_PAD, tm), lambda i, j: (0, i))
        dtt_shape = jax.ShapeDtypeStruct((DT_PAD, rows), F32)
    est = (2 * tm * D_MODEL * 2 + 2 * D_MODEL * tn * 2 + 2 * tm * tn * 2 + 2 * tm * tn * 4
           + 2 * D_MODEL * DT_PAD * 2 + 4 * tm * DT_PAD * 4)
    gates, dtt = pl.pallas_call(
        functools.partial(_gates_kernel, batched_dtt=batched_dtt),
        grid=(rows // tm, 2 * D_MODEL // tn),
        in_specs=[
            pl.BlockSpec((tm, D_MODEL), lambda i, j: (i, 0)),
            pl.BlockSpec((D_MODEL, tn), lambda i, j: (0, j)),
            pl.BlockSpec((DT_PAD, D_MODEL), lambda i, j: (0, 0)),
        ],
        out_specs=[pl.BlockSpec((tm, tn), lambda i, j: (i, j)), dtt_spec],
        out_shape=[jax.ShapeDtypeStruct((rows, 2 * D_MODEL), BF16), dtt_shape],
        compiler_params=pltpu.CompilerParams(
            dimension_semantics=("parallel", "arbitrary"), vmem_limit_bytes=_vmem_limit(est)),
        name="gates",
    )(n, w_gates, w_dtt)
    if not batched_dtt:
        dtt = dtt.reshape(DT_PAD, batch, seq).transpose(1, 0, 2)
    return gates, dtt


def _ret_kernel(*refs, chunk, has_init):
    if has_init:
        q_ref, k_ref, v_ref, g_ref, nw_ref, s0_ref, o_ref, s_ref, mask_ref = refs
    else:
        q_ref, k_ref, v_ref, g_ref, nw_ref, o_ref, s_ref, mask_ref = refs
    c = pl.program_id(1)
    log_decay = [math.log1p(-(2.0 ** (-5.0 - hd))) for hd in range(RET_HEADS)]

    @pl.when((pl.program_id(0) == 0) & (c == 0))
    def _():
        ii = lax.broadcasted_iota(jnp.int32, (chunk, chunk), 0)
        jj = lax.broadcasted_iota(jnp.int32, (chunk, chunk), 1)
        causal = ii >= jj
        diff = jnp.where(causal, ii - jj, 0).astype(F32)
        for hd in range(RET_HEADS):
            mask_ref[hd] = jnp.where(causal, jnp.exp(diff * log_decay[hd]), 0.0)

    @pl.when(c == 0)
    def _():
        if has_init:
            s_ref[...] = s0_ref[...]
        else:
            s_ref[...] = jnp.zeros_like(s_ref)

    idx = lax.broadcasted_iota(jnp.int32, (chunk, 1), 0).astype(F32)
    for r0 in range(0, q_ref.shape[1], chunk):
        rows = slice(r0, r0 + chunk)
        for hd in range(RET_HEADS):
            lg = log_decay[hd]
            qk_cols = slice(hd * RET_DK, (hd + 1) * RET_DK)
            v_cols = slice(hd * RET_DV, (hd + 1) * RET_DV)
            decay_mask = mask_ref[hd]
            inner_decay = jnp.exp((idx + 1.0) * lg)
            state_decay = jnp.exp((chunk - 1.0 - idx) * lg)
            chunk_decay = math.exp(chunk * lg)
            q = q_ref[0, rows, qk_cols]
            k = k_ref[0, rows, qk_cols]
            v = v_ref[0, rows, v_cols]
            s = s_ref[0, hd]
            scores = _dot_nt(q, k) * decay_mask
            o = _dot(scores, v) + _dot(q, s) * inner_decay
            s_ref[0, hd] = s * chunk_decay + _dot_tn(k.astype(F32) * state_decay, v)
            mu = jnp.mean(o, axis=-1, keepdims=True)
            var = jnp.mean(jnp.square(o - mu), axis=-1, keepdims=True)
            on = (o - mu) * lax.rsqrt(var + EPS) * nw_ref[:, v_cols]
            gate = g_ref[0, rows, v_cols].astype(F32)
            o_ref[0, rows, v_cols] = (_silu(gate) * on).astype(BF16)


def _retention(proj3, norm_w, s0, *, chunk, block):
    b, seq, _ = proj3.shape
    has_init = s0 is not None
    assert block % chunk == 0 and seq % block == 0
    state_block = pl.BlockSpec((1, RET_HEADS, RET_DK, RET_DV), lambda bi, c: (bi, 0, 0, 0))
    in_specs = [
        pl.BlockSpec((1, block, RET_QK), lambda bi, c: (bi, c, OFF_Q // RET_QK)),
        pl.BlockSpec((1, block, RET_QK), lambda bi, c: (bi, c, OFF_K // RET_QK)),
        pl.BlockSpec((1, block, RET_V), lambda bi, c: (bi, c, OFF_V // RET_V)),
        pl.BlockSpec((1, block, RET_V), lambda bi, c: (bi, c, OFF_G // RET_V)),
        pl.BlockSpec((1, RET_V), lambda bi, c: (0, 0)),
    ]
    args = [proj3, proj3, proj3, proj3, norm_w]
    if has_init:
        in_specs.append(state_block)
        args.append(s0)
    state_bytes = RET_HEADS * RET_DK * RET_DV * 4
    est = (2 * block * (2 * RET_QK + 3 * RET_V) * 2 + (4 if has_init else 2) * state_bytes
           + 12 * chunk * max(chunk, RET_DV) * 4 + RET_HEADS * chunk * chunk * 4)
    return pl.pallas_call(
        functools.partial(_ret_kernel, chunk=chunk, has_init=has_init),
        grid=(b, seq // block),
        in_specs=in_specs,
        out_specs=[pl.BlockSpec((1, block, RET_V), lambda bi, c: (bi, c, 0)), state_block],
        out_shape=[jax.ShapeDtypeStruct((b, seq, RET_V), BF16),
                   jax.ShapeDtypeStruct((b, RET_HEADS, RET_DK, RET_DV), F32)],
        scratch_shapes=[pltpu.VMEM((RET_HEADS, chunk, chunk), F32)],
        compiler_params=pltpu.CompilerParams(
            dimension_semantics=("arbitrary", "arbitrary"), vmem_limit_bytes=_vmem_limit(est)),
        name="retention",
    )(*args)


def _expand_table():
    e = np.zeros((SSM_GROUPS, 2 * DT_PAD, SSM_GW), np.float32)
    for g in range(SSM_GROUPS):
        for col in range(SSM_GW):
            head = g * SSM_HPG + col // SSM_HEAD_DIM
            for piece in range(3):
                e[g, piece * SSM_HEADS + head, col] = 1.0
    return jnp.asarray(e, BF16)


def _ssd_kernel(*refs, chunk, has_init, has_dt):
    refs = iter(refs)
    take = lambda k: [next(refs) for _ in range(k)]
    z_ref, x_ref, b_ref, c_ref = take(4)
    dt_ref, = take(1) if has_dt else (None,)
    dtt_ref, cw_ref, cb_ref, dtb_ref, dtbt_ref, al_ref, alt_ref, dsk_ref, nw_ref, ex_ref = take(10)
    h0_ref, cs0_ref = take(2) if has_init else (None, None)
    y_ref, hout_ref, csout_ref, ht_ref, xc_ref = take(5)
    fix_ref, ext_ref = take(2) if has_init else (None, None)
    c = pl.program_id(1)
    taps = SSM_CONV - 1
    kk = CONV_TAIL + chunk

    @pl.when(c == 0)
    def _():
        for g in range(SSM_GROUPS):
            if has_init:
                blk = h0_ref[0, g * SSM_HPG:(g + 1) * SSM_HPG].reshape(SSM_GW, SSM_STATE)
                ht_ref[g] = blk.T
            else:
                ht_ref[g] = jnp.zeros((SSM_STATE, SSM_GW), F32)
        xc_ref[0:CONV_TAIL, :] = jnp.zeros((CONV_TAIL, SSM_XBC), BF16)
        if has_init:
            ext_ref[...] = jnp.zeros_like(ext_ref)
            ext_ref[SUBLANES - taps:SUBLANES, :] = cs0_ref[0]
            fix = jnp.zeros((SUBLANES, SSM_XBC), F32)
            for i in range(taps):
                fix = fix + ext_ref[SUBLANES - taps + i:2 * SUBLANES - taps + i, :] * cw_ref[i:i + 1, :]
            fix_ref[...] = fix

    if has_init:
        @pl.when(c == 1)
        def _():
            fix_ref[...] = jnp.zeros_like(fix_ref)

    block = x_ref.shape[1]
    xc_ref[CONV_TAIL:CONV_TAIL + block, 0:SSM_INNER] = x_ref[0]
    xc_ref[CONV_TAIL:CONV_TAIL + block, SSM_INNER:SSM_INNER + SSM_BC] = b_ref[0]
    xc_ref[CONV_TAIL:CONV_TAIL + block, SSM_INNER + SSM_BC:SSM_XBC] = c_ref[0]

    group = taps * SUBLANES
    si = lax.broadcasted_iota(jnp.int32, (taps * chunk, kk), 0)
    sj = lax.broadcasted_iota(jnp.int32, (taps * chunk, kk), 1)
    tile = (si * 2731) >> 16
    rem = si - group * tile
    src_row = tile * SUBLANES + (rem & (SUBLANES - 1)) + (rem >> 3) + (CONV_TAIL - taps)
    shift = jnp.where(sj == src_row, 1.0, 0.0).astype(BF16)

    def conv(src_ref, r0, lo, col0, width):
        cur = src_ref[0, r0:r0 + chunk, lo:lo + width]
        lo = col0 + lo
        prev = jnp.dot(shift, xc_ref[r0:r0 + kk, lo:lo + width], preferred_element_type=F32)
        prev = prev.reshape(chunk // SUBLANES, taps, SUBLANES, width)
        half = cb_ref[:, lo:lo + width] + cur.astype(F32) * cw_ref[taps:taps + 1, lo:lo + width]
        for i in range(taps):
            half = half + prev[:, i].reshape(chunk, width) * cw_ref[i:i + 1, lo:lo + width]
        if has_init and r0 == 0:
            half = jnp.concatenate([half[:SUBLANES] + fix_ref[:, lo:lo + width], half[SUBLANES:]], axis=0)
        return half * jnp.tanh(half) + half

    a_row = -jnp.exp(al_ref[...]) * LOG2E
    a_col = -jnp.exp(alt_ref[...]) * LOG2E
    ii = lax.broadcasted_iota(jnp.int32, (chunk, chunk), 0)
    jj = lax.broadcasted_iota(jnp.int32, (chunk, chunk), 1)
    tri = ii >= jj
    lower = tri.astype(BF16)
    upper = (ii <= jj).astype(BF16)
    first_half = lax.broadcasted_iota(jnp.int32, (chunk, LANES), 1) < SSM_HEADS
    lane_head = lax.broadcasted_iota(jnp.int32, (chunk, LANES), 1) < SSM_HEAD_DIM
    head_a = jnp.where(lane_head, 1.0, 0.0).astype(BF16)
    head_b = jnp.where(lane_head, 0.0, 1.0).astype(BF16)

    def pieces(v):
        hi, mid, lo = _split3(v)
        return jnp.concatenate([jnp.where(first_half, hi, mid), lo], axis=1)

    def one_chunk(r0):
        rows = slice(r0, r0 + chunk)
        dtt_raw = dtt_ref[0, :, rows]
        dt_raw = dt_ref[0, rows, :] if has_dt else dtt_raw.T
        dt = _softplus(dt_raw + dtb_ref[...])
        dtt = _softplus(dtt_raw + dtbt_ref[...])
        cum = sum(jnp.dot(lower, p, preferred_element_type=F32) for p in _split3(dt * a_row))
        cumt = sum(jnp.dot(p, upper, preferred_element_type=F32) for p in _split3(dtt * a_col))
        dt_p = pieces(dt)
        cum_p = pieces(cum)

        def stage_a(g):
            cols = slice(g * SSM_GW, (g + 1) * SSM_GW)
            xs = conv(x_ref, r0, g * SSM_GW, 0, SSM_GW)
            bm = conv(b_ref, r0, g * SSM_STATE, SSM_INNER, SSM_STATE)
            cm = conv(c_ref, r0, g * SSM_STATE, SSM_INNER + SSM_BC, SSM_STATE)
            cbm = jnp.where(tri, _dot_nt(cm, bm), 0.0)
            dt_e = jnp.dot(dt_p, ex_ref[g], preferred_element_type=F32)
            cum_e = jnp.dot(cum_p, ex_ref[g], preferred_element_type=F32)
            e_all = jnp.exp2(cum_e)
            xdt = xs * dt_e
            xd = xdt * jnp.exp2(cum_e[chunk - 1:chunk, :] - cum_e)
            return xs, bm, cm, cbm, e_all, xd, xdt.astype(BF16)

        nxt = stage_a(0)
        for g in range(SSM_GROUPS):
            xs, bm, cm, cbm, e_all, xd, xdt16 = nxt
            if g + 1 < SSM_GROUPS:
                nxt = stage_a(g + 1)
            cols = slice(g * SSM_GW, (g + 1) * SSM_GW)
            y_parts = []
            for s in range(SSM_GW // LANES):
                rhs = xdt16[:, s * LANES:(s + 1) * LANES]
                ms = []
                for j in (g * SSM_HPG + 2 * s, g * SSM_HPG + 2 * s + 1):
                    seg = (jnp.broadcast_to(cum[:, j:j + 1], (chunk, chunk))
                           - jnp.broadcast_to(cumt[j:j + 1, :], (chunk, chunk)))
                    ms.append((cbm * jnp.exp2(jnp.minimum(seg, 0.0))).astype(BF16))
                if chunk % LANES == 0:
                    rhs2 = jnp.concatenate([rhs * head_a, rhs * head_b], axis=0)
                    y_parts.append(jnp.dot(jnp.concatenate(ms, axis=1), rhs2, preferred_element_type=F32))
                else:
                    y_parts.append(jnp.dot(ms[0], rhs * head_a, preferred_element_type=F32)
                                   + jnp.dot(ms[1], rhs * head_b, preferred_element_type=F32))
            htg = ht_ref[g]
            y = jnp.concatenate(y_parts, axis=1) + _dot(cm, htg) * e_all + xs * dsk_ref[:, cols]
            ht_ref[g] = htg * e_all[chunk - 1:chunk, :] + _dot_tn(bm, xd)
            y = y * _silu(z_ref[0, rows, cols].astype(F32))
            y_ref[0, rows, cols] = _rms(y, nw_ref[:, cols]).astype(BF16)

    for r0 in range(0, block, chunk):
        one_chunk(r0)

    xc_ref[0:CONV_TAIL, :] = xc_ref[block:block + CONV_TAIL, :]

    @pl.when(c == pl.num_programs(1) - 1)
    def _():
        csout_ref[0] = xc_ref[block:block + CONV_TAIL, :].astype(F32)[CONV_TAIL - taps:CONV_TAIL, :]
        for g in range(SSM_GROUPS):
            hout_ref[0, g * SSM_HPG:(g + 1) * SSM_HPG] = ht_ref[g].T.reshape(SSM_HPG, SSM_HEAD_DIM, SSM_STATE)


def _ssd(proj3, dtt3, conv_w, conv_b, dt_bias, a_log, d_skip, norm_w, h0, cs0, *, chunk, block):
    b, seq, _ = proj3.shape
    has_init = h0 is not None
    has_dt = chunk % LANES != 0
    assert chunk % SUBLANES == 0 and chunk >= SUBLANES and block % chunk == 0 and seq % block == 0
    assert chunk & (chunk - 1) == 0
    dtb = jnp.tile(dt_bias.astype(F32), 2)
    al = jnp.tile(a_log.astype(F32), 2)
    dsk = jnp.repeat(d_skip.astype(F32), SSM_HEAD_DIM).reshape(1, SSM_INNER)
    const = lambda *shape: pl.BlockSpec(shape, lambda bi, c: (0,) * len(shape))
    in_specs = [
        pl.BlockSpec((1, block, SSM_INNER), lambda bi, c: (bi, c, OFF_Z // SSM_INNER)),
        pl.BlockSpec((1, block, SSM_INNER), lambda bi, c: (bi, c, OFF_X // SSM_INNER)),
        pl.BlockSpec((1, block, SSM_BC), lambda bi, c: (bi, c, OFF_B // SSM_BC)),
        pl.BlockSpec((1, block, SSM_BC), lambda bi, c: (bi, c, OFF_C // SSM_BC)),
    ]
    args = [proj3, proj3, proj3, proj3]
    if has_dt:
        in_specs.append(pl.BlockSpec((1, block, DT_PAD), lambda bi, c: (bi, c, 0)))
        args.append(dtt3.transpose(0, 2, 1))
    in_specs += [
        pl.BlockSpec((1, DT_PAD, block), lambda bi, c: (bi, 0, c)),
        const(SSM_CONV, SSM_XBC), const(1, SSM_XBC), const(1, DT_PAD), const(DT_PAD, 1),
        const(1, DT_PAD), const(DT_PAD, 1), const(1, SSM_INNER), const(1, SSM_INNER),
        const(SSM_GROUPS, 2 * DT_PAD, SSM_GW),
    ]
    args += [dtt3, 0.5 * conv_w.astype(F32), 0.5 * conv_b.astype(F32).reshape(1, SSM_XBC),
             dtb.reshape(1, DT_PAD), dtb.reshape(DT_PAD, 1), al.reshape(1, DT_PAD), al.reshape(DT_PAD, 1),
             dsk, norm_w.astype(F32).reshape(1, SSM_INNER), _expand_table()]
    h_block = pl.BlockSpec((1, SSM_HEADS, SSM_HEAD_DIM, SSM_STATE), lambda bi, c: (bi, 0, 0, 0))
    cs_block = pl.BlockSpec((1, SSM_CONV - 1, SSM_XBC), lambda bi, c: (bi, 0, 0))
    scratch = [pltpu.VMEM((SSM_GROUPS, SSM_STATE, SSM_GW), F32),
               pltpu.VMEM((CONV_TAIL + block, SSM_XBC), BF16)]
    if has_init:
        in_specs += [h_block, cs_block]
        args += [h0, cs0]
        scratch += [pltpu.VMEM((SUBLANES, SSM_XBC), F32), pltpu.VMEM((2 * SUBLANES, SSM_XBC), F32)]
    state_bytes = SSM_HEADS * SSM_HEAD_DIM * SSM_STATE * 4
    est = (2 * block * (3 * SSM_INNER + 2 * SSM_BC) * 2 + (5 if has_init else 3) * state_bytes
           + (block + CONV_TAIL) * SSM_XBC * 2 + 32 * chunk * max(chunk, SSM_GW) * 4 + 8 * 1024 * 1024)
    return pl.pallas_call(
        functools.partial(_ssd_kernel, chunk=chunk, has_init=has_init, has_dt=has_dt),
        grid=(b, seq // block),
        in_specs=in_specs,
        out_specs=[pl.BlockSpec((1, block, SSM_INNER), lambda bi, c: (bi, c, 0)), h_block, cs_block],
        out_shape=[jax.ShapeDtypeStruct((b, seq, SSM_INNER), BF16),
                   jax.ShapeDtypeStruct((b, SSM_HEADS, SSM_HEAD_DIM, SSM_STATE), F32),
                   jax.ShapeDtypeStruct((b, SSM_CONV - 1, SSM_XBC), F32)],
        scratch_shapes=scratch,
        compiler_params=pltpu.CompilerParams(
            dimension_semantics=("parallel", "arbitrary"), vmem_limit_bytes=_vmem_limit(est)),
        name="ssd",
    )(*args)


def _merge_kernel(ar_ref, as_ref, wr_ref, ws_ref, ga_ref, gb_ref, ba_ref, bb_ref, o_ref, *, sub):
    for lo in range(0, o_ref.shape[0], sub):
        rows = slice(lo, lo + sub)
        yr = jnp.dot(ar_ref[rows, :], wr_ref[...], preferred_element_type=F32)
        ys = jnp.dot(as_ref[rows, :], ws_ref[...], preferred_element_type=F32)
        ga = _sigmoid(ga_ref[rows, :].astype(F32) + ba_ref[...])
        gb = _sigmoid(gb_ref[rows, :].astype(F32) + bb_ref[...])
        o_ref[rows, :] = (ga * yr + gb * ys).astype(BF16)


def _merge(a_ret, a_ssm, w_ret, w_ssm, gates, b_gate):
    rows = a_ret.shape[0]
    tm = min(1024, rows)
    tn = 512
    nb = D_MODEL // tn
    est = 2 * 2 * tm * RET_V * 2 + 2 * 2 * RET_V * tn * 2 + 8 * tm * tn * 4
    return pl.pallas_call(
        functools.partial(_merge_kernel, sub=min(512, tm)),
        grid=(rows // tm, nb),
        in_specs=[
            pl.BlockSpec((tm, RET_V), lambda i, j: (i, 0)),
            pl.BlockSpec((tm, SSM_INNER), lambda i, j: (i, 0)),
            pl.BlockSpec((RET_V, tn), lambda i, j: (0, j)),
            pl.BlockSpec((SSM_INNER, tn), lambda i, j: (0, j)),
            pl.BlockSpec((tm, tn), lambda i, j: (i, j)),
            pl.BlockSpec((tm, tn), lambda i, j: (i, nb + j)),
            pl.BlockSpec((1, tn), lambda i, j: (0, j)),
            pl.BlockSpec((1, tn), lambda i, j: (0, nb + j)),
        ],
        out_specs=pl.BlockSpec((tm, tn), lambda i, j: (i, j)),
        out_shape=jax.ShapeDtypeStruct((rows, D_MODEL), BF16),
        compiler_params=pltpu.CompilerParams(
            dimension_semantics=("parallel", "arbitrary"), vmem_limit_bytes=_vmem_limit(est)),
        name="merge",
    )(a_ret, a_ssm, w_ret, w_ssm, gates, gates, b_gate, b_gate)


def _outproj_kernel(m_ref, w_ref, h_ref, o_ref):
    o_ref[...] = h_ref[...] + jnp.dot(m_ref[...], w_ref[...], preferred_element_type=F32)


def _outproj(m, w_out, h):
    rows = m.shape[0]
    tm = min(1024, rows)
    tn = 1024
    est = 2 * tm * D_MODEL * 2 + 2 * D_MODEL * tn * 2 + 5 * tm * tn * 4
    return pl.pallas_call(
        _outproj_kernel,
        grid=(rows // tm, D_MODEL // tn),
        in_specs=[
            pl.BlockSpec((tm, D_MODEL), lambda i, j: (i, 0)),
            pl.BlockSpec((D_MODEL, tn), lambda i, j: (0, j)),
            pl.BlockSpec((tm, tn), lambda i, j: (i, j)),
        ],
        out_specs=pl.BlockSpec((tm, tn), lambda i, j: (i, j)),
        out_shape=jax.ShapeDtypeStruct((rows, D_MODEL), F32),
        compiler_params=pltpu.CompilerParams(
            dimension_semantics=("parallel", "arbitrary"), vmem_limit_bytes=_vmem_limit(est)),
        name="outproj",
    )(m, w_out, h)


def _layer(x, s_ret, s_ssm, s_conv, pos0, p, *, ret_chunk, ssd_chunk):
    b, seq, _ = x.shape
    rows = b * seq
    h, n = _ffn(x.reshape(rows, D_MODEL), p["norm_ffn1"], p["ffn1_w_gate"], p["ffn1_w_up"], p["ffn1_w_down"],
                p["norm_mix"], tail="emit_norm")
    proj = _proj(n, p["w_all"], p["inv_freq"], seq=seq, pos0=pos0)
    gates, dtt3 = _gates(n, p["w_gates"], p["w_dtt"], batch=b, seq=seq)
    proj3 = proj.reshape(b, seq, PROJ_W)
    a_ret, ret_new = _retention(proj3, p["ret_norm_w"], s_ret, chunk=min(ret_chunk, seq),
                                block=min(RET_CHUNKS_PER_STEP * ret_chunk, seq))
    a_ssm, ssm_new, conv_new = _ssd(proj3, dtt3, p["conv_w"], p["conv_b"], p["dt_bias"], p["a_log"],
                                    p["d_skip"], p["ssm_norm_w"], s_ssm, s_conv,
                                    chunk=min(ssd_chunk, seq), block=min(SSD_CHUNKS_PER_STEP * ssd_chunk, seq))
    m = _merge(a_ret.reshape(rows, RET_V), a_ssm.reshape(rows, SSM_INNER), p["w_out_ret"], p["w_out_ssm"],
               gates, p["b_gate"])
    h2 = _outproj(m, p["w_out"], h)
    y, = _ffn(h2, p["norm_ffn2"], p["ffn2_w_gate"], p["ffn2_w_up"], p["ffn2_w_down"], p["norm_final"],
              tail="final_norm")
    return y.reshape(b, seq, D_MODEL), ret_new, ssm_new, conv_new


def kernel(x_prompt, x_sample, state_ret, state_ssm, state_conv, norm_ffn1, ffn1_w_gate, ffn1_w_up, ffn1_w_down, norm_mix, w_in, b_gate, ret_norm_w, w_out_ret, conv_w, conv_b, dt_bias, a_log, d_skip, ssm_norm_w, w_out_ssm, w_out, norm_ffn2, ffn2_w_gate, ffn2_w_up, ffn2_w_down, norm_final):
    row = lambda v: v.astype(F32).reshape(1, -1)
    dt_lo = 2 * RET_QK + 2 * RET_V + SSM_INNER + SSM_XBC
    p = dict(
        norm_ffn1=row(norm_ffn1), norm_mix=row(norm_mix), norm_ffn2=row(norm_ffn2), norm_final=row(norm_final),
        ffn1_w_gate=ffn1_w_gate.astype(BF16), ffn1_w_up=ffn1_w_up.astype(BF16), ffn1_w_down=ffn1_w_down.astype(BF16),
        ffn2_w_gate=ffn2_w_gate.astype(BF16), ffn2_w_up=ffn2_w_up.astype(BF16), ffn2_w_down=ffn2_w_down.astype(BF16),
        w_all=w_in.astype(BF16), w_gates=w_in[:, dt_lo + SSM_HEADS:].astype(BF16),
        w_dtt=jnp.tile(w_in[:, dt_lo:dt_lo + SSM_HEADS].T, (2, 1)).astype(BF16),
        inv_freq=(ROPE_BASE ** (-jnp.arange(ROPE_HALF, dtype=F32) / ROPE_HALF)).reshape(1, ROPE_HALF),
        b_gate=row(b_gate), ret_norm_w=row(ret_norm_w),
        w_out_ret=w_out_ret.astype(BF16), w_out_ssm=w_out_ssm.astype(BF16), w_out=w_out.astype(BF16),
        conv_w=conv_w, conv_b=conv_b, dt_bias=dt_bias, a_log=a_log, d_skip=d_skip, ssm_norm_w=ssm_norm_w,
    )
    y_p, ret_p, ssm_p, conv_p = _layer(x_prompt, None, None, None, 0, p, ret_chunk=256, ssd_chunk=128)
    y_s, ret_s, ssm_s, conv_s = _layer(x_sample, state_ret.astype(F32), state_ssm.astype(F32),
                                       state_conv.astype(F32), PAST_LEN, p, ret_chunk=256, ssd_chunk=128)
    return (y_p, y_s, ret_p, ssm_p, conv_p, ret_s, ssm_s, conv_s)
```
